```python
import math
import jax, jax.numpy as jnp
from jax import lax
import numpy as np

D_MODEL = 2048
BATCH = 2
SEQ = 8192
DEPTH = 1

CHUNK = 64
D_CONV = D_MODEL // 2
D_MLSTM = D_MODEL - D_CONV
N_HEADS_MLSTM = 4
HEAD_DIM = D_MLSTM // N_HEADS_MLSTM
CONV_WIDTH = 31
QK_CONV_WIDTH = 4
N_EXPERTS = 256
TOP_K = 8
N_GROUPS = 8
TOPK_GROUPS = 4
D_EXPERT = 512
ROUTED_SCALE = 2.5
EXPERT_BLOCK = 128
EPS = 1e-6
STAB_INIT = -1e30
D_IN = 2 * D_CONV + 4 * D_MLSTM + 2 * N_HEADS_MLSTM

kernel_name = 'hymba_conformer_mlstm_moe_block'


def rmsnorm(x, g):
    xf = x.astype(jnp.float32)
    y = xf * lax.rsqrt(jnp.mean(xf * xf, axis=-1, keepdims=True) + EPS)
    return (y * g.astype(jnp.float32)).astype(x.dtype)


def layernorm(x, g, b):
    xf = x.astype(jnp.float32)
    mu = jnp.mean(xf, axis=-1, keepdims=True)
    var = jnp.mean(jnp.square(xf - mu), axis=-1, keepdims=True)
    y = (xf - mu) * lax.rsqrt(var + EPS) * g.astype(jnp.float32) + b.astype(jnp.float32)
    return y.astype(x.dtype)


def headnorm(h, g):
    mu = jnp.mean(h, axis=-1, keepdims=True)
    var = jnp.mean(jnp.square(h - mu), axis=-1, keepdims=True)
    return (h - mu) * lax.rsqrt(var + EPS) * g.astype(jnp.float32).reshape(N_HEADS_MLSTM, HEAD_DIM)


def causal_dwconv(u, w, b):
    width, ch = w.shape
    y = lax.conv_general_dilated(
        u, w[:, None, :].astype(u.dtype), window_strides=(1,), padding=[(width - 1, 0)],
        dimension_numbers=('NWC', 'WIO', 'NWC'), feature_group_count=ch)
    return y + b.astype(u.dtype)


def mlstm_core(q, k, v, ig, fg):
    bsz, seqlen, nh, dh = q.shape
    nc = seqlen // CHUNK

    def chunks(t):
        t = t.astype(jnp.float32).reshape(bsz, nc, CHUNK, nh, *t.shape[3:])
        return jnp.moveaxis(t, (1, 3), (0, 2))

    qc, kc, vc = chunks(q), chunks(k) * (dh ** -0.5), chunks(v)
    ic = chunks(ig)
    lfc = jax.nn.log_sigmoid(chunks(fg))
    causal = jnp.tril(jnp.ones((CHUNK, CHUNK), dtype=bool))

    def step(carry, inp):
        cmat, nvec, m = carry
        qt, kt, vt, it, lf = inp
        b = jnp.cumsum(lf, axis=-1)
        d = jnp.where(causal, b[..., :, None] - b[..., None, :] + it[..., None, :], -jnp.inf)
        m_inter = b + m[..., None]
        m_t = jnp.maximum(m_inter, jnp.max(d, axis=-1))
        s = jnp.einsum('bhtd,bhsd->bhts', qt, kt) * jnp.exp(d - m_t[..., None])
        a = jnp.exp(m_inter - m_t)
        num = jnp.einsum('bhts,bhsd->bhtd', s, vt) + a[..., None] * jnp.einsum('bhvk,bhtk->bhtv', cmat, qt)
        den = jnp.sum(s, axis=-1) + a * jnp.einsum('bhk,bhtk->bht', nvec, qt)
        h = num / jnp.maximum(jnp.abs(den), jnp.exp(-m_t))[..., None]
        g = b[..., -1:] - b + it
        m_new = jnp.maximum(b[..., -1] + m, jnp.max(g, axis=-1))
        wk = jnp.exp(g - m_new[..., None])
        decay = jnp.exp(b[..., -1] + m - m_new)
        cmat = decay[..., None, None] * cmat + jnp.einsum('bhsv,bhsk->bhvk', vt * wk[..., None], kt)
        nvec = decay[..., None] * nvec + jnp.einsum('bhs,bhsk->bhk', wk, kt)
        return (cmat, nvec, m_new), h

    init = (jnp.zeros((bsz, nh, dh, dh), jnp.float32),
            jnp.zeros((bsz, nh, dh), jnp.float32),
            jnp.full((bsz, nh), STAB_INIT, jnp.float32))
    _, hs = lax.scan(step, init, (qc, kc, vc, ic, lfc))
    return jnp.moveaxis(hs, (0, 2), (1, 3)).reshape(bsz, seqlen, nh, dh)


def swiglu(h, w_gu, w_d):
    g, u = jnp.split(h @ w_gu, 2, axis=-1)
    return (jax.nn.silu(g) * u) @ w_d


def routed_moe(h, w_router, router_bias, w_gate_up, w_down, layer):
    n_tok, d = h.shape
    scores = jax.nn.sigmoid(h.astype(jnp.float32) @ w_router.astype(jnp.float32))
    choice = scores + router_bias.astype(jnp.float32)
    grp_score = jnp.sum(lax.top_k(choice.reshape(n_tok, N_GROUPS, -1), 2)[0], axis=-1)
    _, grp_idx = lax.top_k(grp_score, TOPK_GROUPS)
    grp_keep = jnp.any(grp_idx[..., None] == jnp.arange(N_GROUPS), axis=-2)
    keep = jnp.repeat(grp_keep, N_EXPERTS // N_GROUPS, axis=-1)
    _, top_idx = lax.top_k(jnp.where(keep, choice, -jnp.inf), TOP_K)
    top_w = jnp.take_along_axis(scores, top_idx, axis=-1)
    top_w = top_w / jnp.sum(top_w, axis=-1, keepdims=True) * ROUTED_SCALE

    n_assign = n_tok * TOP_K
    flat_e = top_idx.reshape(-1)
    order = jnp.argsort(flat_e)
    se = flat_e[order]
    stok = (order // TOP_K).astype(jnp.int32)
    sw = top_w.reshape(-1)[order]
    sizes = jnp.bincount(flat_e, length=N_EXPERTS)
    starts = jnp.cumsum(sizes) - sizes
    padded = (sizes + EXPERT_BLOCK - 1) // EXPERT_BLOCK * EXPERT_BLOCK
    pad_ends = jnp.cumsum(padded)
    dest = pad_ends[se] - padded[se] + jnp.arange(n_assign) - starts[se]
    n_blocks = -(-(n_assign + N_EXPERTS * (EXPERT_BLOCK - 1)) // EXPERT_BLOCK)
    n_slots = n_blocks * EXPERT_BLOCK
    slot_tok = jnp.zeros((n_slots,), jnp.int32).at[dest].set(stok)
    slot_w = jnp.zeros((n_slots,), h.dtype).at[dest].set(sw.astype(h.dtype))
    block_e = jnp.minimum(
        jnp.searchsorted(pad_ends, jnp.arange(n_blocks) * EXPERT_BLOCK, side='right'), N_EXPERTS - 1)

    def step(acc, blk):
        tok, w, e = blk
        xb = h[tok]
        y = swiglu(xb, w_gate_up[layer, e], w_down[layer, e])
        return acc.at[tok].add((y * w[:, None]).astype(acc.dtype)), None

    out, _ = lax.scan(step, jnp.zeros_like(h),
                      (slot_tok.reshape(n_blocks, EXPERT_BLOCK), slot_w.reshape(n_blocks, EXPERT_BLOCK), block_e))
    return out


def setup_inputs(seed: int = 0) -> dict:
    key = jax.random.key(seed)
    ks = jax.random.split(key, 26)
    f32 = jnp.float32
    L = DEPTH
    H = N_HEADS_MLSTM

    def nrm(k, shape, scale):
        return jax.random.normal(k, shape, f32) * scale

    return {
        'x': nrm(ks[0], (BATCH, SEQ, D_MODEL), 1.0),
        'c': nrm(ks[1], (BATCH, D_MODEL), 1.0),
        'w_ada': nrm(ks[2], (L, D_MODEL, 6 * D_MODEL), 0.5 * D_MODEL ** -0.5),
        'b_ada': nrm(ks[3], (L, 6 * D_MODEL), 0.02),
        'g_mix': 1.0 + nrm(ks[4], (L, D_MODEL), 0.02),
        'w_in': nrm(ks[5], (L, D_MODEL, D_IN), D_MODEL ** -0.5),
        'conv_w': nrm(ks[6], (L, CONV_WIDTH, D_CONV), CONV_WIDTH ** -0.5),
        'conv_b': nrm(ks[7], (L, D_CONV), 0.02),
        'conv_ln_g': 1.0 + nrm(ks[8], (L, D_CONV), 0.02),
        'conv_ln_b': nrm(ks[9], (L, D_CONV), 0.02),
        'qk_conv_w': nrm(ks[10], (L, QK_CONV_WIDTH, 2 * D_MLSTM), QK_CONV_WIDTH ** -0.5),
        'qk_conv_b': nrm(ks[11], (L, 2 * D_MLSTM), 0.02),
        'b_igate': nrm(ks[12], (L, H), 0.1),
        'b_fgate': jnp.linspace(3.0, 6.0, H, dtype=f32) + nrm(ks[13], (L, H), 0.1),
        'mlstm_norm_g': 1.0 + nrm(ks[14], (L, D_MLSTM), 0.02),
        'w_out': nrm(ks[15], (L, D_MODEL, D_MODEL), D_MODEL ** -0.5),
        'g_ffn': 1.0 + nrm(ks[16], (L, D_MODEL), 0.02),
        'w_router': nrm(ks[17], (L, D_MODEL, N_EXPERTS), D_MODEL ** -0.5),
        'router_bias': nrm(ks[18], (L, N_EXPERTS), 0.01),
        'w_gate_up': nrm(ks[19], (L, N_EXPERTS, D_MODEL, 2 * D_EXPERT), D_MODEL ** -0.5),
        'w_down': nrm(ks[20], (L, N_EXPERTS, D_EXPERT, D_MODEL), D_EXPERT ** -0.5),
        'ws_gate_up': nrm(ks[21], (L, D_MODEL, 2 * D_EXPERT), D_MODEL ** -0.5),
        'ws_down': nrm(ks[22], (L, D_EXPERT, D_MODEL), D_EXPERT ** -0.5),
        'g_final': 1.0 + nrm(ks[23], (D_MODEL,), 0.02),
    }


def reference(x, c, w_ada, b_ada, g_mix, w_in, conv_w, conv_b, conv_ln_g, conv_ln_b,
              qk_conv_w, qk_conv_b, b_igate, b_fgate, mlstm_norm_g, w_out, g_ffn,
              w_router, router_bias, w_gate_up, w_down, ws_gate_up, ws_down, g_final):
    bsz, seqlen, d = x.shape
    H = N_HEADS_MLSTM
    cs = jax.nn.silu(c)
    o_qk = 2 * D_CONV
    o_vo = o_qk + 2 * D_MLSTM
    o_gate = o_vo + 2 * D_MLSTM

    def heads(t):
        return t.reshape(bsz, seqlen, H, HEAD_DIM)

    for l in range(DEPTH):
        mod = cs @ w_ada[l] + b_ada[l]
        sh1, sc1, gt1, sh2, sc2, gt2 = jnp.split(mod, 6, axis=-1)

        h = rmsnorm(x, g_mix[l]) * (1.0 + sc1[:, None]) + sh1[:, None]
        proj = h @ w_in[l]

        glu = proj[..., :D_CONV] * jax.nn.sigmoid(proj[..., D_CONV:o_qk])
        u = causal_dwconv(glu, conv_w[l], conv_b[l])
        u = jax.nn.silu(layernorm(u, conv_ln_g[l], conv_ln_b[l]))

        qk = jax.nn.silu(causal_dwconv(proj[..., o_qk:o_vo], qk_conv_w[l], qk_conv_b[l]))
        v = proj[..., o_vo:o_vo + D_MLSTM]
        og = proj[..., o_vo + D_MLSTM:o_gate]
        ig = proj[..., o_gate:o_gate + H] + b_igate[l]
        fg = proj[..., o_gate + H:] + b_fgate[l]
        hm = mlstm_core(heads(qk[..., :D_MLSTM]), heads(qk[..., D_MLSTM:]), heads(v), ig, fg)
        hm = headnorm(hm, mlstm_norm_g[l]).reshape(bsz, seqlen, D_MLSTM).astype(x.dtype)
        hm = jax.nn.sigmoid(og) * hm

        mix = jnp.concatenate([u, hm.astype(u.dtype)], axis=-1) @ w_out[l]
        x = x + gt1[:, None] * mix

        hn = rmsnorm(x, g_ffn[l]) * (1.0 + sc2[:, None]) + sh2[:, None]
        hf = hn.reshape(bsz * seqlen, d)
        y = routed_moe(hf, w_router[l], router_bias[l], w_gate_up, w_down, l) + swiglu(hf, ws_gate_up[l], ws_down[l])
        x = x + gt2[:, None] * y.reshape(bsz, seqlen, d)

    return rmsnorm(x, g_final)
```

```python
import functools

import jax
import jax.numpy as jnp
from jax import lax
from jax.experimental import pallas as pl
from jax.experimental.pallas import tpu as pltpu

F32 = jnp.float32
BF16 = jnp.bfloat16

EPS = 1e-6
STAB_INIT = -1e30
TOP_K = 8
N_GROUPS = 8
TOPK_GROUPS = 4
ROUTED_SCALE = 2.5
N_HEADS = 4

LANES = 128
SUBLANES = 8
VMEM_LIMIT = 56 * 1024 * 1024

MLSTM_CHUNK = 256
MOE_BLOCK = 128
NEG_INF = float("-inf")


def _cparams(sem):
    return pltpu.CompilerParams(dimension_semantics=sem, vmem_limit_bytes=VMEM_LIMIT)


def _sigmoid(x):
    return 1.0 / (1.0 + jnp.exp(-x))


def _silu(x):
    return x * _sigmoid(x)


def _ada_kernel(c_ref, w_ref, b_ref, o_ref):
    cs = _silu(c_ref[...])
    o_ref[...] = jnp.dot(cs, w_ref[...], preferred_element_type=F32,
                         precision=lax.Precision.HIGHEST) + b_ref[...]


def _ada(c, w_ada, b_ada):
    bsz, d = c.shape
    n = w_ada.shape[1]
    tn = 1536
    cp = jnp.zeros((SUBLANES, d), F32).at[:bsz].set(c)
    out = pl.pallas_call(
        _ada_kernel,
        grid=(n // tn,),
        in_specs=[pl.BlockSpec((SUBLANES, d), lambda j: (0, 0)),
                  pl.BlockSpec((d, tn), lambda j: (0, j)),
                  pl.BlockSpec((1, tn), lambda j: (0, j))],
        out_specs=pl.BlockSpec((SUBLANES, tn), lambda j: (0, j)),
        out_shape=jax.ShapeDtypeStruct((SUBLANES, n), F32),
        compiler_params=_cparams(("arbitrary",)),
        name="ada",
    )(cp, w_ada, b_ada.reshape(1, n))
    return out[:bsz]


def _inproj_kernel(x_ref, sc_ref, sh_ref, g_ref, w_ref, wg_ref, o_ref, og_ref, h_scr):
    @pl.when(pl.program_id(1) == 0)
    def _():
        x = x_ref[...]
        ms = jnp.mean(x * x, axis=-1, keepdims=True)
        h = (x * lax.rsqrt(ms + EPS) * g_ref[...]) * (1.0 + sc_ref[...]) + sh_ref[...]
        hb = h.astype(BF16)
        h_scr[...] = hb
        og_ref[...] = jnp.dot(hb, wg_ref[...], preferred_element_type=F32)

    o_ref[...] = jnp.dot(h_scr[...], w_ref[...], preferred_element_type=F32).astype(o_ref.dtype)


def _inproj(x2, sc, sh, g, w_main, w_gate, seq, tm=512, tn=1536):
    t, d = x2.shape
    n = w_main.shape[1]
    tpb = seq // tm
    return pl.pallas_call(
        _inproj_kernel,
        grid=(t // tm, n // tn),
        in_specs=[pl.BlockSpec((tm, d), lambda i, j: (i, 0)),
                  pl.BlockSpec((None, 1, d), lambda i, j: (i // tpb, 0, 0)),
                  pl.BlockSpec((None, 1, d), lambda i, j: (i // tpb, 0, 0)),
                  pl.BlockSpec((1, d), lambda i, j: (0, 0)),
                  pl.BlockSpec((d, tn), lambda i, j: (0, j)),
                  pl.BlockSpec((d, LANES), lambda i, j: (0, 0))],
        out_specs=[pl.BlockSpec((tm, tn), lambda i, j: (i, j)),
                   pl.BlockSpec((tm, LANES), lambda i, j: (i, 0))],
        out_shape=[jax.ShapeDtypeStruct((t, n), BF16),
                   jax.ShapeDtypeStruct((t, LANES), F32)],
        scratch_shapes=[pltpu.VMEM((tm, d), BF16)],
        compiler_params=_cparams(("arbitrary", "arbitrary")),
        name="inproj",
    )(x2, sc, sh, g, w_main, w_gate)


_CONV_RC = 128
_CONV_HIST = 32


def _convmod_kernel(a_ref, gate_ref, w_ref, b_ref, lg_ref, lb_ref, o_ref, ext_scr, y_scr, *, width):
    tt, ch = a_ref.shape
    hist = _CONV_HIST
    rc = _CONV_RC

    @pl.when(pl.program_id(1) == 0)
    def _():
        ext_scr[0:hist, :] = jnp.zeros((hist, ch), F32)

    @pl.when(pl.program_id(1) != 0)
    def _():
        ext_scr[0:hist, :] = ext_scr[tt:tt + hist, :]

    ext_scr[hist:hist + tt, :] = a_ref[...].astype(F32) * _sigmoid(gate_ref[...].astype(F32))

    off0 = hist - (width - 1)
    nrow = rc + hist
    for r0 in range(0, tt, rc):
        for c0 in range(0, ch, LANES):
            blk = ext_scr[r0:r0 + nrow, c0:c0 + LANES]
            acc = jnp.zeros((rc, LANES), F32)
            for r in range(SUBLANES):
                taps = [j for j in range(width) if (off0 + j) % SUBLANES == r]
                if not taps:
                    continue
                rolled = blk if r == 0 else pltpu.roll(blk, nrow - r, axis=0)
                for j in taps:
                    a0 = off0 + j - r
                    acc = acc + w_ref[j:j + 1, c0:c0 + LANES] * rolled[a0:a0 + rc, :]
            y_scr[r0:r0 + rc, c0:c0 + LANES] = acc

    y = y_scr[...] + b_ref[...]
    mu = jnp.mean(y, axis=-1, keepdims=True)
    yc = y - mu
    var = jnp.mean(yc * yc, axis=-1, keepdims=True)
    z = yc * lax.rsqrt(var + EPS) * lg_ref[...] + lb_ref[...]
    o_ref[...] = _silu(z).astype(o_ref.dtype)


def _convmod(proj, conv_w, conv_b, ln_g, ln_b, bsz, seq, tt=512):
    width, ch = conv_w.shape
    assert width - 1 <= _CONV_HIST and tt % _CONV_RC == 0
    tpb = seq // tt
    vec = lambda: pl.BlockSpec((1, ch), lambda b, i: (0, 0))
    return pl.pallas_call(
        functools.partial(_convmod_kernel, width=width),
        grid=(bsz, tpb),
        in_specs=[pl.BlockSpec((tt, ch), lambda b, i: (b * tpb + i, 0)),
                  pl.BlockSpec((tt, ch), lambda b, i: (b * tpb + i, 1)),
                  pl.BlockSpec((width, ch), lambda b, i: (0, 0)),
                  vec(), vec(), vec()],
        out_specs=pl.BlockSpec((tt, ch), lambda b, i: (b * tpb + i, 0)),
        out_shape=jax.ShapeDtypeStruct((bsz * seq, ch), BF16),
        scratch_shapes=[pltpu.VMEM((tt + _CONV_HIST, ch), F32),
                        pltpu.VMEM((tt, ch), F32)],
        compiler_params=_cparams(("arbitrary", "arbitrary")),
        name="convmod",
    )(proj, proj, conv_w, conv_b.reshape(1, ch), ln_g.reshape(1, ch), ln_b.reshape(1, ch))


def _log_sigmoid(x):
    return -(jnp.maximum(-x, 0.0) + jnp.log(1.0 + jnp.exp(-jnp.abs(x))))


def _mlstm_kernel(q_ref, k_ref, v_ref, og_ref, gates_ref, gbias_ref, cw_ref, cb_ref, ng_ref,
                  o_ref, hist_scr, c_scr, n_scr, m_scr, *, dh, qk_width):
    ln = q_ref.shape[0]
    dm = q_ref.shape[1]
    nh = dm // dh
    hist = SUBLANES

    @pl.when(pl.program_id(1) == 0)
    def _():
        hist_scr[...] = jnp.zeros(hist_scr.shape, F32)
        c_scr[...] = jnp.zeros(c_scr.shape, F32)
        n_scr[...] = jnp.zeros(n_scr.shape, F32)
        m_scr[...] = jnp.full(m_scr.shape, STAB_INIT, F32)

    gts = gates_ref[...] + gbias_ref[...]
    lf = _log_sigmoid(gts)
    row = lax.broadcasted_iota(jnp.int32, (ln, ln), 0)
    col = lax.broadcasted_iota(jnp.int32, (ln, ln), 1)
    causal = row >= col
    tri = causal.astype(F32)
    bcum = jnp.dot(tri, lf, preferred_element_type=F32, precision=lax.Precision.HIGHEST)
    bcum_t = bcum.T
    gts_t = gts.T

    def short_conv(raw_ref, hist_off, c0):
        raw = raw_ref[:, c0:c0 + dh].astype(F32)
        prev = hist_scr[:, hist_off + c0:hist_off + c0 + dh]
        ext = jnp.concatenate([prev, raw], axis=0)
        n = hist + ln
        acc = jnp.zeros((ln, dh), F32) + cb_ref[:, hist_off + c0:hist_off + c0 + dh]
        for j in range(qk_width):
            s = hist - (qk_width - 1) + j
            w = cw_ref[j:j + 1, hist_off + c0:hist_off + c0 + dh]
            if s % SUBLANES == 0:
                acc = acc + w * ext[s:s + ln, :]
            else:
                acc = acc + w * pltpu.roll(ext, n - s, axis=0)[0:ln, :]
        return _silu(acc)

    for h in range(nh):
        c0 = h * dh
        q = short_conv(q_ref, 0, c0)
        k = short_conv(k_ref, dm, c0) * (dh ** -0.5)
        qb = q.astype(BF16)
        kb = k.astype(BF16)
        v = v_ref[:, c0:c0 + dh]
        vf = v.astype(F32)

        b_col = bcum[:, nh + h:nh + h + 1]
        b_row = bcum_t[nh + h:nh + h + 1, :]
        ig_col = gts[:, h:h + 1]
        ig_row = gts_t[h:h + 1, :]
        m_prev = m_scr[h, 0:1, 0:1]

        dmat = jnp.where(causal, b_col - b_row + ig_row, NEG_INF)
        m_inter = b_col + m_prev
        m_t = jnp.maximum(m_inter, jnp.max(dmat, axis=1, keepdims=True))
        pmat = jnp.exp(dmat - m_t)
        qk = lax.dot_general(qb, kb, (((1,), (1,)), ((), ())), preferred_element_type=F32)
        s = qk * pmat
        a = jnp.exp(m_inter - m_t)
        inter = jnp.dot(qb, c_scr[h].astype(BF16), preferred_element_type=F32)
        num = jnp.dot(s.astype(BF16), v, preferred_element_type=F32) + a * inter
        den = jnp.sum(s, axis=1, keepdims=True) + a * jnp.sum(q * n_scr[h], axis=1, keepdims=True)
        hv = num / jnp.maximum(jnp.abs(den), jnp.exp(-m_t))

        mu = jnp.mean(hv, axis=-1, keepdims=True)
        hc = hv - mu
        var = jnp.mean(hc * hc, axis=-1, keepdims=True)
        hn = hc * lax.rsqrt(var + EPS) * ng_ref[:, c0:c0 + dh]
        og = og_ref[:, c0:c0 + dh].astype(F32)
        o_ref[:, c0:c0 + dh] = (_sigmoid(og) * hn).astype(o_ref.dtype)

        b_last = b_col[ln - 1:ln, :]
        g_col = b_last - b_col + ig_col
        m_new = jnp.maximum(b_last + m_prev, jnp.max(g_col, axis=0, keepdims=True))
        wk = jnp.exp(g_col - m_new)
        decay = jnp.exp(b_last + m_prev - m_new)
        upd = lax.dot_general(kb, (vf * wk).astype(BF16), (((0,), (0,)), ((), ())),
                              preferred_element_type=F32)
        c_scr[h] = decay * c_scr[h] + upd
        n_scr[h] = decay * n_scr[h] + jnp.sum(k * wk, axis=0, keepdims=True)
        m_scr[h] = jnp.broadcast_to(m_new, m_scr.shape[1:])

    hist_scr[:, 0:dm] = q_ref[ln - hist:ln, :].astype(F32)
    hist_scr[:, dm:2 * dm] = k_ref[ln - hist:ln, :].astype(F32)


def _mlstm(proj, gates, gbias, qk_conv_w, qk_conv_b, norm_g, bsz, seq, dm, dh, col0):
    ln = MLSTM_CHUNK
    nc = seq // ln
    nh = dm // dh
    width = qk_conv_w.shape[0]
    blk = lambda off: pl.BlockSpec((ln, dm), lambda b, c: (b * nc + c, col0 + off))
    return pl.pallas_call(
        functools.partial(_mlstm_kernel, dh=dh, qk_width=width),
        grid=(bsz, nc),
        in_specs=[blk(0), blk(1), blk(2), blk(3),
                  pl.BlockSpec((ln, LANES), lambda b, c: (b * nc + c, 0)),
                  pl.BlockSpec((1, LANES), lambda b, c: (0, 0)),
                  pl.BlockSpec((width, 2 * dm), lambda b, c: (0, 0)),
                  pl.BlockSpec((1, 2 * dm), lambda b, c: (0, 0)),
                  pl.BlockSpec((1, dm), lambda b, c: (0, 0))],
        out_specs=pl.BlockSpec((ln, dm), lambda b, c: (b * nc + c, 0)),
        out_shape=jax.ShapeDtypeStruct((bsz * seq, dm), BF16),
        scratch_shapes=[pltpu.VMEM((SUBLANES, 2 * dm), F32),
                        pltpu.VMEM((nh, dh, dh), F32),
                        pltpu.VMEM((nh, 1, dh), F32),
                        pltpu.VMEM((nh, SUBLANES, LANES), F32)],
        compiler_params=_cparams(("arbitrary", "arbitrary")),
        name="mlstm",
    )(proj, proj, proj, proj, gates, gbias, qk_conv_w, qk_conv_b.reshape(1, 2 * dm),
      norm_g.reshape(1, dm))


def _route_chunk(logits_t, bias_col, n_exp):
    gsz = n_exp // N_GROUPS
    scores = _sigmoid(logits_t)
    choice = scores + bias_col
    rows_g = lax.broadcasted_iota(jnp.int32, (gsz, LANES), 0).astype(F32)
    gscore = []
    for g in range(N_GROUPS):
        v = choice[g * gsz:(g + 1) * gsz, :]
        m1 = jnp.max(v, axis=0, keepdims=True)
        i1 = jnp.min(jnp.where(v == m1, rows_g, float(gsz)), axis=0, keepdims=True)
        m2 = jnp.max(jnp.where(rows_g == i1, NEG_INF, v), axis=0, keepdims=True)
        gscore.append(m1 + m2)
    gs = jnp.concatenate(gscore, axis=0)
    giota = lax.broadcasted_iota(jnp.int32, (N_GROUPS, LANES), 0).astype(F32)
    keep = jnp.zeros((N_GROUPS, LANES), F32)
    for _ in range(TOPK_GROUPS):
        m = jnp.max(gs, axis=0, keepdims=True)
        gi = jnp.min(jnp.where(gs == m, giota, float(N_GROUPS)), axis=0, keepdims=True)
        sel = giota == gi
        keep = jnp.where(sel, 1.0, keep)
        gs = jnp.where(sel, NEG_INF, gs)
    mc = jnp.concatenate(
        [jnp.where(keep[g:g + 1, :] > 0.5, choice[g * gsz:(g + 1) * gsz, :], NEG_INF)
         for g in range(N_GROUPS)], axis=0)
    rows = lax.broadcasted_iota(jnp.int32, (n_exp, LANES), 0).astype(F32)
    idxs, ws = [], []
    for _ in range(TOP_K):
        m = jnp.max(mc, axis=0, keepdims=True)
        idx = jnp.min(jnp.where(mc == m, rows, float(n_exp)), axis=0, keepdims=True)
        sel = rows == idx
        ws.append(jnp.sum(jnp.where(sel, scores, 0.0), axis=0, keepdims=True))
        idxs.append(idx)
        mc = jnp.where(sel, NEG_INF, mc)
    w = jnp.concatenate(ws, axis=0)
    w = w / jnp.sum(w, axis=0, keepdims=True) * ROUTED_SCALE
    return jnp.concatenate(idxs, axis=0).astype(jnp.int32), w


def _outproj_kernel(u_ref, hm_ref, x_ref, gt_ref, sc_ref, sh_ref, g_ref, w_ref, wr_ref, rb_ref,
                    x1_ref, hn_ref, idx_ref, tw_ref):
    tm = x_ref.shape[0]
    dc = u_ref.shape[1]
    n_exp = wr_ref.shape[0]
    mix = (jnp.dot(u_ref[...], w_ref[0:dc, :], preferred_element_type=F32)
           + jnp.dot(hm_ref[...], w_ref[dc:, :], preferred_element_type=F32))
    x1 = x_ref[...] + gt_ref[...] * mix
    x1_ref[...] = x1
    ms = jnp.mean(x1 * x1, axis=-1, keepdims=True)
    hn = (x1 * lax.rsqrt(ms + EPS) * g_ref[...]) * (1.0 + sc_ref[...]) + sh_ref[...]
    hn_ref[...] = hn
    logits_t = lax.dot_general(wr_ref[...], hn, (((1,), (1,)), ((), ())),
                               preferred_element_type=F32, precision=lax.Precision.HIGHEST)
    for c0 in range(0, tm, LANES):
        idx, w = _route_chunk(logits_t[:, c0:c0 + LANES], rb_ref[...], n_exp)
        idx_ref[:, c0:c0 + LANES] = idx
        tw_ref[:, c0:c0 + LANES] = w


def _outproj(u, hm, x2, gt1, sc2, sh2, g_ffn, w_out, w_router_t, router_bias, seq, tm=256):
    t, d = x2.shape
    dc = u.shape[1]
    n_exp = w_router_t.shape[0]
    tpb = seq // tm
    per_b = lambda: pl.BlockSpec((None, 1, d), lambda i: (i // tpb, 0, 0))
    return pl.pallas_call(
        _outproj_kernel,
        grid=(t // tm,),
        in_specs=[pl.BlockSpec((tm, dc), lambda i: (i, 0)),
                  pl.BlockSpec((tm, d - dc), lambda i: (i, 0)),
                  pl.BlockSpec((tm, d), lambda i: (i, 0)),
                  per_b(), per_b(), per_b(),
                  pl.BlockSpec((1, d), lambda i: (0, 0)),
                  pl.BlockSpec((d, d), lambda i: (0, 0)),
                  pl.BlockSpec((n_exp, d), lambda i: (0, 0)),
                  pl.BlockSpec((n_exp, 1), lambda i: (0, 0))],
        out_specs=[pl.BlockSpec((tm, d), lambda i: (i, 0)),
                   pl.BlockSpec((tm, d), lambda i: (i, 0)),
                   pl.BlockSpec((TOP_K, tm), lambda i: (0, i)),
                   pl.BlockSpec((TOP_K, tm), lambda i: (0, i))],
        out_shape=[jax.ShapeDtypeStruct((t, d), F32),
                   jax.ShapeDtypeStruct((t, d), F32),
                   jax.ShapeDtypeStruct((TOP_K, t), jnp.int32),
                   jax.ShapeDtypeStruct((TOP_K, t), F32)],
        compiler_params=_cparams(("arbitrary",)),
        name="outproj_route",
    )(u, hm, x2, gt1, sc2, sh2, g_ffn, w_out, w_router_t, router_bias.reshape(n_exp, 1))


def _dispatch_plan(top_idx_t, top_w_t, n_exp, mb):
    k, t = top_idx_t.shape
    n_assign = k * t
    flat_e = top_idx_t.reshape(-1)
    order = jnp.argsort(flat_e).astype(jnp.int32)
    se = flat_e[order]
    sizes = jnp.bincount(flat_e, length=n_exp).astype(jnp.int32)
    starts = jnp.cumsum(sizes) - sizes
    padded = (sizes + mb - 1) // mb * mb
    pad_ends = jnp.cumsum(padded)
    dest = pad_ends[se] - padded[se] + jnp.arange(n_assign, dtype=jnp.int32) - starts[se]
    n_blocks = -(-(n_assign + n_exp * (mb - 1)) // mb)
    n_slots = n_blocks * mb
    spill = n_assign + jnp.tile(jnp.arange(mb, dtype=jnp.int32), n_blocks)
    slot_tok = jnp.zeros((n_slots,), jnp.int32).at[dest].set(order % t)
    slot_dst = spill.at[dest].set(order)
    slot_w = jnp.zeros((n_slots,), F32).at[dest].set(top_w_t.reshape(-1)[order])
    n_used = (pad_ends[-1] // mb).astype(jnp.int32)
    blk_start = jnp.arange(n_blocks, dtype=jnp.int32) * mb
    block_e = jnp.minimum(jnp.searchsorted(pad_ends, blk_start, side='right'), n_exp - 1)
    block_e = jnp.where(jnp.arange(n_blocks) < n_used, block_e, block_e[n_used - 1]).astype(jnp.int32)
    idx = jnp.stack([slot_tok.reshape(n_blocks, mb), slot_dst.reshape(n_blocks, mb)], axis=1)
    return idx, slot_w.reshape(n_blocks, mb, 1), block_e, n_used.reshape(1)


def _moe_kernel(be_ref, nu_ref, idx_hbm, hn_hbm, sw_ref, wgu_ref, wd_ref, yk_hbm,
                idx_s, xbuf, ybuf, wgu_b, wd_b, sem_i, sem_g, sem_s):
    b = pl.program_id(0)
    n_used = nu_ref[0]
    mb = xbuf.shape[1]
    de = wd_ref.shape[0]

    def idx_copy(j):
        return pltpu.make_async_copy(idx_hbm.at[j], idx_s.at[j % 3], sem_i.at[j % 3])

    def gather_start(j):
        islot = j % 3
        xslot = j % 2
        for m in range(mb):
            tok = idx_s[islot, 0, m]
            pltpu.make_async_copy(hn_hbm.at[pl.ds(tok, 1), :],
                                  xbuf.at[xslot, pl.ds(m, 1), :], sem_g.at[xslot]).start()

    def gather_wait(j):
        xslot = j % 2
        pltpu.make_async_copy(hn_hbm.at[pl.ds(0, mb), :], xbuf.at[xslot], sem_g.at[xslot]).wait()

    def scatter_start(j):
        islot = j % 3
        for m in range(mb):
            dst = idx_s[islot, 1, m]
            pltpu.make_async_copy(ybuf.at[pl.ds(m, 1), :],
                                  yk_hbm.at[pl.ds(dst, 1), :], sem_s.at[0]).start()

    def scatter_wait():
        pltpu.make_async_copy(ybuf, yk_hbm.at[pl.ds(0, mb), :], sem_s.at[0]).wait()

    @pl.when(b == 0)
    def _():
        idx_copy(0).start()
        idx_copy(0).wait()
        gather_start(0)

        @pl.when(n_used > 1)
        def _():
            idx_copy(1).start()

    @pl.when(b < n_used)
    def _():
        @pl.when(b + 2 < n_used)
        def _():
            idx_copy(b + 2).start()

        @pl.when(b + 1 < n_used)
        def _():
            idx_copy(b + 1).wait()
            gather_start(b + 1)

        first = jnp.logical_or(b == 0, be_ref[b] != be_ref[jnp.maximum(b - 1, 0)])

        @pl.when(first)
        def _():
            wgu_b[...] = wgu_ref[...].astype(BF16)
            wd_b[...] = wd_ref[...].astype(BF16)

        gather_wait(b)
        x = xbuf[b % 2].astype(BF16)
        gu = jnp.dot(x, wgu_b[...], preferred_element_type=F32)
        act = _silu(gu[:, :de]) * gu[:, de:] * sw_ref[...]
        y = jnp.dot(act.astype(BF16), wd_b[...], preferred_element_type=F32)

        @pl.when(b > 0)
        def _():
            scatter_wait()

        ybuf[...] = y
        scatter_start(b)

        @pl.when(b == n_used - 1)
        def _():
            scatter_wait()
            tail = pltpu.make_async_copy(
                ybuf, yk_hbm.at[pl.ds(yk_hbm.shape[0] - mb, mb), :], sem_s.at[0])
            tail.start()
            tail.wait()


def _moe(hn, idx, slot_w, block_e, n_used, w_gate_up, w_down, n_rows_out):
    t, d = hn.shape
    n_blocks, _, mb = idx.shape
    n_exp, _, de2 = w_gate_up.shape
    de = w_down.shape[1]
    grid_spec = pltpu.PrefetchScalarGridSpec(
        num_scalar_prefetch=2,
        grid=(n_blocks,),
        in_specs=[pl.BlockSpec(memory_space=pl.ANY),
                  pl.BlockSpec(memory_space=pl.ANY),
                  pl.BlockSpec((None, mb, 1), lambda b, be, nu: (b, 0, 0)),
                  pl.BlockSpec((None, d, de2), lambda b, be, nu: (be[b], 0, 0)),
                  pl.BlockSpec((None, de, d), lambda b, be, nu: (be[b], 0, 0))],
        out_specs=pl.BlockSpec(memory_space=pl.ANY),
        scratch_shapes=[pltpu.SMEM((3, 2, mb), jnp.int32),
                        pltpu.VMEM((2, mb, d), F32),
                        pltpu.VMEM((mb, d), F32),
                        pltpu.VMEM((d, de2), BF16),
                        pltpu.VMEM((de, d), BF16),
                        pltpu.SemaphoreType.DMA((3,)),
                        pltpu.SemaphoreType.DMA((2,)),
                        pltpu.SemaphoreType.DMA((1,))])
    return pl.pallas_call(
        _moe_kernel,
        grid_spec=grid_spec,
        out_shape=jax.ShapeDtypeStruct((n_rows_out, d), F32),
        compiler_params=_cparams(("arbitrary",)),
        name="moe",
    )(block_e, n_used, idx, hn, slot_w, w_gate_up, w_down)


def _combine_kernel(*refs):
    yk_refs = refs[:TOP_K]
    x1_ref, hn_ref, gt_ref, gf_ref, wgu_ref, wd_ref, o_ref = refs[TOP_K:]
    de = wd_ref.shape[0]
    y = yk_refs[0][...]
    for r in yk_refs[1:]:
        y = y + r[...]
    gu = jnp.dot(hn_ref[...].astype(BF16), wgu_ref[...], preferred_element_type=F32)
    act = _silu(gu[:, :de]) * gu[:, de:]
    y = y + jnp.dot(act.astype(BF16), wd_ref[...], preferred_element_type=F32)
    x2 = x1_ref[...] + gt_ref[...] * y
    ms = jnp.mean(x2 * x2, axis=-1, keepdims=True)
    o_ref[...] = x2 * lax.rsqrt(ms + EPS) * gf_ref[...]


def _combine(yk, x1, hn, gt2, g_final, ws_gu, ws_d, seq, tm=128):
    t, d = x1.shape
    tpb = seq // tm
    nt = t // tm
    de2 = ws_gu.shape[1]
    yk_specs = [pl.BlockSpec((tm, d), lambda i, k=k: (k * nt + i, 0)) for k in range(TOP_K)]
    return pl.pallas_call(
        _combine_kernel,
        grid=(nt,),
        in_specs=yk_specs + [
            pl.BlockSpec((tm, d), lambda i: (i, 0)),
            pl.BlockSpec((tm, d), lambda i: (i, 0)),
            pl.BlockSpec((None, 1, d), lambda i: (i // tpb, 0, 0)),
            pl.BlockSpec((1, d), lambda i: (0, 0)),
            pl.BlockSpec((d, de2), lambda i: (0, 0)),
            pl.BlockSpec((de2 // 2, d), lambda i: (0, 0))],
        out_specs=pl.BlockSpec((tm, d), lambda i: (i, 0)),
        out_shape=jax.ShapeDtypeStruct((t, d), F32),
        compiler_params=_cparams(("arbitrary",)),
        name="combine",
    )(*([yk] * TOP_K), x1, hn, gt2, g_final.reshape(1, d), ws_gu, ws_d)


def kernel(x, c, w_ada, b_ada, g_mix, w_in, conv_w, conv_b, conv_ln_g, conv_ln_b, qk_conv_w, qk_conv_b, b_igate, b_fgate, mlstm_norm_g, w_out, g_ffn, w_router, router_bias, w_gate_up, w_down, ws_gate_up, ws_down, g_final):
    bsz, seq, d = x.shape
    depth = w_ada.shape[0]
    assert depth == 1, "single-layer block"
    dc = conv_w.shape[2]
    dm = d - dc
    nh = b_igate.shape[1]
    dh = dm // nh
    n_exp = w_router.shape[2]
    t = bsz * seq
    n_main = 2 * dc + 4 * dm
    assert dc == dm and nh == N_HEADS and 2 * nh <= LANES

    mod = _ada(c, w_ada[0], b_ada[0])
    sh1, sc1, gt1, sh2, sc2, gt2 = [m.reshape(bsz, 1, d) for m in jnp.split(mod, 6, axis=-1)]

    x2 = x.reshape(t, d)
    w_main = w_in[0, :, :n_main].astype(BF16)
    w_gate = jnp.zeros((d, LANES), BF16).at[:, :2 * nh].set(w_in[0, :, n_main:].astype(BF16))
    proj, gates = _inproj(x2, sc1, sh1, g_mix[0].reshape(1, d), w_main, w_gate, seq)

    u = _convmod(proj, conv_w[0], conv_b[0], conv_ln_g[0], conv_ln_b[0], bsz, seq)

    gbias = jnp.zeros((1, LANES), F32).at[0, :nh].set(b_igate[0]).at[0, nh:2 * nh].set(b_fgate[0])
    hm = _mlstm(proj, gates, gbias, qk_conv_w[0], qk_conv_b[0], mlstm_norm_g[0],
                bsz, seq, dm, dh, col0=2 * dc // dm)

    x1, hn, top_idx_t, top_w_t = _outproj(
        u, hm, x2, gt1, sc2, sh2, g_ffn[0].reshape(1, d), w_out[0].astype(BF16),
        w_router[0].T, router_bias[0], seq)

    idx, slot_w, block_e, n_used = _dispatch_plan(top_idx_t, top_w_t, n_exp, MOE_BLOCK)
    yk = _moe(hn, idx, slot_w, block_e, n_used, w_gate_up[0], w_down[0], TOP_K * t + MOE_BLOCK)

    out = _combine(yk, x1, hn, gt2, g_final, ws_gate_up[0].astype(BF16), ws_down[0].astype(BF16), seq)
    return out.reshape(bsz, seq, d)
```

```python
import functools

import jax
import jax.numpy as jnp
from jax import lax
from jax.experimental import pallas as pl
from jax.experimental.pallas import tpu as pltpu

F32 = jnp.float32
BF16 = jnp.bfloat16
U32 = jnp.uint32
I32 = jnp.int32

EPS = 1e-6
STAB_INIT = -1e30
TOP_K = 8
N_GROUPS = 8
TOPK_GROUPS = 4
ROUTED_SCALE = 2.5
N_HEADS = 4

LANES = 128
SUBLANES = 8
VMEM_LIMIT = 56 * 1024 * 1024

MLSTM_CHUNK = 256
MOE_BLOCK = 128
NEG_INF = float("-inf")
HIGHEST = lax.Precision.HIGHEST


def _cparams(sem):
    return pltpu.CompilerParams(dimension_semantics=sem, vmem_limit_bytes=VMEM_LIMIT)


def _sigmoid(x):
    return 1.0 / (1.0 + jnp.exp(-x))


def _silu(x):
    return x * _sigmoid(x)


def _pack_halves(lo, hi):
    lo_b = lax.bitcast_convert_type(lo.astype(BF16).astype(F32), U32)
    hi_b = lax.bitcast_convert_type(hi.astype(BF16).astype(F32), U32)
    return (hi_b & jnp.uint32(0xFFFF0000)) | (lo_b >> 16)


def _unpack_halves(w):
    lo = lax.bitcast_convert_type(w << 16, F32)
    hi = lax.bitcast_convert_type(w & jnp.uint32(0xFFFF0000), F32)
    return lo, hi


def _ada_kernel(c_ref, w_ref, b_ref, o_ref):
    cs = _silu(c_ref[...])
    o_ref[...] = jnp.dot(cs, w_ref[...], preferred_element_type=F32, precision=HIGHEST) + b_ref[...]


def _ada(c, w_ada, b_ada):
    bsz, d = c.shape
    n = w_ada.shape[1]
    tn = 1536
    cp = jnp.zeros((SUBLANES, d), F32).at[:bsz].set(c)
    out = pl.pallas_call(
        _ada_kernel,
        grid=(n // tn,),
        in_specs=[pl.BlockSpec((SUBLANES, d), lambda j: (0, 0)),
                  pl.BlockSpec((d, tn), lambda j: (0, j)),
                  pl.BlockSpec((1, tn), lambda j: (0, j))],
        out_specs=pl.BlockSpec((SUBLANES, tn), lambda j: (0, j)),
        out_shape=jax.ShapeDtypeStruct((SUBLANES, n), F32),
        compiler_params=_cparams(("arbitrary",)),
        name="ada",
    )(cp, w_ada, b_ada.reshape(1, n))
    return out[:bsz]


def _inproj_kernel(x_ref, sc_ref, sh_ref, g_ref, w_ref, wg_ref, o_ref, og_ref, h_scr):
    @pl.when(pl.program_id(1) == 0)
    def _():
        x = x_ref[...]
        ms = jnp.mean(x * x, axis=-1, keepdims=True)
        h = (x * lax.rsqrt(ms + EPS) * g_ref[...]) * (1.0 + sc_ref[...]) + sh_ref[...]
        hb = h.astype(BF16)
        h_scr[...] = hb
        og_ref[...] = jnp.dot(hb, wg_ref[...], preferred_element_type=F32)

    o_ref[...] = jnp.dot(h_scr[...], w_ref[...], preferred_element_type=F32).astype(o_ref.dtype)


def _inproj(x2, sc, sh, g, w_main, w_gate, seq, tm=512, tn=1536):
    t, d = x2.shape
    n = w_main.shape[1]
    tpb = seq // tm
    return pl.pallas_call(
        _inproj_kernel,
        grid=(t // tm, n // tn),
        in_specs=[pl.BlockSpec((tm, d), lambda i, j: (i, 0)),
                  pl.BlockSpec((None, 1, d), lambda i, j: (i // tpb, 0, 0)),
                  pl.BlockSpec((None, 1, d), lambda i, j: (i // tpb, 0, 0)),
                  pl.BlockSpec((1, d), lambda i, j: (0, 0)),
                  pl.BlockSpec((d, tn), lambda i, j: (0, j)),
                  pl.BlockSpec((d, LANES), lambda i, j: (0, 0))],
        out_specs=[pl.BlockSpec((tm, tn), lambda i, j: (i, j)),
                   pl.BlockSpec((tm, LANES), lambda i, j: (i, 0))],
        out_shape=[jax.ShapeDtypeStruct((t, n), BF16),
                   jax.ShapeDtypeStruct((t, LANES), F32)],
        scratch_shapes=[pltpu.VMEM((tm, d), BF16)],
        compiler_params=_cparams(("arbitrary", "arbitrary")),
        name="inproj",
    )(x2, sc, sh, g, w_main, w_gate)


_CONV_RC = 128
_CONV_HIST = 32


def _convmod_kernel(a_ref, gate_ref, w_ref, b_ref, lg_ref, lb_ref, o_ref, ext_scr, y_scr, *, width):
    tt, ch = a_ref.shape
    hist = _CONV_HIST
    rc = _CONV_RC

    @pl.when(pl.program_id(1) == 0)
    def _():
        ext_scr[0:hist, :] = jnp.zeros((hist, ch), F32)

    @pl.when(pl.program_id(1) != 0)
    def _():
        ext_scr[0:hist, :] = ext_scr[tt:tt + hist, :]

    ext_scr[hist:hist + tt, :] = a_ref[...].astype(F32) * _sigmoid(gate_ref[...].astype(F32))

    off0 = hist - (width - 1)
    nrow = rc + hist
    for r0 in range(0, tt, rc):
        for c0 in range(0, ch, LANES):
            blk = ext_scr[r0:r0 + nrow, c0:c0 + LANES]
            acc = jnp.zeros((rc, LANES), F32)
            for r in range(SUBLANES):
                taps = [j for j in range(width) if (off0 + j) % SUBLANES == r]
                if not taps:
                    continue
                rolled = blk if r == 0 else pltpu.roll(blk, nrow - r, axis=0)
                for j in taps:
                    a0 = off0 + j - r
                    acc = acc + w_ref[j:j + 1, c0:c0 + LANES] * rolled[a0:a0 + rc, :]
            y_scr[r0:r0 + rc, c0:c0 + LANES] = acc

    y = y_scr[...] + b_ref[...]
    mu = jnp.mean(y, axis=-1, keepdims=True)
    yc = y - mu
    var = jnp.mean(yc * yc, axis=-1, keepdims=True)
    z = yc * lax.rsqrt(var + EPS) * lg_ref[...] + lb_ref[...]
    o_ref[...] = _silu(z).astype(o_ref.dtype)


def _convmod(proj, conv_w, conv_b, ln_g, ln_b, bsz, seq, tt=512):
    width, ch = conv_w.shape
    assert width - 1 <= _CONV_HIST and tt % _CONV_RC == 0
    tpb = seq // tt
    vec = lambda: pl.BlockSpec((1, ch), lambda b, i: (0, 0))
    return pl.pallas_call(
        functools.partial(_convmod_kernel, width=width),
        grid=(bsz, tpb),
        in_specs=[pl.BlockSpec((tt, ch), lambda b, i: (b * tpb + i, 0)),
                  pl.BlockSpec((tt, ch), lambda b, i: (b * tpb + i, 1)),
                  pl.BlockSpec((width, ch), lambda b, i: (0, 0)),
                  vec(), vec(), vec()],
        out_specs=pl.BlockSpec((tt, ch), lambda b, i: (b * tpb + i, 0)),
        out_shape=jax.ShapeDtypeStruct((bsz * seq, ch), BF16),
        scratch_shapes=[pltpu.VMEM((tt + _CONV_HIST, ch), F32),
                        pltpu.VMEM((tt, ch), F32)],
        compiler_params=_cparams(("arbitrary", "arbitrary")),
        name="convmod",
    )(proj, proj, conv_w, conv_b.reshape(1, ch), ln_g.reshape(1, ch), ln_b.reshape(1, ch))


def _log_sigmoid(x):
    return -(jnp.maximum(-x, 0.0) + jnp.log(1.0 + jnp.exp(-jnp.abs(x))))


def _mlstm_kernel(q_ref, k_ref, v_ref, og_ref, gates_ref, gbias_ref, cw_ref, cb_ref, ng_ref,
                  o_ref, hist_scr, c_scr, n_scr, m_scr, *, dh, qk_width):
    ln = q_ref.shape[0]
    dm = q_ref.shape[1]
    nh = dm // dh
    hist = SUBLANES

    @pl.when(pl.program_id(1) == 0)
    def _():
        hist_scr[...] = jnp.zeros(hist_scr.shape, F32)
        c_scr[...] = jnp.zeros(c_scr.shape, F32)
        n_scr[...] = jnp.zeros(n_scr.shape, F32)
        m_scr[...] = jnp.full(m_scr.shape, STAB_INIT, F32)

    gts = gates_ref[...] + gbias_ref[...]
    lf = _log_sigmoid(gts)
    row = lax.broadcasted_iota(I32, (ln, ln), 0)
    col = lax.broadcasted_iota(I32, (ln, ln), 1)
    causal = row >= col
    tri = causal.astype(F32)
    bcum = jnp.dot(tri, lf, preferred_element_type=F32, precision=HIGHEST)
    bcum_t = bcum.T
    gts_t = gts.T

    def short_conv(raw_ref, hist_off, c0):
        raw = raw_ref[:, c0:c0 + dh].astype(F32)
        prev = hist_scr[:, hist_off + c0:hist_off + c0 + dh]
        ext = jnp.concatenate([prev, raw], axis=0)
        n = hist + ln
        acc = jnp.zeros((ln, dh), F32) + cb_ref[:, hist_off + c0:hist_off + c0 + dh]
        for j in range(qk_width):
            s = hist - (qk_width - 1) + j
            w = cw_ref[j:j + 1, hist_off + c0:hist_off + c0 + dh]
            if s % SUBLANES == 0:
                acc = acc + w * ext[s:s + ln, :]
            else:
                acc = acc + w * pltpu.roll(ext, n - s, axis=0)[0:ln, :]
        return _silu(acc)

    for h in range(nh):
        c0 = h * dh
        q = short_conv(q_ref, 0, c0)
        k = short_conv(k_ref, dm, c0) * (dh ** -0.5)
        qb = q.astype(BF16)
        kb = k.astype(BF16)
        v = v_ref[:, c0:c0 + dh]
        vf = v.astype(F32)

        b_col = bcum[:, nh + h:nh + h + 1]
        b_row = bcum_t[nh + h:nh + h + 1, :]
        ig_col = gts[:, h:h + 1]
        ig_row = gts_t[h:h + 1, :]
        m_prev = m_scr[h, 0:1, 0:1]

        dmat = jnp.where(causal, b_col - b_row + ig_row, NEG_INF)
        m_inter = b_col + m_prev
        m_t = jnp.maximum(m_inter, jnp.max(dmat, axis=1, keepdims=True))
        pmat = jnp.exp(dmat - m_t)
        qk = lax.dot_general(qb, kb, (((1,), (1,)), ((), ())), preferred_element_type=F32)
        s = qk * pmat
        a = jnp.exp(m_inter - m_t)
        inter = jnp.dot(qb, c_scr[h].astype(BF16), preferred_element_type=F32)
        num = jnp.dot(s.astype(BF16), v, preferred_element_type=F32) + a * inter
        den = jnp.sum(s, axis=1, keepdims=True) + a * jnp.sum(q * n_scr[h], axis=1, keepdims=True)
        hv = num / jnp.maximum(jnp.abs(den), jnp.exp(-m_t))

        mu = jnp.mean(hv, axis=-1, keepdims=True)
        hc = hv - mu
        var = jnp.mean(hc * hc, axis=-1, keepdims=True)
        hn = hc * lax.rsqrt(var + EPS) * ng_ref[:, c0:c0 + dh]
        og = og_ref[:, c0:c0 + dh].astype(F32)
        o_ref[:, c0:c0 + dh] = (_sigmoid(og) * hn).astype(o_ref.dtype)

        b_last = b_col[ln - 1:ln, :]
        g_col = b_last - b_col + ig_col
        m_new = jnp.maximum(b_last + m_prev, jnp.max(g_col, axis=0, keepdims=True))
        wk = jnp.exp(g_col - m_new)
        decay = jnp.exp(b_last + m_prev - m_new)
        upd = lax.dot_general(kb, (vf * wk).astype(BF16), (((0,), (0,)), ((), ())),
                              preferred_element_type=F32)
        c_scr[h] = decay * c_scr[h] + upd
        n_scr[h] = decay * n_scr[h] + jnp.sum(k * wk, axis=0, keepdims=True)
        m_scr[h] = jnp.broadcast_to(m_new, m_scr.shape[1:])

    hist_scr[:, 0:dm] = q_ref[ln - hist:ln, :].astype(F32)
    hist_scr[:, dm:2 * dm] = k_ref[ln - hist:ln, :].astype(F32)


def _mlstm(proj, gates, gbias, qk_conv_w, qk_conv_b, norm_g, bsz, seq, dm, dh, col0):
    ln = MLSTM_CHUNK
    nc = seq // ln
    nh = dm // dh
    width = qk_conv_w.shape[0]
    blk = lambda off: pl.BlockSpec((ln, dm), lambda b, c: (b * nc + c, col0 + off))
    return pl.pallas_call(
        functools.partial(_mlstm_kernel, dh=dh, qk_width=width),
        grid=(bsz, nc),
        in_specs=[blk(0), blk(1), blk(2), blk(3),
                  pl.BlockSpec((ln, LANES), lambda b, c: (b * nc + c, 0)),
                  pl.BlockSpec((1, LANES), lambda b, c: (0, 0)),
                  pl.BlockSpec((width, 2 * dm), lambda b, c: (0, 0)),
                  pl.BlockSpec((1, 2 * dm), lambda b, c: (0, 0)),
                  pl.BlockSpec((1, dm), lambda b, c: (0, 0))],
        out_specs=pl.BlockSpec((ln, dm), lambda b, c: (b * nc + c, 0)),
        out_shape=jax.ShapeDtypeStruct((bsz * seq, dm), BF16),
        scratch_shapes=[pltpu.VMEM((SUBLANES, 2 * dm), F32),
                        pltpu.VMEM((nh, dh, dh), F32),
                        pltpu.VMEM((nh, 1, dh), F32),
                        pltpu.VMEM((nh, SUBLANES, LANES), F32)],
        compiler_params=_cparams(("arbitrary", "arbitrary")),
        name="mlstm",
    )(proj, proj, proj, proj, gates, gbias, qk_conv_w, qk_conv_b.reshape(1, 2 * dm),
      norm_g.reshape(1, dm))


def _route_chunk(logits_t, bias_col, n_exp):
    gsz = n_exp // N_GROUPS
    scores = _sigmoid(logits_t)
    choice = scores + bias_col
    rows_g = lax.broadcasted_iota(I32, (gsz, LANES), 0).astype(F32)
    gscore = []
    for g in range(N_GROUPS):
        v = choice[g * gsz:(g + 1) * gsz, :]
        m1 = jnp.max(v, axis=0, keepdims=True)
        i1 = jnp.min(jnp.where(v == m1, rows_g, float(gsz)), axis=0, keepdims=True)
        m2 = jnp.max(jnp.where(rows_g == i1, NEG_INF, v), axis=0, keepdims=True)
        gscore.append(m1 + m2)
    gs = jnp.concatenate(gscore, axis=0)
    giota = lax.broadcasted_iota(I32, (N_GROUPS, LANES), 0).astype(F32)
    keep = jnp.zeros((N_GROUPS, LANES), F32)
    for _ in range(TOPK_GROUPS):
        m = jnp.max(gs, axis=0, keepdims=True)
        gi = jnp.min(jnp.where(gs == m, giota, float(N_GROUPS)), axis=0, keepdims=True)
        sel = giota == gi
        keep = jnp.where(sel, 1.0, keep)
        gs = jnp.where(sel, NEG_INF, gs)
    mc = jnp.concatenate(
        [jnp.where(keep[g:g + 1, :] > 0.5, choice[g * gsz:(g + 1) * gsz, :], NEG_INF)
         for g in range(N_GROUPS)], axis=0)
    rows = lax.broadcasted_iota(I32, (n_exp, LANES), 0).astype(F32)
    idxs, ws = [], []
    picked = jnp.zeros((n_exp, LANES), F32)
    for _ in range(TOP_K):
        m = jnp.max(mc, axis=0, keepdims=True)
        idx = jnp.min(jnp.where(mc == m, rows, float(n_exp)), axis=0, keepdims=True)
        sel = rows == idx
        ws.append(jnp.sum(jnp.where(sel, scores, 0.0), axis=0, keepdims=True))
        idxs.append(idx)
        picked = jnp.where(sel, 1.0, picked)
        mc = jnp.where(sel, NEG_INF, mc)
    w = jnp.concatenate(ws, axis=0)
    w = w / jnp.sum(w, axis=0, keepdims=True) * ROUTED_SCALE
    return jnp.concatenate(idxs, axis=0).astype(I32), w, picked


def _outproj_kernel(u_ref, hm_ref, x_ref, gt_ref, sc_ref, sh_ref, g_ref, w_ref, wr_ref, rb_ref,
                    x1_ref, hnp_ref, idx_ref, tw_ref, cnt_ref):
    tm, d = x_ref.shape
    dc = u_ref.shape[1]
    half = d // 2
    n_exp = wr_ref.shape[0]

    @pl.when(pl.program_id(0) == 0)
    def _():
        cnt_ref[...] = jnp.zeros(cnt_ref.shape, F32)

    mix = (jnp.dot(u_ref[...], w_ref[0:dc, :], preferred_element_type=F32)
           + jnp.dot(hm_ref[...], w_ref[dc:, :], preferred_element_type=F32))
    x1 = x_ref[...] + gt_ref[...] * mix
    x1_ref[...] = x1
    ms = jnp.mean(x1 * x1, axis=-1, keepdims=True)
    hn = (x1 * lax.rsqrt(ms + EPS) * g_ref[...]) * (1.0 + sc_ref[...]) + sh_ref[...]
    hnp_ref[...] = _pack_halves(hn[:, :half], hn[:, half:])
    logits_t = lax.dot_general(wr_ref[...], hn, (((1,), (1,)), ((), ())),
                               preferred_element_type=F32, precision=HIGHEST)
    cnt = cnt_ref[...]
    for c0 in range(0, tm, LANES):
        idx, w, picked = _route_chunk(logits_t[:, c0:c0 + LANES], rb_ref[...], n_exp)
        idx_ref[:, c0:c0 + LANES] = idx
        tw_ref[:, c0:c0 + LANES] = w
        cnt = cnt + picked
    cnt_ref[...] = cnt


def _outproj(u, hm, x2, gt1, sc2, sh2, g_ffn, w_out, w_router_t, router_bias, seq, tm=256):
    t, d = x2.shape
    dc = u.shape[1]
    n_exp = w_router_t.shape[0]
    tpb = seq // tm
    per_b = lambda: pl.BlockSpec((None, 1, d), lambda i: (i // tpb, 0, 0))
    return pl.pallas_call(
        _outproj_kernel,
        grid=(t // tm,),
        in_specs=[pl.BlockSpec((tm, dc), lambda i: (i, 0)),
                  pl.BlockSpec((tm, d - dc), lambda i: (i, 0)),
                  pl.BlockSpec((tm, d), lambda i: (i, 0)),
                  per_b(), per_b(), per_b(),
                  pl.BlockSpec((1, d), lambda i: (0, 0)),
                  pl.BlockSpec((d, d), lambda i: (0, 0)),
                  pl.BlockSpec((n_exp, d), lambda i: (0, 0)),
                  pl.BlockSpec((n_exp, 1), lambda i: (0, 0))],
        out_specs=[pl.BlockSpec((tm, d), lambda i: (i, 0)),
                   pl.BlockSpec((tm, d // 2), lambda i: (i, 0)),
                   pl.BlockSpec((TOP_K, tm), lambda i: (0, i)),
                   pl.BlockSpec((TOP_K, tm), lambda i: (0, i)),
                   pl.BlockSpec((n_exp, LANES), lambda i: (0, 0))],
        out_shape=[jax.ShapeDtypeStruct((t, d), F32),
                   jax.ShapeDtypeStruct((t, d // 2), U32),
                   jax.ShapeDtypeStruct((TOP_K, t), I32),
                   jax.ShapeDtypeStruct((TOP_K, t), F32),
                   jax.ShapeDtypeStruct((n_exp, LANES), F32)],
        compiler_params=_cparams(("arbitrary",)),
        name="outproj_route",
    )(u, hm, x2, gt1, sc2, sh2, g_ffn, w_out, w_router_t, router_bias.reshape(n_exp, 1))


def _plan_kernel(idx_ref, start_ref, dest_ref, run_scr):
    n_exp = start_ref.shape[0]
    tp = idx_ref.shape[1]

    @pl.when(pl.program_id(0) == 0)
    def _():
        run_scr[...] = start_ref[...]

    rows = lax.broadcasted_iota(I32, (n_exp, tp), 0)
    earlier = (lax.broadcasted_iota(I32, (tp, tp), 0)
               < lax.broadcasted_iota(I32, (tp, tp), 1)).astype(BF16)
    base = run_scr[...]
    for k in range(TOP_K):
        oh = rows == idx_ref[k:k + 1, :]
        ohf = oh.astype(F32)
        pref = jnp.dot(ohf.astype(BF16), earlier, preferred_element_type=F32)
        dest = jnp.sum(jnp.where(oh, pref + base, 0.0), axis=0, keepdims=True)
        dest_ref[k:k + 1, :] = dest.astype(I32)
        base = base + jnp.sum(ohf, axis=1, keepdims=True)
    run_scr[...] = base


def _plan(top_idx_t, starts_col, tp=512):
    k, t = top_idx_t.shape
    n_exp = starts_col.shape[0]
    return pl.pallas_call(
        _plan_kernel,
        grid=(t // tp,),
        in_specs=[pl.BlockSpec((k, tp), lambda i: (0, i)),
                  pl.BlockSpec((n_exp, 1), lambda i: (0, 0))],
        out_specs=pl.BlockSpec((k, tp), lambda i: (0, i)),
        out_shape=jax.ShapeDtypeStruct((k, t), I32),
        scratch_shapes=[pltpu.VMEM((n_exp, 1), F32)],
        compiler_params=_cparams(("arbitrary",)),
        name="plan",
    )(top_idx_t, starts_col)


def _fill_tail(src_ref, dst_hbm, first_row, max_tail, sem):
    nz = src_ref.shape[0]
    n_rows = dst_hbm.shape[0]
    n_win = -(-max_tail // nz)

    def window(j):
        row = first_row + j * nz
        fits = row + nz <= n_rows
        row = pl.multiple_of(jnp.minimum(row, n_rows - nz), SUBLANES)
        return fits, pltpu.make_async_copy(src_ref, dst_hbm.at[pl.ds(row, nz), :], sem)

    for j in range(n_win):
        fits, cp = window(j)
        pl.when(fits)(cp.start)
    for j in range(n_win):
        fits, cp = window(j)
        pl.when(fits)(cp.wait)
    last = pltpu.make_async_copy(src_ref, dst_hbm.at[pl.ds(n_rows - nz, nz), :], sem)
    last.start()
    last.wait()


def _dispatch_kernel(padlo_ref, padn_ref, meta_ref, dest_hbm, hnp_ref, xs_hbm,
                     idx_s, zero_scr, sem_i, sem_r):
    i = pl.program_id(0)
    nt = pl.num_programs(0)
    td = hnp_ref.shape[0]
    n_pick = idx_s.shape[1]

    def idx_copy(j):
        return pltpu.make_async_copy(dest_hbm.at[:, pl.ds(j * td, td)], idx_s.at[j % 2], sem_i.at[j % 2])

    @pl.when(i == 0)
    def _():
        idx_copy(0).start()

    @pl.when(i + 1 < nt)
    def _():
        idx_copy(i + 1).start()

    idx_copy(i).wait()
    slot = i % 2

    def body(t, carry):
        for k in range(n_pick):
            dst = idx_s[slot, k, t]
            pltpu.make_async_copy(hnp_ref.at[pl.ds(t, 1), :], xs_hbm.at[pl.ds(dst, 1), :],
                                  sem_r.at[0]).start()
        return carry

    lax.fori_loop(0, td, body, 0)

    for _ in range(n_pick):
        pltpu.make_async_copy(hnp_ref, xs_hbm.at[pl.ds(0, td), :], sem_r.at[0]).wait()

    @pl.when(i == nt - 1)
    def _():
        zero_scr[...] = jnp.zeros(zero_scr.shape, zero_scr.dtype)
        nz = zero_scr.shape[0]
        n_exp = padlo_ref.shape[0]
        zrow = zero_scr.at[pl.ds(0, 1), :]

        def pad_body(e, carry):
            lo = padlo_ref[e]
            for r in range(SUBLANES - 1):
                @pl.when(r < padn_ref[e])
                def _():
                    pltpu.make_async_copy(zrow, xs_hbm.at[pl.ds(lo + r, 1), :], sem_r.at[0]).start()
            return carry

        lax.fori_loop(0, n_exp, pad_body, 0)

        def pad_wait(j, carry):
            pltpu.make_async_copy(zrow, xs_hbm.at[pl.ds(0, 1), :], sem_r.at[0]).wait()
            return carry

        lax.fori_loop(0, meta_ref[0], pad_wait, 0)
        _fill_tail(zero_scr, xs_hbm, meta_ref[1], SUBLANES * n_exp + nz, sem_r.at[0])


def _dispatch(pad_lo, pad_n, meta, dest, hnp, n_rows, tail_rows, td=256):
    t, half = hnp.shape
    k = dest.shape[0]
    grid_spec = pltpu.PrefetchScalarGridSpec(
        num_scalar_prefetch=3,
        grid=(t // td,),
        in_specs=[pl.BlockSpec(memory_space=pl.ANY),
                  pl.BlockSpec((td, half), lambda i, *_: (i, 0))],
        out_specs=pl.BlockSpec(memory_space=pl.ANY),
        scratch_shapes=[pltpu.SMEM((2, k, td), I32),
                        pltpu.VMEM((tail_rows, half), U32),
                        pltpu.SemaphoreType.DMA((2,)),
                        pltpu.SemaphoreType.DMA((1,))])
    return pl.pallas_call(
        _dispatch_kernel,
        grid_spec=grid_spec,
        out_shape=jax.ShapeDtypeStruct((n_rows, half), U32),
        compiler_params=_cparams(("arbitrary",)),
        name="dispatch",
    )(pad_lo, pad_n, meta, dest, hnp)


def _block_plan(sizes, starts, mb, n_blocks):
    nblk = (sizes + mb - 1) // mb
    blk_end = jnp.cumsum(nblk)
    n_used = blk_end[-1]
    b = jnp.arange(n_blocks, dtype=I32)
    be = jnp.minimum(jnp.searchsorted(blk_end, b, side='right'), sizes.shape[0] - 1).astype(I32)
    off = starts[be] + (b - (blk_end[be] - nblk[be])) * mb
    last = n_used - 1
    be = jnp.where(b < n_used, be, be[last])
    off = jnp.where(b < n_used, off, off[last])
    return be.astype(I32), off.astype(I32), n_used.reshape(1).astype(I32)


def _moe_kernel(be_ref, off_ref, nu_ref, xs_hbm, wgu_ref, wd_ref, ys_hbm,
                xbuf, ybuf, wgu_b, wd_b, sem_x, sem_y, *, tail_rows):
    b = pl.program_id(0)
    n_used = nu_ref[0]
    mb = ybuf.shape[0]
    half = ybuf.shape[1]
    de = wd_ref.shape[0]

    def x_copy(j):
        off = pl.multiple_of(off_ref[j], SUBLANES)
        return pltpu.make_async_copy(xs_hbm.at[pl.ds(off, mb), :], xbuf.at[j % 2], sem_x.at[j % 2])

    def y_copy(j):
        off = pl.multiple_of(off_ref[j], SUBLANES)
        return pltpu.make_async_copy(ybuf, ys_hbm.at[pl.ds(off, mb), :], sem_y.at[0])

    @pl.when(b == 0)
    def _():
        x_copy(0).start()

    @pl.when(b < n_used)
    def _():
        @pl.when(b + 1 < n_used)
        def _():
            x_copy(b + 1).start()

        first = jnp.logical_or(b == 0, be_ref[b] != be_ref[jnp.maximum(b - 1, 0)])

        @pl.when(first)
        def _():
            wgu_b[...] = wgu_ref[...].astype(BF16)
            wd_b[...] = wd_ref[...].astype(BF16)

        x_copy(b).wait()
        lo, hi = _unpack_halves(xbuf[b % 2])
        gu = (jnp.dot(lo.astype(BF16), wgu_b[0:half, :], preferred_element_type=F32)
              + jnp.dot(hi.astype(BF16), wgu_b[half:, :], preferred_element_type=F32))
        act = _silu(gu[:, :de]) * gu[:, de:]
        y = jnp.dot(act.astype(BF16), wd_b[...], preferred_element_type=F32)

        @pl.when(b > 0)
        def _():
            y_copy(b - 1).wait()

        ybuf[...] = _pack_halves(y[:, :half], y[:, half:])
        y_copy(b).start()

        @pl.when(b == n_used - 1)
        def _():
            y_copy(b).wait()
            _fill_tail(ybuf, ys_hbm, off_ref[b] + mb, tail_rows, sem_y.at[0])


def _moe(xs, block_e, block_off, n_used, w_gate_up, w_down, mb):
    n_rows, half = xs.shape
    n_blocks = block_e.shape[0]
    n_exp, d, de2 = w_gate_up.shape
    de = w_down.shape[1]
    grid_spec = pltpu.PrefetchScalarGridSpec(
        num_scalar_prefetch=3,
        grid=(n_blocks,),
        in_specs=[pl.BlockSpec(memory_space=pl.ANY),
                  pl.BlockSpec((None, d, de2), lambda b, be, off, nu: (be[b], 0, 0)),
                  pl.BlockSpec((None, de, d), lambda b, be, off, nu: (be[b], 0, 0))],
        out_specs=pl.BlockSpec(memory_space=pl.ANY),
        scratch_shapes=[pltpu.VMEM((2, mb, half), U32),
                        pltpu.VMEM((mb, half), U32),
                        pltpu.VMEM((d, de2), BF16),
                        pltpu.VMEM((de, d), BF16),
                        pltpu.SemaphoreType.DMA((2,)),
                        pltpu.SemaphoreType.DMA((1,))])
    return pl.pallas_call(
        functools.partial(_moe_kernel, tail_rows=SUBLANES * n_exp + 2 * mb),
        grid_spec=grid_spec,
        out_shape=jax.ShapeDtypeStruct((n_rows, half), U32),
        compiler_params=_cparams(("arbitrary",)),
        name="moe",
    )(block_e, block_off, n_used, xs, w_gate_up, w_down)


def _combine_kernel(dest_hbm, ys_hbm, tw_ref, x1_ref, hnp_ref, gt_ref, gf_ref, wgu_ref, wd_ref,
                    o_ref, idx_s, gbuf, sem_i, sem_g):
    i = pl.program_id(0)
    nt = pl.num_programs(0)
    tc, d = x1_ref.shape
    half = d // 2
    de = wd_ref.shape[0]
    n_pick = tw_ref.shape[0]

    def idx_copy(j):
        return pltpu.make_async_copy(dest_hbm.at[:, pl.ds(j * tc, tc)], idx_s.at[j % 3], sem_i.at[j % 3])

    def gather_start(j):
        islot = j % 3
        gslot = j % 2

        def body(t, carry):
            for k in range(n_pick):
                src = idx_s[islot, k, t]
                pltpu.make_async_copy(ys_hbm.at[pl.ds(src, 1), :], gbuf.at[gslot, k, pl.ds(t, 1), :],
                                      sem_g.at[gslot]).start()
            return carry

        lax.fori_loop(0, tc, body, 0)

    def gather_wait(j):
        gslot = j % 2
        for k in range(n_pick):
            pltpu.make_async_copy(ys_hbm.at[pl.ds(0, tc), :], gbuf.at[gslot, k], sem_g.at[gslot]).wait()

    @pl.when(i == 0)
    def _():
        idx_copy(0).start()
        idx_copy(0).wait()
        gather_start(0)

        @pl.when(nt > 1)
        def _():
            idx_copy(1).start()

    @pl.when(i + 2 < nt)
    def _():
        idx_copy(i + 2).start()

    @pl.when(i + 1 < nt)
    def _():
        idx_copy(i + 1).wait()
        gather_start(i + 1)

    lo, hi = _unpack_halves(hnp_ref[...])
    gu = (jnp.dot(lo.astype(BF16), wgu_ref[0:half, :], preferred_element_type=F32)
          + jnp.dot(hi.astype(BF16), wgu_ref[half:, :], preferred_element_type=F32))
    act = _silu(gu[:, :de]) * gu[:, de:]
    y = jnp.dot(act.astype(BF16), wd_ref[...], preferred_element_type=F32)

    tw = jnp.concatenate([tw_ref[...], jnp.zeros((LANES - n_pick, tc), F32)], axis=0)
    wcol = jnp.concatenate([tw[:, c0:c0 + LANES].T for c0 in range(0, tc, LANES)], axis=0)

    gather_wait(i)
    gslot = i % 2
    acc_lo = y[:, :half]
    acc_hi = y[:, half:]
    for k in range(n_pick):
        rlo, rhi = _unpack_halves(gbuf[gslot, k])
        wk = wcol[:, k:k + 1]
        acc_lo = acc_lo + wk * rlo
        acc_hi = acc_hi + wk * rhi
    yy = jnp.concatenate([acc_lo, acc_hi], axis=1)
    x2 = x1_ref[...] + gt_ref[...] * yy
    ms = jnp.mean(x2 * x2, axis=-1, keepdims=True)
    o_ref[...] = x2 * lax.rsqrt(ms + EPS) * gf_ref[...]


def _combine(dest, ys, top_w_t, x1, hnp, gt2, g_final, ws_gu, ws_d, seq, tc=128):
    t, d = x1.shape
    half = d // 2
    k = dest.shape[0]
    tpb = seq // tc
    de2 = ws_gu.shape[1]
    return pl.pallas_call(
        _combine_kernel,
        grid=(t // tc,),
        in_specs=[pl.BlockSpec(memory_space=pl.ANY),
                  pl.BlockSpec(memory_space=pl.ANY),
                  pl.BlockSpec((k, tc), lambda i: (0, i)),
                  pl.BlockSpec((tc, d), lambda i: (i, 0)),
                  pl.BlockSpec((tc, half), lambda i: (i, 0)),
                  pl.BlockSpec((None, 1, d), lambda i: (i // tpb, 0, 0)),
                  pl.BlockSpec((1, d), lambda i: (0, 0)),
                  pl.BlockSpec((d, de2), lambda i: (0, 0)),
                  pl.BlockSpec((de2 // 2, d), lambda i: (0, 0))],
        out_specs=pl.BlockSpec((tc, d), lambda i: (i, 0)),
        out_shape=jax.ShapeDtypeStruct((t, d), F32),
        scratch_shapes=[pltpu.SMEM((3, k, tc), I32),
                        pltpu.VMEM((2, k, tc, half), U32),
                        pltpu.SemaphoreType.DMA((3,)),
                        pltpu.SemaphoreType.DMA((2,))],
        compiler_params=_cparams(("arbitrary",)),
        name="combine",
    )(dest, ys, top_w_t, x1, hnp, gt2, g_final.reshape(1, d), ws_gu, ws_d)


def kernel(x, c, w_ada, b_ada, g_mix, w_in, conv_w, conv_b, conv_ln_g, conv_ln_b, qk_conv_w, qk_conv_b, b_igate, b_fgate, mlstm_norm_g, w_out, g_ffn, w_router, router_bias, w_gate_up, w_down, ws_gate_up, ws_down, g_final):
    bsz, seq, d = x.shape
    depth = w_ada.shape[0]
    assert depth == 1, "single-layer block"
    dc = conv_w.shape[2]
    dm = d - dc
    nh = b_igate.shape[1]
    dh = dm // nh
    n_exp = w_router.shape[2]
    t = bsz * seq
    n_main = 2 * dc + 4 * dm
    n_assign = TOP_K * t
    mb = MOE_BLOCK
    assert dc == dm and nh == N_HEADS and 2 * nh <= LANES and n_assign % mb == 0

    mod = _ada(c, w_ada[0], b_ada[0])
    sh1, sc1, gt1, sh2, sc2, gt2 = [m.reshape(bsz, 1, d) for m in jnp.split(mod, 6, axis=-1)]

    x2 = x.reshape(t, d)
    w_main = w_in[0, :, :n_main].astype(BF16)
    w_gate = jnp.zeros((d, LANES), BF16).at[:, :2 * nh].set(w_in[0, :, n_main:].astype(BF16))
    proj, gates = _inproj(x2, sc1, sh1, g_mix[0].reshape(1, d), w_main, w_gate, seq)

    u = _convmod(proj, conv_w[0], conv_b[0], conv_ln_g[0], conv_ln_b[0], bsz, seq)

    gbias = jnp.zeros((1, LANES), F32).at[0, :nh].set(b_igate[0]).at[0, nh:2 * nh].set(b_fgate[0])
    hm = _mlstm(proj, gates, gbias, qk_conv_w[0], qk_conv_b[0], mlstm_norm_g[0],
                bsz, seq, dm, dh, col0=2 * dc // dm)

    x1, hnp, top_idx_t, top_w_t, cnt = _outproj(
        u, hm, x2, gt1, sc2, sh2, g_ffn[0].reshape(1, d), w_out[0].astype(BF16),
        w_router[0].T, router_bias[0], seq)

    sizes = jnp.sum(cnt, axis=1).astype(I32)
    sizes8 = (sizes + SUBLANES - 1) // SUBLANES * SUBLANES
    starts = jnp.cumsum(sizes8) - sizes8
    dest = _plan(top_idx_t, starts.astype(F32).reshape(n_exp, 1))
    pad_n = sizes8 - sizes
    meta = jnp.stack([jnp.sum(pad_n), jnp.sum(sizes8)]).astype(I32)
    n_rows = n_assign + SUBLANES * n_exp + mb
    xs = _dispatch(starts + sizes, pad_n, meta, dest, hnp, n_rows, mb)

    n_blocks = n_assign // mb + n_exp
    block_e, block_off, n_used = _block_plan(sizes, starts, mb, n_blocks)
    ys = _moe(xs, block_e, block_off, n_used, w_gate_up[0], w_down[0], mb)

    out = _combine(dest, ys, top_w_t, x1, hnp, gt2, g_final,
                   ws_gate_up[0].astype(BF16), ws_down[0].astype(BF16), seq)
    return out.reshape(bsz, seq, d)
```

```python
import functools

import jax
import jax.numpy as jnp
from jax import lax
from jax.experimental import pallas as pl
from jax.experimental.pallas import tpu as pltpu

F32 = jnp.float32
BF16 = jnp.bfloat16
U32 = jnp.uint32
I32 = jnp.int32

EPS = 1e-6
STAB_INIT = -1e30
TOP_K = 8
N_GROUPS = 8
TOPK_GROUPS = 4
ROUTED_SCALE = 2.5
N_HEADS = 4

LANES = 128
SUBLANES = 8
VMEM_LIMIT = 56 * 1024 * 1024

MLSTM_CHUNK = 256
MOE_BLOCK = 128
NEG_INF = float("-inf")
HIGHEST = lax.Precision.HIGHEST


def _cparams(sem):
    return pltpu.CompilerParams(dimension_semantics=sem, vmem_limit_bytes=VMEM_LIMIT)


def _sigmoid(x):
    return 1.0 / (1.0 + jnp.exp(-x))


def _silu(x):
    return x * _sigmoid(x)


def _pack_halves(lo, hi):
    lo_b = lax.bitcast_convert_type(lo.astype(BF16).astype(F32), U32)
    hi_b = lax.bitcast_convert_type(hi.astype(BF16).astype(F32), U32)
    return (hi_b & jnp.uint32(0xFFFF0000)) | (lo_b >> 16)


def _unpack_halves(w):
    lo = lax.bitcast_convert_type(w << 16, F32)
    hi = lax.bitcast_convert_type(w & jnp.uint32(0xFFFF0000), F32)
    return lo, hi


def _store_rows(ref, words):
    for s in range(ref.shape[1]):
        ref[:, s, :] = words[:, s * LANES:(s + 1) * LANES]


def _load_rows(ref):
    return jnp.concatenate([ref[:, s, :] for s in range(ref.shape[1])], axis=1)


def _ada_kernel(c_ref, w_ref, b_ref, o_ref):
    cs = _silu(c_ref[...])
    o_ref[...] = jnp.dot(cs, w_ref[...], preferred_element_type=F32, precision=HIGHEST) + b_ref[...]


def _ada(c, w_ada, b_ada):
    bsz, d = c.shape
    n = w_ada.shape[1]
    tn = 1536
    cp = jnp.zeros((SUBLANES, d), F32).at[:bsz].set(c)
    out = pl.pallas_call(
        _ada_kernel,
        grid=(n // tn,),
        in_specs=[pl.BlockSpec((SUBLANES, d), lambda j: (0, 0)),
                  pl.BlockSpec((d, tn), lambda j: (0, j)),
                  pl.BlockSpec((1, tn), lambda j: (0, j))],
        out_specs=pl.BlockSpec((SUBLANES, tn), lambda j: (0, j)),
        out_shape=jax.ShapeDtypeStruct((SUBLANES, n), F32),
        compiler_params=_cparams(("arbitrary",)),
        name="ada",
    )(cp, w_ada, b_ada.reshape(1, n))
    return out[:bsz]


def _inproj_kernel(x_ref, sc_ref, sh_ref, g_ref, w_ref, wg_ref, o_ref, og_ref, h_scr):
    @pl.when(pl.program_id(1) == 0)
    def _():
        x = x_ref[...]
        ms = jnp.mean(x * x, axis=-1, keepdims=True)
        h = (x * lax.rsqrt(ms + EPS) * g_ref[...]) * (1.0 + sc_ref[...]) + sh_ref[...]
        hb = h.astype(BF16)
        h_scr[...] = hb
        og_ref[...] = jnp.dot(hb, wg_ref[...], preferred_element_type=F32)

    o_ref[...] = jnp.dot(h_scr[...], w_ref[...], preferred_element_type=F32).astype(o_ref.dtype)


def _inproj(x2, sc, sh, g, w_main, w_gate, seq, tm=512, tn=1536):
    t, d = x2.shape
    n = w_main.shape[1]
    tpb = seq // tm
    return pl.pallas_call(
        _inproj_kernel,
        grid=(t // tm, n // tn),
        in_specs=[pl.BlockSpec((tm, d), lambda i, j: (i, 0)),
                  pl.BlockSpec((None, 1, d), lambda i, j: (i // tpb, 0, 0)),
                  pl.BlockSpec((None, 1, d), lambda i, j: (i // tpb, 0, 0)),
                  pl.BlockSpec((1, d), lambda i, j: (0, 0)),
                  pl.BlockSpec((d, tn), lambda i, j: (0, j)),
                  pl.BlockSpec((d, LANES), lambda i, j: (0, 0))],
        out_specs=[pl.BlockSpec((tm, tn), lambda i, j: (i, j)),
                   pl.BlockSpec((tm, LANES), lambda i, j: (i, 0))],
        out_shape=[jax.ShapeDtypeStruct((t, n), BF16),
                   jax.ShapeDtypeStruct((t, LANES), F32)],
        scratch_shapes=[pltpu.VMEM((tm, d), BF16)],
        compiler_params=_cparams(("arbitrary", "arbitrary")),
        name="inproj",
    )(x2, sc, sh, g, w_main, w_gate)


_CONV_RC = 128
_CONV_HIST = 32


def _convmod_kernel(a_ref, gate_ref, w_ref, b_ref, lg_ref, lb_ref, o_ref, ext_scr, y_scr, *, width):
    tt, ch = a_ref.shape
    hist = _CONV_HIST
    rc = _CONV_RC

    @pl.when(pl.program_id(1) == 0)
    def _():
        ext_scr[0:hist, :] = jnp.zeros((hist, ch), F32)

    @pl.when(pl.program_id(1) != 0)
    def _():
        ext_scr[0:hist, :] = ext_scr[tt:tt + hist, :]

    ext_scr[hist:hist + tt, :] = a_ref[...].astype(F32) * _sigmoid(gate_ref[...].astype(F32))

    off0 = hist - (width - 1)
    nrow = rc + hist
    for r0 in range(0, tt, rc):
        for c0 in range(0, ch, LANES):
            blk = ext_scr[r0:r0 + nrow, c0:c0 + LANES]
            acc = jnp.zeros((rc, LANES), F32)
            for r in range(SUBLANES):
                taps = [j for j in range(width) if (off0 + j) % SUBLANES == r]
                if not taps:
                    continue
                rolled = blk if r == 0 else pltpu.roll(blk, nrow - r, axis=0)
                for j in taps:
                    a0 = off0 + j - r
                    acc = acc + w_ref[j:j + 1, c0:c0 + LANES] * rolled[a0:a0 + rc, :]
            y_scr[r0:r0 + rc, c0:c0 + LANES] = acc

    y = y_scr[...] + b_ref[...]
    mu = jnp.mean(y, axis=-1, keepdims=True)
    yc = y - mu
    var = jnp.mean(yc * yc, axis=-1, keepdims=True)
    z = yc * lax.rsqrt(var + EPS) * lg_ref[...] + lb_ref[...]
    o_ref[...] = _silu(z).astype(o_ref.dtype)


def _convmod(proj, conv_w, conv_b, ln_g, ln_b, bsz, seq, tt=512):
    width, ch = conv_w.shape
    assert width - 1 <= _CONV_HIST and tt % _CONV_RC == 0
    tpb = seq // tt
    vec = lambda: pl.BlockSpec((1, ch), lambda b, i: (0, 0))
    return pl.pallas_call(
        functools.partial(_convmod_kernel, width=width),
        grid=(bsz, tpb),
        in_specs=[pl.BlockSpec((tt, ch), lambda b, i: (b * tpb + i, 0)),
                  pl.BlockSpec((tt, ch), lambda b, i: (b * tpb + i, 1)),
                  pl.BlockSpec((width, ch), lambda b, i: (0, 0)),
                  vec(), vec(), vec()],
        out_specs=pl.BlockSpec((tt, ch), lambda b, i: (b * tpb + i, 0)),
        out_shape=jax.ShapeDtypeStruct((bsz * seq, ch), BF16),
        scratch_shapes=[pltpu.VMEM((tt + _CONV_HIST, ch), F32),
                        pltpu.VMEM((tt, ch), F32)],
        compiler_params=_cparams(("arbitrary", "arbitrary")),
        name="convmod",
    )(proj, proj, conv_w, conv_b.reshape(1, ch), ln_g.reshape(1, ch), ln_b.reshape(1, ch))


def _log_sigmoid(x):
    return -(jnp.maximum(-x, 0.0) + jnp.log(1.0 + jnp.exp(-jnp.abs(x))))


def _mlstm_kernel(q_ref, k_ref, v_ref, og_ref, gates_ref, gbias_ref, cw_ref, cb_ref, ng_ref,
                  o_ref, hist_scr, c_scr, n_scr, m_scr, *, dh, qk_width):
    ln = q_ref.shape[0]
    dm = q_ref.shape[1]
    nh = dm // dh
    hist = SUBLANES

    @pl.when(pl.program_id(1) == 0)
    def _():
        hist_scr[...] = jnp.zeros(hist_scr.shape, F32)
        c_scr[...] = jnp.zeros(c_scr.shape, F32)
        n_scr[...] = jnp.zeros(n_scr.shape, F32)
        m_scr[...] = jnp.full(m_scr.shape, STAB_INIT, F32)

    gts = gates_ref[...] + gbias_ref[...]
    lf = _log_sigmoid(gts)
    row = lax.broadcasted_iota(I32, (ln, ln), 0)
    col = lax.broadcasted_iota(I32, (ln, ln), 1)
    causal = row >= col
    tri = causal.astype(F32)
    bcum = jnp.dot(tri, lf, preferred_element_type=F32, precision=HIGHEST)
    bcum_t = bcum.T
    gts_t = gts.T

    def short_conv(raw_ref, hist_off, c0):
        raw = raw_ref[:, c0:c0 + dh].astype(F32)
        prev = hist_scr[:, hist_off + c0:hist_off + c0 + dh]
        ext = jnp.concatenate([prev, raw], axis=0)
        n = hist + ln
        acc = jnp.zeros((ln, dh), F32) + cb_ref[:, hist_off + c0:hist_off + c0 + dh]
        for j in range(qk_width):
            s = hist - (qk_width - 1) + j
            w = cw_ref[j:j + 1, hist_off + c0:hist_off + c0 + dh]
            if s % SUBLANES == 0:
                acc = acc + w * ext[s:s + ln, :]
            else:
                acc = acc + w * pltpu.roll(ext, n - s, axis=0)[0:ln, :]
        return _silu(acc)

    for h in range(nh):
        c0 = h * dh
        q = short_conv(q_ref, 0, c0)
        k = short_conv(k_ref, dm, c0) * (dh ** -0.5)
        qb = q.astype(BF16)
        kb = k.astype(BF16)
        v = v_ref[:, c0:c0 + dh]
        vf = v.astype(F32)

        b_col = bcum[:, nh + h:nh + h + 1]
        b_row = bcum_t[nh + h:nh + h + 1, :]
        ig_col = gts[:, h:h + 1]
        ig_row = gts_t[h:h + 1, :]
        m_prev = m_scr[h, 0:1, 0:1]

        dmat = jnp.where(causal, b_col - b_row + ig_row, NEG_INF)
        m_inter = b_col + m_prev
        m_t = jnp.maximum(m_inter, jnp.max(dmat, axis=1, keepdims=True))
        pmat = jnp.exp(dmat - m_t)
        qk = lax.dot_general(qb, kb, (((1,), (1,)), ((), ())), preferred_element_type=F32)
        s = qk * pmat
        a = jnp.exp(m_inter - m_t)
        inter = jnp.dot(qb, c_scr[h].astype(BF16), preferred_element_type=F32)
        num = jnp.dot(s.astype(BF16), v, preferred_element_type=F32) + a * inter
        den = jnp.sum(s, axis=1, keepdims=True) + a * jnp.sum(q * n_scr[h], axis=1, keepdims=True)
        hv = num / jnp.maximum(jnp.abs(den), jnp.exp(-m_t))

        mu = jnp.mean(hv, axis=-1, keepdims=True)
        hc = hv - mu
        var = jnp.mean(hc * hc, axis=-1, keepdims=True)
        hn = hc * lax.rsqrt(var + EPS) * ng_ref[:, c0:c0 + dh]
        og = og_ref[:, c0:c0 + dh].astype(F32)
        o_ref[:, c0:c0 + dh] = (_sigmoid(og) * hn).astype(o_ref.dtype)

        b_last = b_col[ln - 1:ln, :]
        g_col = b_last - b_col + ig_col
        m_new = jnp.maximum(b_last + m_prev, jnp.max(g_col, axis=0, keepdims=True))
        wk = jnp.exp(g_col - m_new)
        decay = jnp.exp(b_last + m_prev - m_new)
        upd = lax.dot_general(kb, (vf * wk).astype(BF16), (((0,), (0,)), ((), ())),
                              preferred_element_type=F32)
        c_scr[h] = decay * c_scr[h] + upd
        n_scr[h] = decay * n_scr[h] + jnp.sum(k * wk, axis=0, keepdims=True)
        m_scr[h] = jnp.broadcast_to(m_new, m_scr.shape[1:])

    hist_scr[:, 0:dm] = q_ref[ln - hist:ln, :].astype(F32)
    hist_scr[:, dm:2 * dm] = k_ref[ln - hist:ln, :].astype(F32)


def _mlstm(proj, gates, gbias, qk_conv_w, qk_conv_b, norm_g, bsz, seq, dm, dh, col0):
    ln = MLSTM_CHUNK
    nc = seq // ln
    nh = dm // dh
    width = qk_conv_w.shape[0]
    blk = lambda off: pl.BlockSpec((ln, dm), lambda b, c: (b * nc + c, col0 + off))
    return pl.pallas_call(
        functools.partial(_mlstm_kernel, dh=dh, qk_width=width),
        grid=(bsz, nc),
        in_specs=[blk(0), blk(1), blk(2), blk(3),
                  pl.BlockSpec((ln, LANES), lambda b, c: (b * nc + c, 0)),
                  pl.BlockSpec((1, LANES), lambda b, c: (0, 0)),
                  pl.BlockSpec((width, 2 * dm), lambda b, c: (0, 0)),
                  pl.BlockSpec((1, 2 * dm), lambda b, c: (0, 0)),
                  pl.BlockSpec((1, dm), lambda b, c: (0, 0))],
        out_specs=pl.BlockSpec((ln, dm), lambda b, c: (b * nc + c, 0)),
        out_shape=jax.ShapeDtypeStruct((bsz * seq, dm), BF16),
        scratch_shapes=[pltpu.VMEM((SUBLANES, 2 * dm), F32),
                        pltpu.VMEM((nh, dh, dh), F32),
                        pltpu.VMEM((nh, 1, dh), F32),
                        pltpu.VMEM((nh, SUBLANES, LANES), F32)],
        compiler_params=_cparams(("arbitrary", "arbitrary")),
        name="mlstm",
    )(proj, proj, proj, proj, gates, gbias, qk_conv_w, qk_conv_b.reshape(1, 2 * dm),
      norm_g.reshape(1, dm))


def _route_chunk(logits_t, bias_col, n_exp):
    gsz = n_exp // N_GROUPS
    scores = _sigmoid(logits_t)
    choice = scores + bias_col
    rows_g = lax.broadcasted_iota(I32, (gsz, LANES), 0).astype(F32)
    gscore = []
    for g in range(N_GROUPS):
        v = choice[g * gsz:(g + 1) * gsz, :]
        m1 = jnp.max(v, axis=0, keepdims=True)
        i1 = jnp.min(jnp.where(v == m1, rows_g, float(gsz)), axis=0, keepdims=True)
        m2 = jnp.max(jnp.where(rows_g == i1, NEG_INF, v), axis=0, keepdims=True)
        gscore.append(m1 + m2)
    gs = jnp.concatenate(gscore, axis=0)
    giota = lax.broadcasted_iota(I32, (N_GROUPS, LANES), 0).astype(F32)
    keep = jnp.zeros((N_GROUPS, LANES), F32)
    for _ in range(TOPK_GROUPS):
        m = jnp.max(gs, axis=0, keepdims=True)
        gi = jnp.min(jnp.where(gs == m, giota, float(N_GROUPS)), axis=0, keepdims=True)
        sel = giota == gi
        keep = jnp.where(sel, 1.0, keep)
        gs = jnp.where(sel, NEG_INF, gs)
    mc = jnp.concatenate(
        [jnp.where(keep[g:g + 1, :] > 0.5, choice[g * gsz:(g + 1) * gsz, :], NEG_INF)
         for g in range(N_GROUPS)], axis=0)
    rows = lax.broadcasted_iota(I32, (n_exp, LANES), 0).astype(F32)
    idxs, ws = [], []
    picked = jnp.zeros((n_exp, LANES), F32)
    for _ in range(TOP_K):
        m = jnp.max(mc, axis=0, keepdims=True)
        idx = jnp.min(jnp.where(mc == m, rows, float(n_exp)), axis=0, keepdims=True)
        sel = rows == idx
        ws.append(jnp.sum(jnp.where(sel, scores, 0.0), axis=0, keepdims=True))
        idxs.append(idx)
        picked = jnp.where(sel, 1.0, picked)
        mc = jnp.where(sel, NEG_INF, mc)
    w = jnp.concatenate(ws, axis=0)
    w = w / jnp.sum(w, axis=0, keepdims=True) * ROUTED_SCALE
    return jnp.concatenate(idxs, axis=0).astype(I32), w, picked


def _outproj_kernel(u_ref, hm_ref, x_ref, gt_ref, sc_ref, sh_ref, g_ref, w_ref, wr_ref, rb_ref,
                    x1_ref, hnp_ref, idx_ref, tw_ref, cnt_ref):
    tm, d = x_ref.shape
    dc = u_ref.shape[1]
    half = d // 2
    n_exp = wr_ref.shape[0]

    @pl.when(pl.program_id(0) == 0)
    def _():
        cnt_ref[...] = jnp.zeros(cnt_ref.shape, F32)

    mix = (jnp.dot(u_ref[...], w_ref[0:dc, :], preferred_element_type=F32)
           + jnp.dot(hm_ref[...], w_ref[dc:, :], preferred_element_type=F32))
    x1 = x_ref[...] + gt_ref[...] * mix
    x1_ref[...] = x1
    ms = jnp.mean(x1 * x1, axis=-1, keepdims=True)
    hn = (x1 * lax.rsqrt(ms + EPS) * g_ref[...]) * (1.0 + sc_ref[...]) + sh_ref[...]
    _store_rows(hnp_ref, _pack_halves(hn[:, :half], hn[:, half:]))
    logits_t = lax.dot_general(wr_ref[...], hn, (((1,), (1,)), ((), ())),
                               preferred_element_type=F32, precision=HIGHEST)
    cnt = cnt_ref[...]
    for c0 in range(0, tm, LANES):
        idx, w, picked = _route_chunk(logits_t[:, c0:c0 + LANES], rb_ref[...], n_exp)
        idx_ref[:, c0:c0 + LANES] = idx
        tw_ref[:, c0:c0 + LANES] = w
        cnt = cnt + picked
    cnt_ref[...] = cnt


def _outproj(u, hm, x2, gt1, sc2, sh2, g_ffn, w_out, w_router_t, router_bias, seq, tm=256):
    t, d = x2.shape
    dc = u.shape[1]
    n_exp = w_router_t.shape[0]
    tpb = seq // tm
    per_b = lambda: pl.BlockSpec((None, 1, d), lambda i: (i // tpb, 0, 0))
    return pl.pallas_call(
        _outproj_kernel,
        grid=(t // tm,),
        in_specs=[pl.BlockSpec((tm, dc), lambda i: (i, 0)),
                  pl.BlockSpec((tm, d - dc), lambda i: (i, 0)),
                  pl.BlockSpec((tm, d), lambda i: (i, 0)),
                  per_b(), per_b(), per_b(),
                  pl.BlockSpec((1, d), lambda i: (0, 0)),
                  pl.BlockSpec((d, d), lambda i: (0, 0)),
                  pl.BlockSpec((n_exp, d), lambda i: (0, 0)),
                  pl.BlockSpec((n_exp, 1), lambda i: (0, 0))],
        out_specs=[pl.BlockSpec((tm, d), lambda i: (i, 0)),
                   pl.BlockSpec((tm, d // 2 // LANES, LANES), lambda i: (i, 0, 0)),
                   pl.BlockSpec((TOP_K, tm), lambda i: (0, i)),
                   pl.BlockSpec((TOP_K, tm), lambda i: (0, i)),
                   pl.BlockSpec((n_exp, LANES), lambda i: (0, 0))],
        out_shape=[jax.ShapeDtypeStruct((t, d), F32),
                   jax.ShapeDtypeStruct((t, d // 2 // LANES, LANES), U32),
                   jax.ShapeDtypeStruct((TOP_K, t), I32),
                   jax.ShapeDtypeStruct((TOP_K, t), F32),
                   jax.ShapeDtypeStruct((n_exp, LANES), F32)],
        compiler_params=_cparams(("arbitrary",)),
        name="outproj_route",
    )(u, hm, x2, gt1, sc2, sh2, g_ffn, w_out, w_router_t, router_bias.reshape(n_exp, 1))


def _plan_kernel(idx_ref, start_ref, dest_ref, run_scr):
    n_exp = start_ref.shape[0]
    tp = idx_ref.shape[1]

    @pl.when(pl.program_id(0) == 0)
    def _():
        run_scr[...] = start_ref[...]

    rows = lax.broadcasted_iota(I32, (n_exp, tp), 0)
    earlier = (lax.broadcasted_iota(I32, (tp, tp), 0)
               < lax.broadcasted_iota(I32, (tp, tp), 1)).astype(BF16)
    base = run_scr[...]
    for k in range(TOP_K):
        oh = rows == idx_ref[k:k + 1, :]
        ohf = oh.astype(F32)
        pref = jnp.dot(ohf.astype(BF16), earlier, preferred_element_type=F32)
        dest = jnp.sum(jnp.where(oh, pref + base, 0.0), axis=0, keepdims=True)
        dest_ref[k:k + 1, :] = dest.astype(I32)
        base = base + jnp.sum(ohf, axis=1, keepdims=True)
    run_scr[...] = base


def _plan(top_idx_t, starts_col, tp=512):
    k, t = top_idx_t.shape
    n_exp = starts_col.shape[0]
    return pl.pallas_call(
        _plan_kernel,
        grid=(t // tp,),
        in_specs=[pl.BlockSpec((k, tp), lambda i: (0, i)),
                  pl.BlockSpec((n_exp, 1), lambda i: (0, 0))],
        out_specs=pl.BlockSpec((k, tp), lambda i: (0, i)),
        out_shape=jax.ShapeDtypeStruct((k, t), I32),
        scratch_shapes=[pltpu.VMEM((n_exp, 1), F32)],
        compiler_params=_cparams(("arbitrary",)),
        name="plan",
    )(top_idx_t, starts_col)


def _dispatch_kernel(dest_hbm, hnp_ref, xs_hbm, idx_s, zero_scr, sem_i, sem_r):
    i = pl.program_id(0)
    nt = pl.num_programs(0)
    td = hnp_ref.shape[0]
    n_pick = idx_s.shape[1]

    def idx_copy(j):
        return pltpu.make_async_copy(dest_hbm.at[:, pl.ds(j * td, td)], idx_s.at[j % 2], sem_i.at[j % 2])

    @pl.when(i == 0)
    def _():
        idx_copy(0).start()

    @pl.when(i + 1 < nt)
    def _():
        idx_copy(i + 1).start()

    idx_copy(i).wait()
    slot = i % 2

    def body(t, carry):
        for k in range(n_pick):
            dst = idx_s[slot, k, t]
            pltpu.make_async_copy(hnp_ref.at[t], xs_hbm.at[dst], sem_r.at[0]).start()
        return carry

    lax.fori_loop(0, td, body, 0)

    for _ in range(n_pick):
        pltpu.make_async_copy(hnp_ref, xs_hbm.at[pl.ds(0, td)], sem_r.at[0]).wait()

    @pl.when(i == nt - 1)
    def _():
        zero_scr[...] = jnp.zeros(zero_scr.shape, zero_scr.dtype)
        nz = zero_scr.shape[0]
        tail = pltpu.make_async_copy(zero_scr, xs_hbm.at[pl.ds(xs_hbm.shape[0] - nz, nz)], sem_r.at[0])
        tail.start()
        tail.wait()


def _dispatch(dest, hnp, n_rows, tail_rows, td=256):
    t, s, _ = hnp.shape
    k = dest.shape[0]
    return pl.pallas_call(
        _dispatch_kernel,
        grid=(t // td,),
        in_specs=[pl.BlockSpec(memory_space=pl.ANY),
                  pl.BlockSpec((td, s, LANES), lambda i: (i, 0, 0))],
        out_specs=pl.BlockSpec(memory_space=pl.ANY),
        out_shape=jax.ShapeDtypeStruct((n_rows, s, LANES), U32),
        scratch_shapes=[pltpu.SMEM((2, k, td), I32),
                        pltpu.VMEM((tail_rows, s, LANES), U32),
                        pltpu.SemaphoreType.DMA((2,)),
                        pltpu.SemaphoreType.DMA((1,))],
        compiler_params=_cparams(("arbitrary",)),
        name="dispatch",
    )(dest, hnp)


def _moe_kernel(start_ref, size_ref, bstart_ref, xs_hbm, wgu_hbm, wd_hbm, ys_hbm,
                xbuf, ybuf, wgu_f, wd_f, wgu_b, wd_b, sem_x, sem_y, sem_w, *, n_assign):
    e = pl.program_id(0)
    n_exp = pl.num_programs(0)
    mb = ybuf.shape[0]
    half = ybuf.shape[1] * ybuf.shape[2]
    de = wd_b.shape[0]

    def x_copy(off, slot):
        return pltpu.make_async_copy(xs_hbm.at[pl.ds(off, mb)], xbuf.at[slot], sem_x.at[slot])

    def y_copy(off):
        return pltpu.make_async_copy(ybuf, ys_hbm.at[pl.ds(off, mb)], sem_y.at[0])

    def w_copies(j):
        slot = j % 2
        return (pltpu.make_async_copy(wgu_hbm.at[j], wgu_f.at[slot], sem_w.at[0, slot]),
                pltpu.make_async_copy(wd_hbm.at[j], wd_f.at[slot], sem_w.at[1, slot]))

    @pl.when(e == 0)
    def _():
        for cp in w_copies(0):
            cp.start()
        x_copy(0, 0).start()

    @pl.when(e + 1 < n_exp)
    def _():
        for cp in w_copies(e + 1):
            cp.start()

    for cp in w_copies(e):
        cp.wait()

    st = start_ref[e]
    sz = size_ref[e]
    nb = (sz + mb - 1) // mb
    b0 = bstart_ref[e]

    @pl.when(nb > 0)
    def _():
        wslot = e % 2
        wgu_b[...] = wgu_f[wslot].astype(BF16)
        wd_b[...] = wd_f[wslot].astype(BF16)

        def body(i, carry):
            b = b0 + i
            off = st + i * mb
            more = i + 1 < nb
            nxt = jnp.where(more, off + mb, st + sz)

            @pl.when(jnp.logical_or(more, nxt < n_assign))
            def _():
                x_copy(nxt, (b + 1) % 2).start()

            x_copy(off, b % 2).wait()
            lo, hi = _unpack_halves(_load_rows(xbuf.at[b % 2]))
            gu = (jnp.dot(lo.astype(BF16), wgu_b[0:half, :], preferred_element_type=F32)
                  + jnp.dot(hi.astype(BF16), wgu_b[half:, :], preferred_element_type=F32))
            act = _silu(gu[:, :de]) * gu[:, de:]
            y = jnp.dot(act.astype(BF16), wd_b[...], preferred_element_type=F32)

            @pl.when(b > 0)
            def _():
                y_copy(0).wait()

            _store_rows(ybuf, _pack_halves(y[:, :half], y[:, half:]))
            y_copy(off).start()
            return carry

        lax.fori_loop(0, nb, body, 0)

    @pl.when(e == n_exp - 1)
    def _():
        y_copy(0).wait()
        tail = y_copy(ys_hbm.shape[0] - mb)
        tail.start()
        tail.wait()


def _moe(xs, starts, sizes, bstart, w_gate_up, w_down, mb, n_assign):
    n_rows, s, _ = xs.shape
    n_exp, d, de2 = w_gate_up.shape
    de = w_down.shape[1]
    grid_spec = pltpu.PrefetchScalarGridSpec(
        num_scalar_prefetch=3,
        grid=(n_exp,),
        in_specs=[pl.BlockSpec(memory_space=pl.ANY),
                  pl.BlockSpec(memory_space=pl.ANY),
                  pl.BlockSpec(memory_space=pl.ANY)],
        out_specs=pl.BlockSpec(memory_space=pl.ANY),
        scratch_shapes=[pltpu.VMEM((2, mb, s, LANES), U32),
                        pltpu.VMEM((mb, s, LANES), U32),
                        pltpu.VMEM((2, d, de2), F32),
                        pltpu.VMEM((2, de, d), F32),
                        pltpu.VMEM((d, de2), BF16),
                        pltpu.VMEM((de, d), BF16),
                        pltpu.SemaphoreType.DMA((2,)),
                        pltpu.SemaphoreType.DMA((1,)),
                        pltpu.SemaphoreType.DMA((2, 2))])
    return pl.pallas_call(
        functools.partial(_moe_kernel, n_assign=n_assign),
        grid_spec=grid_spec,
        out_shape=jax.ShapeDtypeStruct((n_rows, s, LANES), U32),
        compiler_params=_cparams(("arbitrary",)),
        name="moe",
    )(starts, sizes, bstart, xs, w_gate_up, w_down)


def _combine_kernel(dest_hbm, ys_hbm, tw_ref, x1_ref, hnp_ref, gt_ref, gf_ref, wgu_ref, wd_ref,
                    o_ref, idx_s, gbuf, sem_i, sem_g):
    i = pl.program_id(0)
    nt = pl.num_programs(0)
    tc, d = x1_ref.shape
    half = d // 2
    de = wd_ref.shape[0]
    n_pick = tw_ref.shape[0]

    def idx_copy(j):
        return pltpu.make_async_copy(dest_hbm.at[:, pl.ds(j * tc, tc)], idx_s.at[j % 3], sem_i.at[j % 3])

    def gather_start(j):
        islot = j % 3
        gslot = j % 2

        def body(t, carry):
            for k in range(n_pick):
                src = idx_s[islot, k, t]
                pltpu.make_async_copy(ys_hbm.at[src], gbuf.at[gslot, k, t], sem_g.at[gslot]).start()
            return carry

        lax.fori_loop(0, tc, body, 0)

    def gather_wait(j):
        gslot = j % 2
        for k in range(n_pick):
            pltpu.make_async_copy(ys_hbm.at[pl.ds(0, tc)], gbuf.at[gslot, k], sem_g.at[gslot]).wait()

    @pl.when(i == 0)
    def _():
        idx_copy(0).start()
        idx_copy(0).wait()
        gather_start(0)

        @pl.when(nt > 1)
        def _():
            idx_copy(1).start()

    @pl.when(i + 2 < nt)
    def _():
        idx_copy(i + 2).start()

    @pl.when(i + 1 < nt)
    def _():
        idx_copy(i + 1).wait()
        gather_start(i + 1)

    lo, hi = _unpack_halves(_load_rows(hnp_ref))
    gu = (jnp.dot(lo.astype(BF16), wgu_ref[0:half, :], preferred_element_type=F32)
          + jnp.dot(hi.astype(BF16), wgu_ref[half:, :], preferred_element_type=F32))
    act = _silu(gu[:, :de]) * gu[:, de:]
    y = jnp.dot(act.astype(BF16), wd_ref[...], preferred_element_type=F32)

    tw = jnp.concatenate([tw_ref[...], jnp.zeros((LANES - n_pick, tc), F32)], axis=0)
    wcol = jnp.concatenate([tw[:, c0:c0 + LANES].T for c0 in range(0, tc, LANES)], axis=0)

    gather_wait(i)
    gslot = i % 2
    acc_lo = y[:, :half]
    acc_hi = y[:, half:]
    for k in range(n_pick):
        rlo, rhi = _unpack_halves(_load_rows(gbuf.at[gslot, k]))
        wk = wcol[:, k:k + 1]
        acc_lo = acc_lo + wk * rlo
        acc_hi = acc_hi + wk * rhi
    yy = jnp.concatenate([acc_lo, acc_hi], axis=1)
    x2 = x1_ref[...] + gt_ref[...] * yy
    ms = jnp.mean(x2 * x2, axis=-1, keepdims=True)
    o_ref[...] = x2 * lax.rsqrt(ms + EPS) * gf_ref[...]


def _combine(dest, ys, top_w_t, x1, hnp, gt2, g_final, ws_gu, ws_d, seq, tc=128):
    t, d = x1.shape
    half = d // 2
    k = dest.shape[0]
    tpb = seq // tc
    de2 = ws_gu.shape[1]
    return pl.pallas_call(
        _combine_kernel,
        grid=(t // tc,),
        in_specs=[pl.BlockSpec(memory_space=pl.ANY),
                  pl.BlockSpec(memory_space=pl.ANY),
                  pl.BlockSpec((k, tc), lambda i: (0, i)),
                  pl.BlockSpec((tc, d), lambda i: (i, 0)),
                  pl.BlockSpec((tc, half // LANES, LANES), lambda i: (i, 0, 0)),
                  pl.BlockSpec((None, 1, d), lambda i: (i // tpb, 0, 0)),
                  pl.BlockSpec((1, d), lambda i: (0, 0)),
                  pl.BlockSpec((d, de2), lambda i: (0, 0)),
                  pl.BlockSpec((de2 // 2, d), lambda i: (0, 0))],
        out_specs=pl.BlockSpec((tc, d), lambda i: (i, 0)),
        out_shape=jax.ShapeDtypeStruct((t, d), F32),
        scratch_shapes=[pltpu.SMEM((3, k, tc), I32),
                        pltpu.VMEM((2, k, tc, half // LANES, LANES), U32),
                        pltpu.SemaphoreType.DMA((3,)),
                        pltpu.SemaphoreType.DMA((2,))],
        compiler_params=_cparams(("arbitrary",)),
        name="combine",
    )(dest, ys, top_w_t, x1, hnp, gt2, g_final.reshape(1, d), ws_gu, ws_d)


def kernel(x, c, w_ada, b_ada, g_mix, w_in, conv_w, conv_b, conv_ln_g, conv_ln_b, qk_conv_w, qk_conv_b, b_igate, b_fgate, mlstm_norm_g, w_out, g_ffn, w_router, router_bias, w_gate_up, w_down, ws_gate_up, ws_down, g_final):
    bsz, seq, d = x.shape
    depth = w_ada.shape[0]
    assert depth == 1, "single-layer block"
    dc = conv_w.shape[2]
    dm = d - dc
    nh = b_igate.shape[1]
    dh = dm // nh
    n_exp = w_router.shape[2]
    t = bsz * seq
    n_main = 2 * dc + 4 * dm
    n_assign = TOP_K * t
    mb = MOE_BLOCK
    assert dc == dm and nh == N_HEADS and 2 * nh <= LANES and n_assign % mb == 0

    mod = _ada(c, w_ada[0], b_ada[0])
    sh1, sc1, gt1, sh2, sc2, gt2 = [m.reshape(bsz, 1, d) for m in jnp.split(mod, 6, axis=-1)]

    x2 = x.reshape(t, d)
    w_main = w_in[0, :, :n_main].astype(BF16)
    w_gate = jnp.zeros((d, LANES), BF16).at[:, :2 * nh].set(w_in[0, :, n_main:].astype(BF16))
    proj, gates = _inproj(x2, sc1, sh1, g_mix[0].reshape(1, d), w_main, w_gate, seq)

    u = _convmod(proj, conv_w[0], conv_b[0], conv_ln_g[0], conv_ln_b[0], bsz, seq)

    gbias = jnp.zeros((1, LANES), F32).at[0, :nh].set(b_igate[0]).at[0, nh:2 * nh].set(b_fgate[0])
    hm = _mlstm(proj, gates, gbias, qk_conv_w[0], qk_conv_b[0], mlstm_norm_g[0],
                bsz, seq, dm, dh, col0=2 * dc // dm)

    x1, hnp, top_idx_t, top_w_t, cnt = _outproj(
        u, hm, x2, gt1, sc2, sh2, g_ffn[0].reshape(1, d), w_out[0].astype(BF16),
        w_router[0].T, router_bias[0], seq)

    sizes = jnp.sum(cnt, axis=1).astype(I32)
    starts = jnp.cumsum(sizes) - sizes
    nblk = (sizes + mb - 1) // mb
    bstart = jnp.cumsum(nblk) - nblk
    dest = _plan(top_idx_t, starts.astype(F32).reshape(n_exp, 1))
    xs = _dispatch(dest, hnp, n_assign + mb, mb)
    ys = _moe(xs, starts, sizes, bstart, w_gate_up[0], w_down[0], mb, n_assign)

    out = _combine(dest, ys, top_w_t, x1, hnp, gt2, g_final,
                   ws_gate_up[0].astype(BF16), ws_down[0].astype(BF16), seq)
    return out.reshape(bsz, seq, d)
```

```python
import functools

import jax
import jax.numpy as jnp
from jax import lax
from jax.experimental import pallas as pl
from jax.experimental.pallas import tpu as pltpu

F32 = jnp.float32
BF16 = jnp.bfloat16
U32 = jnp.uint32
I32 = jnp.int32

EPS = 1e-6
STAB_INIT = -1e30
TOP_K = 8
N_GROUPS = 8
TOPK_GROUPS = 4
ROUTED_SCALE = 2.5
N_HEADS = 4

LANES = 128
SUBLANES = 8
VMEM_LIMIT = 56 * 1024 * 1024

MLSTM_CHUNK = 256
MOE_BLOCK = 128
NEG_INF = float("-inf")
HIGHEST = lax.Precision.HIGHEST


def _cparams(sem):
    return pltpu.CompilerParams(dimension_semantics=sem, vmem_limit_bytes=VMEM_LIMIT)


def _sigmoid(x):
    return 1.0 / (1.0 + jnp.exp(-x))


def _silu(x):
    return x * _sigmoid(x)


def _pack_halves(lo, hi):
    lo_b = lax.bitcast_convert_type(lo.astype(BF16).astype(F32), U32)
    hi_b = lax.bitcast_convert_type(hi.astype(BF16).astype(F32), U32)
    return (hi_b & jnp.uint32(0xFFFF0000)) | (lo_b >> 16)


def _unpack_halves(w):
    lo = lax.bitcast_convert_type(w << 16, F32)
    hi = lax.bitcast_convert_type(w & jnp.uint32(0xFFFF0000), F32)
    return lo, hi


def _row_slab(ref, r, s_per_row, n=1):
    return ref.at[pl.ds(pl.multiple_of(r * s_per_row, s_per_row), n * s_per_row), :]


def _store_rows(ref, words):
    m = words.shape[0]
    s_per_row = words.shape[1] // LANES
    for s in range(s_per_row):
        ref[pl.ds(s, m, stride=s_per_row), :] = words[:, s * LANES:(s + 1) * LANES]


def _load_rows(ref, s_per_row):
    m = ref.shape[0] // s_per_row
    return jnp.concatenate([ref[pl.ds(s, m, stride=s_per_row), :] for s in range(s_per_row)], axis=1)


def _ada_kernel(c_ref, w_ref, b_ref, o_ref):
    cs = _silu(c_ref[...])
    o_ref[...] = jnp.dot(cs, w_ref[...], preferred_element_type=F32, precision=HIGHEST) + b_ref[...]


def _ada(c, w_ada, b_ada):
    bsz, d = c.shape
    n = w_ada.shape[1]
    tn = 1536
    cp = jnp.zeros((SUBLANES, d), F32).at[:bsz].set(c)
    out = pl.pallas_call(
        _ada_kernel,
        grid=(n // tn,),
        in_specs=[pl.BlockSpec((SUBLANES, d), lambda j: (0, 0)),
                  pl.BlockSpec((d, tn), lambda j: (0, j)),
                  pl.BlockSpec((1, tn), lambda j: (0, j))],
        out_specs=pl.BlockSpec((SUBLANES, tn), lambda j: (0, j)),
        out_shape=jax.ShapeDtypeStruct((SUBLANES, n), F32),
        compiler_params=_cparams(("arbitrary",)),
        name="ada",
    )(cp, w_ada, b_ada.reshape(1, n))
    return out[:bsz]


def _inproj_kernel(x_ref, sc_ref, sh_ref, g_ref, w_ref, wg_ref, o_ref, og_ref, h_scr):
    @pl.when(pl.program_id(1) == 0)
    def _():
        x = x_ref[...]
        ms = jnp.mean(x * x, axis=-1, keepdims=True)
        h = (x * lax.rsqrt(ms + EPS) * g_ref[...]) * (1.0 + sc_ref[...]) + sh_ref[...]
        hb = h.astype(BF16)
        h_scr[...] = hb
        og_ref[...] = jnp.dot(hb, wg_ref[...], preferred_element_type=F32)

    o_ref[...] = jnp.dot(h_scr[...], w_ref[...], preferred_element_type=F32).astype(o_ref.dtype)


def _inproj(x2, sc, sh, g, w_main, w_gate, seq, tm=512, tn=1536):
    t, d = x2.shape
    n = w_main.shape[1]
    tpb = seq // tm
    return pl.pallas_call(
        _inproj_kernel,
        grid=(t // tm, n // tn),
        in_specs=[pl.BlockSpec((tm, d), lambda i, j: (i, 0)),
                  pl.BlockSpec((None, 1, d), lambda i, j: (i // tpb, 0, 0)),
                  pl.BlockSpec((None, 1, d), lambda i, j: (i // tpb, 0, 0)),
                  pl.BlockSpec((1, d), lambda i, j: (0, 0)),
                  pl.BlockSpec((d, tn), lambda i, j: (0, j)),
                  pl.BlockSpec((d, LANES), lambda i, j: (0, 0))],
        out_specs=[pl.BlockSpec((tm, tn), lambda i, j: (i, j)),
                   pl.BlockSpec((tm, LANES), lambda i, j: (i, 0))],
        out_shape=[jax.ShapeDtypeStruct((t, n), BF16),
                   jax.ShapeDtypeStruct((t, LANES), F32)],
        scratch_shapes=[pltpu.VMEM((tm, d), BF16)],
        compiler_params=_cparams(("arbitrary", "arbitrary")),
        name="inproj",
    )(x2, sc, sh, g, w_main, w_gate)


_CONV_RC = 128
_CONV_HIST = 32


def _convmod_kernel(a_ref, gate_ref, w_ref, b_ref, lg_ref, lb_ref, o_ref, ext_scr, y_scr, *, width):
    tt, ch = a_ref.shape
    hist = _CONV_HIST
    rc = _CONV_RC

    @pl.when(pl.program_id(1) == 0)
    def _():
        ext_scr[0:hist, :] = jnp.zeros((hist, ch), F32)

    @pl.when(pl.program_id(1) != 0)
    def _():
        ext_scr[0:hist, :] = ext_scr[tt:tt + hist, :]

    ext_scr[hist:hist + tt, :] = a_ref[...].astype(F32) * _sigmoid(gate_ref[...].astype(F32))

    off0 = hist - (width - 1)
    nrow = rc + hist
    for r0 in range(0, tt, rc):
        for c0 in range(0, ch, LANES):
            blk = ext_scr[r0:r0 + nrow, c0:c0 + LANES]
            acc = jnp.zeros((rc, LANES), F32)
            for r in range(SUBLANES):
                taps = [j for j in range(width) if (off0 + j) % SUBLANES == r]
                if not taps:
                    continue
                rolled = blk if r == 0 else pltpu.roll(blk, nrow - r, axis=0)
                for j in taps:
                    a0 = off0 + j - r
                    acc = acc + w_ref[j:j + 1, c0:c0 + LANES] * rolled[a0:a0 + rc, :]
            y_scr[r0:r0 + rc, c0:c0 + LANES] = acc

    y = y_scr[...] + b_ref[...]
    mu = jnp.mean(y, axis=-1, keepdims=True)
    yc = y - mu
    var = jnp.mean(yc * yc, axis=-1, keepdims=True)
    z = yc * lax.rsqrt(var + EPS) * lg_ref[...] + lb_ref[...]
    o_ref[...] = _silu(z).astype(o_ref.dtype)


def _convmod(proj, conv_w, conv_b, ln_g, ln_b, bsz, seq, tt=512):
    width, ch = conv_w.shape
    assert width - 1 <= _CONV_HIST and tt % _CONV_RC == 0
    tpb = seq // tt
    vec = lambda: pl.BlockSpec((1, ch), lambda b, i: (0, 0))
    return pl.pallas_call(
        functools.partial(_convmod_kernel, width=width),
        grid=(bsz, tpb),
        in_specs=[pl.BlockSpec((tt, ch), lambda b, i: (b * tpb + i, 0)),
                  pl.BlockSpec((tt, ch), lambda b, i: (b * tpb + i, 1)),
                  pl.BlockSpec((width, ch), lambda b, i: (0, 0)),
                  vec(), vec(), vec()],
        out_specs=pl.BlockSpec((tt, ch), lambda b, i: (b * tpb + i, 0)),
        out_shape=jax.ShapeDtypeStruct((bsz * seq, ch), BF16),
        scratch_shapes=[pltpu.VMEM((tt + _CONV_HIST, ch), F32),
                        pltpu.VMEM((tt, ch), F32)],
        compiler_params=_cparams(("arbitrary", "arbitrary")),
        name="convmod",
    )(proj, proj, conv_w, conv_b.reshape(1, ch), ln_g.reshape(1, ch), ln_b.reshape(1, ch))


def _log_sigmoid(x):
    return -(jnp.maximum(-x, 0.0) + jnp.log(1.0 + jnp.exp(-jnp.abs(x))))


def _mlstm_kernel(q_ref, k_ref, v_ref, og_ref, gates_ref, gbias_ref, cw_ref, cb_ref, ng_ref,
                  o_ref, hist_scr, c_scr, n_scr, m_scr, *, dh, qk_width):
    ln = q_ref.shape[0]
    dm = q_ref.shape[1]
    nh = dm // dh
    hist = SUBLANES

    @pl.when(pl.program_id(1) == 0)
    def _():
        hist_scr[...] = jnp.zeros(hist_scr.shape, F32)
        c_scr[...] = jnp.zeros(c_scr.shape, F32)
        n_scr[...] = jnp.zeros(n_scr.shape, F32)
        m_scr[...] = jnp.full(m_scr.shape, STAB_INIT, F32)

    gts = gates_ref[...] + gbias_ref[...]
    lf = _log_sigmoid(gts)
    row = lax.broadcasted_iota(I32, (ln, ln), 0)
    col = lax.broadcasted_iota(I32, (ln, ln), 1)
    causal = row >= col
    tri = causal.astype(F32)
    bcum = jnp.dot(tri, lf, preferred_element_type=F32, precision=HIGHEST)
    bcum_t = bcum.T
    gts_t = gts.T

    def short_conv(raw_ref, hist_off, c0):
        raw = raw_ref[:, c0:c0 + dh].astype(F32)
        prev = hist_scr[:, hist_off + c0:hist_off + c0 + dh]
        ext = jnp.concatenate([prev, raw], axis=0)
        n = hist + ln
        acc = jnp.zeros((ln, dh), F32) + cb_ref[:, hist_off + c0:hist_off + c0 + dh]
        for j in range(qk_width):
            s = hist - (qk_width - 1) + j
            w = cw_ref[j:j + 1, hist_off + c0:hist_off + c0 + dh]
            if s % SUBLANES == 0:
                acc = acc + w * ext[s:s + ln, :]
            else:
                acc = acc + w * pltpu.roll(ext, n - s, axis=0)[0:ln, :]
        return _silu(acc)

    for h in range(nh):
        c0 = h * dh
        q = short_conv(q_ref, 0, c0)
        k = short_conv(k_ref, dm, c0) * (dh ** -0.5)
        qb = q.astype(BF16)
        kb = k.astype(BF16)
        v = v_ref[:, c0:c0 + dh]
        vf = v.astype(F32)

        b_col = bcum[:, nh + h:nh + h + 1]
        b_row = bcum_t[nh + h:nh + h + 1, :]
        ig_col = gts[:, h:h + 1]
        ig_row = gts_t[h:h + 1, :]
        m_prev = m_scr[h, 0:1, 0:1]

        dmat = jnp.where(causal, b_col - b_row + ig_row, NEG_INF)
        m_inter = b_col + m_prev
        m_t = jnp.maximum(m_inter, jnp.max(dmat, axis=1, keepdims=True))
        pmat = jnp.exp(dmat - m_t)
        qk = lax.dot_general(qb, kb, (((1,), (1,)), ((), ())), preferred_element_type=F32)
        s = qk * pmat
        a = jnp.exp(m_inter - m_t)
        inter = jnp.dot(qb, c_scr[h].astype(BF16), preferred_element_type=F32)
        num = jnp.dot(s.astype(BF16), v, preferred_element_type=F32) + a * inter
        den = jnp.sum(s, axis=1, keepdims=True) + a * jnp.sum(q * n_scr[h], axis=1, keepdims=True)
        hv = num / jnp.maximum(jnp.abs(den), jnp.exp(-m_t))

        mu = jnp.mean(hv, axis=-1, keepdims=True)
        hc = hv - mu
        var = jnp.mean(hc * hc, axis=-1, keepdims=True)
        hn = hc * lax.rsqrt(var + EPS) * ng_ref[:, c0:c0 + dh]
        og = og_ref[:, c0:c0 + dh].astype(F32)
        o_ref[:, c0:c0 + dh] = (_sigmoid(og) * hn).astype(o_ref.dtype)

        b_last = b_col[ln - 1:ln, :]
        g_col = b_last - b_col + ig_col
        m_new = jnp.maximum(b_last + m_prev, jnp.max(g_col, axis=0, keepdims=True))
        wk = jnp.exp(g_col - m_new)
        decay = jnp.exp(b_last + m_prev - m_new)
        upd = lax.dot_general(kb, (vf * wk).astype(BF16), (((0,), (0,)), ((), ())),
                              preferred_element_type=F32)
        c_scr[h] = decay * c_scr[h] + upd
        n_scr[h] = decay * n_scr[h] + jnp.sum(k * wk, axis=0, keepdims=True)
        m_scr[h] = jnp.broadcast_to(m_new, m_scr.shape[1:])

    hist_scr[:, 0:dm] = q_ref[ln - hist:ln, :].astype(F32)
    hist_scr[:, dm:2 * dm] = k_ref[ln - hist:ln, :].astype(F32)


def _mlstm(proj, gates, gbias, qk_conv_w, qk_conv_b, norm_g, bsz, seq, dm, dh, col0):
    ln = MLSTM_CHUNK
    nc = seq // ln
    nh = dm // dh
    width = qk_conv_w.shape[0]
    blk = lambda off: pl.BlockSpec((ln, dm), lambda b, c: (b * nc + c, col0 + off))
    return pl.pallas_call(
        functools.partial(_mlstm_kernel, dh=dh, qk_width=width),
        grid=(bsz, nc),
        in_specs=[blk(0), blk(1), blk(2), blk(3),
                  pl.BlockSpec((ln, LANES), lambda b, c: (b * nc + c, 0)),
                  pl.BlockSpec((1, LANES), lambda b, c: (0, 0)),
                  pl.BlockSpec((width, 2 * dm), lambda b, c: (0, 0)),
                  pl.BlockSpec((1, 2 * dm), lambda b, c: (0, 0)),
                  pl.BlockSpec((1, dm), lambda b, c: (0, 0))],
        out_specs=pl.BlockSpec((ln, dm), lambda b, c: (b * nc + c, 0)),
        out_shape=jax.ShapeDtypeStruct((bsz * seq, dm), BF16),
        scratch_shapes=[pltpu.VMEM((SUBLANES, 2 * dm), F32),
                        pltpu.VMEM((nh, dh, dh), F32),
                        pltpu.VMEM((nh, 1, dh), F32),
                        pltpu.VMEM((nh, SUBLANES, LANES), F32)],
        compiler_params=_cparams(("arbitrary", "arbitrary")),
        name="mlstm",
    )(proj, proj, proj, proj, gates, gbias, qk_conv_w, qk_conv_b.reshape(1, 2 * dm),
      norm_g.reshape(1, dm))


def _route_chunk(logits_t, bias_col, n_exp):
    gsz = n_exp // N_GROUPS
    scores = _sigmoid(logits_t)
    choice = scores + bias_col
    rows_g = lax.broadcasted_iota(I32, (gsz, LANES), 0).astype(F32)
    gscore = []
    for g in range(N_GROUPS):
        v = choice[g * gsz:(g + 1) * gsz, :]
        m1 = jnp.max(v, axis=0, keepdims=True)
        i1 = jnp.min(jnp.where(v == m1, rows_g, float(gsz)), axis=0, keepdims=True)
        m2 = jnp.max(jnp.where(rows_g == i1, NEG_INF, v), axis=0, keepdims=True)
        gscore.append(m1 + m2)
    gs = jnp.concatenate(gscore, axis=0)
    giota = lax.broadcasted_iota(I32, (N_GROUPS, LANES), 0).astype(F32)
    keep = jnp.zeros((N_GROUPS, LANES), F32)
    for _ in range(TOPK_GROUPS):
        m = jnp.max(gs, axis=0, keepdims=True)
        gi = jnp.min(jnp.where(gs == m, giota, float(N_GROUPS)), axis=0, keepdims=True)
        sel = giota == gi
        keep = jnp.where(sel, 1.0, keep)
        gs = jnp.where(sel, NEG_INF, gs)
    mc = jnp.concatenate(
        [jnp.where(keep[g:g + 1, :] > 0.5, choice[g * gsz:(g + 1) * gsz, :], NEG_INF)
         for g in range(N_GROUPS)], axis=0)
    rows = lax.broadcasted_iota(I32, (n_exp, LANES), 0).astype(F32)
    idxs, ws = [], []
    picked = jnp.zeros((n_exp, LANES), F32)
    for _ in range(TOP_K):
        m = jnp.max(mc, axis=0, keepdims=True)
        idx = jnp.min(jnp.where(mc == m, rows, float(n_exp)), axis=0, keepdims=True)
        sel = rows == idx
        ws.append(jnp.sum(jnp.where(sel, scores, 0.0), axis=0, keepdims=True))
        idxs.append(idx)
        picked = jnp.where(sel, 1.0, picked)
        mc = jnp.where(sel, NEG_INF, mc)
    w = jnp.concatenate(ws, axis=0)
    w = w / jnp.sum(w, axis=0, keepdims=True) * ROUTED_SCALE
    return jnp.concatenate(idxs, axis=0).astype(I32), w, picked


def _outproj_kernel(u_ref, hm_ref, x_ref, gt_ref, sc_ref, sh_ref, g_ref, w_ref, wr_ref, rb_ref,
                    x1_ref, hnp_ref, idx_ref, tw_ref, cnt_ref):
    tm, d = x_ref.shape
    dc = u_ref.shape[1]
    half = d // 2
    n_exp = wr_ref.shape[0]

    @pl.when(pl.program_id(0) == 0)
    def _():
        cnt_ref[...] = jnp.zeros(cnt_ref.shape, F32)

    mix = (jnp.dot(u_ref[...], w_ref[0:dc, :], preferred_element_type=F32)
           + jnp.dot(hm_ref[...], w_ref[dc:, :], preferred_element_type=F32))
    x1 = x_ref[...] + gt_ref[...] * mix
    x1_ref[...] = x1
    ms = jnp.mean(x1 * x1, axis=-1, keepdims=True)
    hn = (x1 * lax.rsqrt(ms + EPS) * g_ref[...]) * (1.0 + sc_ref[...]) + sh_ref[...]
    _store_rows(hnp_ref, _pack_halves(hn[:, :half], hn[:, half:]))
    logits_t = lax.dot_general(wr_ref[...], hn, (((1,), (1,)), ((), ())),
                               preferred_element_type=F32, precision=HIGHEST)
    cnt = cnt_ref[...]
    for c0 in range(0, tm, LANES):
        idx, w, picked = _route_chunk(logits_t[:, c0:c0 + LANES], rb_ref[...], n_exp)
        idx_ref[:, c0:c0 + LANES] = idx
        tw_ref[:, c0:c0 + LANES] = w
        cnt = cnt + picked
    cnt_ref[...] = cnt


def _outproj(u, hm, x2, gt1, sc2, sh2, g_ffn, w_out, w_router_t, router_bias, seq, tm=256):
    t, d = x2.shape
    dc = u.shape[1]
    n_exp = w_router_t.shape[0]
    tpb = seq // tm
    per_b = lambda: pl.BlockSpec((None, 1, d), lambda i: (i // tpb, 0, 0))
    return pl.pallas_call(
        _outproj_kernel,
        grid=(t // tm,),
        in_specs=[pl.BlockSpec((tm, dc), lambda i: (i, 0)),
                  pl.BlockSpec((tm, d - dc), lambda i: (i, 0)),
                  pl.BlockSpec((tm, d), lambda i: (i, 0)),
                  per_b(), per_b(), per_b(),
                  pl.BlockSpec((1, d), lambda i: (0, 0)),
                  pl.BlockSpec((d, d), lambda i: (0, 0)),
                  pl.BlockSpec((n_exp, d), lambda i: (0, 0)),
                  pl.BlockSpec((n_exp, 1), lambda i: (0, 0))],
        out_specs=[pl.BlockSpec((tm, d), lambda i: (i, 0)),
                   pl.BlockSpec((tm * (d // 2 // LANES), LANES), lambda i: (i, 0)),
                   pl.BlockSpec((TOP_K, tm), lambda i: (0, i)),
                   pl.BlockSpec((TOP_K, tm), lambda i: (0, i)),
                   pl.BlockSpec((n_exp, LANES), lambda i: (0, 0))],
        out_shape=[jax.ShapeDtypeStruct((t, d), F32),
                   jax.ShapeDtypeStruct((t * (d // 2 // LANES), LANES), U32),
                   jax.ShapeDtypeStruct((TOP_K, t), I32),
                   jax.ShapeDtypeStruct((TOP_K, t), F32),
                   jax.ShapeDtypeStruct((n_exp, LANES), F32)],
        compiler_params=_cparams(("arbitrary",)),
        name="outproj_route",
    )(u, hm, x2, gt1, sc2, sh2, g_ffn, w_out, w_router_t, router_bias.reshape(n_exp, 1))


def _plan_kernel(idx_ref, start_ref, dest_ref, run_scr):
    n_exp = start_ref.shape[0]
    tp = idx_ref.shape[1]

    @pl.when(pl.program_id(0) == 0)
    def _():
        run_scr[...] = start_ref[...]

    rows = lax.broadcasted_iota(I32, (n_exp, tp), 0)
    earlier = (lax.broadcasted_iota(I32, (tp, tp), 0)
               < lax.broadcasted_iota(I32, (tp, tp), 1)).astype(BF16)
    base = run_scr[...]
    for k in range(TOP_K):
        oh = rows == idx_ref[k:k + 1, :]
        ohf = oh.astype(F32)
        pref = jnp.dot(ohf.astype(BF16), earlier, preferred_element_type=F32)
        dest = jnp.sum(jnp.where(oh, pref + base, 0.0), axis=0, keepdims=True)
        dest_ref[k:k + 1, :] = dest.astype(I32)
        base = base + jnp.sum(ohf, axis=1, keepdims=True)
    run_scr[...] = base


def _plan(top_idx_t, starts_col, tp=512):
    k, t = top_idx_t.shape
    n_exp = starts_col.shape[0]
    return pl.pallas_call(
        _plan_kernel,
        grid=(t // tp,),
        in_specs=[pl.BlockSpec((k, tp), lambda i: (0, i)),
                  pl.BlockSpec((n_exp, 1), lambda i: (0, 0))],
        out_specs=pl.BlockSpec((k, tp), lambda i: (0, i)),
        out_shape=jax.ShapeDtypeStruct((k, t), I32),
        scratch_shapes=[pltpu.VMEM((n_exp, 1), F32)],
        compiler_params=_cparams(("arbitrary",)),
        name="plan",
    )(top_idx_t, starts_col)


def _dispatch_kernel(dest_hbm, hnp_ref, xs_hbm, idx_s, zero_scr, sem_i, sem_r, *, spr):
    i = pl.program_id(0)
    nt = pl.num_programs(0)
    td = hnp_ref.shape[0] // spr
    n_pick = idx_s.shape[1]

    def idx_copy(j):
        return pltpu.make_async_copy(dest_hbm.at[:, pl.ds(j * td, td)], idx_s.at[j % 2], sem_i.at[j % 2])

    @pl.when(i == 0)
    def _():
        idx_copy(0).start()

    @pl.when(i + 1 < nt)
    def _():
        idx_copy(i + 1).start()

    idx_copy(i).wait()
    slot = i % 2

    def body(t, carry):
        src = _row_slab(hnp_ref, t, spr)
        for k in range(n_pick):
            dst = idx_s[slot, k, t]
            pltpu.make_async_copy(src, _row_slab(xs_hbm, dst, spr), sem_r.at[0]).start(priority=k % 2)
        return carry

    lax.fori_loop(0, td, body, 0)

    for _ in range(n_pick):
        pltpu.make_async_copy(hnp_ref, _row_slab(xs_hbm, 0, spr, td), sem_r.at[0]).wait()

    @pl.when(i == nt - 1)
    def _():
        zero_scr[...] = jnp.zeros(zero_scr.shape, zero_scr.dtype)
        nz = zero_scr.shape[0]
        tail = pltpu.make_async_copy(zero_scr, xs_hbm.at[pl.ds(xs_hbm.shape[0] - nz, nz), :], sem_r.at[0])
        tail.start()
        tail.wait()


def _dispatch(dest, hnp, n_rows, tail_rows, spr, td=256):
    t = hnp.shape[0] // spr
    k = dest.shape[0]
    return pl.pallas_call(
        functools.partial(_dispatch_kernel, spr=spr),
        grid=(t // td,),
        in_specs=[pl.BlockSpec(memory_space=pl.ANY),
                  pl.BlockSpec((td * spr, LANES), lambda i: (i, 0))],
        out_specs=pl.BlockSpec(memory_space=pl.ANY),
        out_shape=jax.ShapeDtypeStruct((n_rows * spr, LANES), U32),
        scratch_shapes=[pltpu.SMEM((2, k, td), I32),
                        pltpu.VMEM((tail_rows * spr, LANES), U32),
                        pltpu.SemaphoreType.DMA((2,)),
                        pltpu.SemaphoreType.DMA((1,))],
        compiler_params=_cparams(("arbitrary",)),
        name="dispatch",
    )(dest, hnp)


def _moe_kernel(start_ref, size_ref, bstart_ref, xs_hbm, wgu_hbm, wd_hbm, ys_hbm,
                xbuf, ybuf, wgu_f, wd_f, wgu_b, wd_b, sem_x, sem_y, sem_w, *, n_assign):
    e = pl.program_id(0)
    n_exp = pl.num_programs(0)
    half = wgu_b.shape[0] // 2
    spr = half // LANES
    mb = ybuf.shape[0] // spr
    de = wd_b.shape[0]

    def x_copy(off, slot):
        return pltpu.make_async_copy(_row_slab(xs_hbm, off, spr, mb), xbuf.at[slot], sem_x.at[slot])

    def y_copy(off):
        return pltpu.make_async_copy(ybuf, _row_slab(ys_hbm, off, spr, mb), sem_y.at[0])

    def w_copies(j):
        slot = j % 2
        return (pltpu.make_async_copy(wgu_hbm.at[j], wgu_f.at[slot], sem_w.at[0, slot]),
                pltpu.make_async_copy(wd_hbm.at[j], wd_f.at[slot], sem_w.at[1, slot]))

    @pl.when(e == 0)
    def _():
        for cp in w_copies(0):
            cp.start(priority=1)
        x_copy(0, 0).start()

    @pl.when(e + 1 < n_exp)
    def _():
        for cp in w_copies(e + 1):
            cp.start(priority=1)

    for cp in w_copies(e):
        cp.wait()

    st = start_ref[e]
    sz = size_ref[e]
    nb = (sz + mb - 1) // mb
    b0 = bstart_ref[e]

    @pl.when(nb > 0)
    def _():
        wslot = e % 2
        wgu_b[...] = wgu_f[wslot].astype(BF16)
        wd_b[...] = wd_f[wslot].astype(BF16)

        def body(i, carry):
            b = b0 + i
            off = st + i * mb
            more = i + 1 < nb
            nxt = jnp.where(more, off + mb, st + sz)

            @pl.when(jnp.logical_or(more, nxt < n_assign))
            def _():
                x_copy(nxt, (b + 1) % 2).start()

            x_copy(off, b % 2).wait()
            lo, hi = _unpack_halves(_load_rows(xbuf.at[b % 2], spr))
            gu = (jnp.dot(lo.astype(BF16), wgu_b[0:half, :], preferred_element_type=F32)
                  + jnp.dot(hi.astype(BF16), wgu_b[half:, :], preferred_element_type=F32))
            act = _silu(gu[:, :de]) * gu[:, de:]
            y = jnp.dot(act.astype(BF16), wd_b[...], preferred_element_type=F32)

            @pl.when(b > 0)
            def _():
                y_copy(0).wait()

            _store_rows(ybuf, _pack_halves(y[:, :half], y[:, half:]))
            y_copy(off).start()
            return carry

        lax.fori_loop(0, nb, body, 0)

    @pl.when(e == n_exp - 1)
    def _():
        y_copy(0).wait()
        tail = y_copy(ys_hbm.shape[0] // spr - mb)
        tail.start()
        tail.wait()


def _moe(xs, starts, sizes, bstart, w_gate_up, w_down, mb, n_assign):
    n_exp, d, de2 = w_gate_up.shape
    s = d // 2 // LANES
    de = w_down.shape[1]
    grid_spec = pltpu.PrefetchScalarGridSpec(
        num_scalar_prefetch=3,
        grid=(n_exp,),
        in_specs=[pl.BlockSpec(memory_space=pl.ANY),
                  pl.BlockSpec(memory_space=pl.ANY),
                  pl.BlockSpec(memory_space=pl.ANY)],
        out_specs=pl.BlockSpec(memory_space=pl.ANY),
        scratch_shapes=[pltpu.VMEM((2, mb * s, LANES), U32),
                        pltpu.VMEM((mb * s, LANES), U32),
                        pltpu.VMEM((2, d, de2), F32),
                        pltpu.VMEM((2, de, d), F32),
                        pltpu.VMEM((d, de2), BF16),
                        pltpu.VMEM((de, d), BF16),
                        pltpu.SemaphoreType.DMA((2,)),
                        pltpu.SemaphoreType.DMA((1,)),
                        pltpu.SemaphoreType.DMA((2, 2))])
    return pl.pallas_call(
        functools.partial(_moe_kernel, n_assign=n_assign),
        grid_spec=grid_spec,
        out_shape=jax.ShapeDtypeStruct(xs.shape, U32),
        compiler_params=_cparams(("arbitrary",)),
        name="moe",
    )(starts, sizes, bstart, xs, w_gate_up, w_down)


def _combine_kernel(dest_hbm, ys_hbm, tw_ref, x1_ref, hnp_ref, gt_ref, gf_ref, wgu_ref, wd_ref,
                    o_ref, idx_s, gbuf, sem_i, sem_g):
    i = pl.program_id(0)
    nt = pl.num_programs(0)
    tc, d = x1_ref.shape
    half = d // 2
    spr = half // LANES
    de = wd_ref.shape[0]
    n_pick = tw_ref.shape[0]

    def idx_copy(j):
        return pltpu.make_async_copy(dest_hbm.at[:, pl.ds(j * tc, tc)], idx_s.at[j % 3], sem_i.at[j % 3])

    def gather_start(j):
        islot = j % 3
        gslot = j % 2

        def body(t, carry):
            for k in range(n_pick):
                src = idx_s[islot, k, t]
                pltpu.make_async_copy(_row_slab(ys_hbm, src, spr), _row_slab(gbuf.at[gslot, k], t, spr),
                                      sem_g.at[gslot]).start(priority=k % 2)
            return carry

        lax.fori_loop(0, tc, body, 0)

    def gather_wait(j):
        gslot = j % 2
        for k in range(n_pick):
            pltpu.make_async_copy(_row_slab(ys_hbm, 0, spr, tc), gbuf.at[gslot, k], sem_g.at[gslot]).wait()

    @pl.when(i == 0)
    def _():
        idx_copy(0).start()
        idx_copy(0).wait()
        gather_start(0)

        @pl.when(nt > 1)
        def _():
            idx_copy(1).start()

    @pl.when(i + 2 < nt)
    def _():
        idx_copy(i + 2).start()

    @pl.when(i + 1 < nt)
    def _():
        idx_copy(i + 1).wait()
        gather_start(i + 1)

    lo, hi = _unpack_halves(_load_rows(hnp_ref, spr))
    gu = (jnp.dot(lo.astype(BF16), wgu_ref[0:half, :], preferred_element_type=F32)
          + jnp.dot(hi.astype(BF16), wgu_ref[half:, :], preferred_element_type=F32))
    act = _silu(gu[:, :de]) * gu[:, de:]
    y = jnp.dot(act.astype(BF16), wd_ref[...], preferred_element_type=F32)

    tw = jnp.concatenate([tw_ref[...], jnp.zeros((LANES - n_pick, tc), F32)], axis=0)
    wcol = jnp.concatenate([tw[:, c0:c0 + LANES].T for c0 in range(0, tc, LANES)], axis=0)

    gather_wait(i)
    gslot = i % 2
    acc_lo = y[:, :half]
    acc_hi = y[:, half:]
    for k in range(n_pick):
        rlo, rhi = _unpack_halves(_load_rows(gbuf.at[gslot, k], spr))
        wk = wcol[:, k:k + 1]
        acc_lo = acc_lo + wk * rlo
        acc_hi = acc_hi + wk * rhi
    yy = jnp.concatenate([acc_lo, acc_hi], axis=1)
    x2 = x1_ref[...] + gt_ref[...] * yy
    ms = jnp.mean(x2 * x2, axis=-1, keepdims=True)
    o_ref[...] = x2 * lax.rsqrt(ms + EPS) * gf_ref[...]


def _combine(dest, ys, top_w_t, x1, hnp, gt2, g_final, ws_gu, ws_d, seq, tc=128):
    t, d = x1.shape
    half = d // 2
    k = dest.shape[0]
    tpb = seq // tc
    de2 = ws_gu.shape[1]
    return pl.pallas_call(
        _combine_kernel,
        grid=(t // tc,),
        in_specs=[pl.BlockSpec(memory_space=pl.ANY),
                  pl.BlockSpec(memory_space=pl.ANY),
                  pl.BlockSpec((k, tc), lambda i: (0, i)),
                  pl.BlockSpec((tc, d), lambda i: (i, 0)),
                  pl.BlockSpec((tc * (half // LANES), LANES), lambda i: (i, 0)),
                  pl.BlockSpec((None, 1, d), lambda i: (i // tpb, 0, 0)),
                  pl.BlockSpec((1, d), lambda i: (0, 0)),
                  pl.BlockSpec((d, de2), lambda i: (0, 0)),
                  pl.BlockSpec((de2 // 2, d), lambda i: (0, 0))],
        out_specs=pl.BlockSpec((tc, d), lambda i: (i, 0)),
        out_shape=jax.ShapeDtypeStruct((t, d), F32),
        scratch_shapes=[pltpu.SMEM((3, k, tc), I32),
                        pltpu.VMEM((2, k, tc * (half // LANES), LANES), U32),
                        pltpu.SemaphoreType.DMA((3,)),
                        pltpu.SemaphoreType.DMA((2,))],
        compiler_params=_cparams(("arbitrary",)),
        name="combine",
    )(dest, ys, top_w_t, x1, hnp, gt2, g_final.reshape(1, d), ws_gu, ws_d)


def kernel(x, c, w_ada, b_ada, g_mix, w_in, conv_w, conv_b, conv_ln_g, conv_ln_b, qk_conv_w, qk_conv_b, b_igate, b_fgate, mlstm_norm_g, w_out, g_ffn, w_router, router_bias, w_gate_up, w_down, ws_gate_up, ws_down, g_final):
    bsz, seq, d = x.shape
    depth = w_ada.shape[0]
    assert depth == 1, "single-layer block"
    dc = conv_w.shape[2]
    dm = d - dc
    nh = b_igate.shape[1]
    dh = dm // nh
    n_exp = w_router.shape[2]
    t = bsz * seq
    n_main = 2 * dc + 4 * dm
    n_assign = TOP_K * t
    mb = MOE_BLOCK
    assert dc == dm and nh == N_HEADS and 2 * nh <= LANES and n_assign % mb == 0

    mod = _ada(c, w_ada[0], b_ada[0])
    sh1, sc1, gt1, sh2, sc2, gt2 = [m.reshape(bsz, 1, d) for m in jnp.split(mod, 6, axis=-1)]

    x2 = x.reshape(t, d)
    w_main = w_in[0, :, :n_main].astype(BF16)
    w_gate = jnp.zeros((d, LANES), BF16).at[:, :2 * nh].set(w_in[0, :, n_main:].astype(BF16))
    proj, gates = _inproj(x2, sc1, sh1, g_mix[0].reshape(1, d), w_main, w_gate, seq)

    u = _convmod(proj, conv_w[0], conv_b[0], conv_ln_g[0], conv_ln_b[0], bsz, seq)

    gbias = jnp.zeros((1, LANES), F32).at[0, :nh].set(b_igate[0]).at[0, nh:2 * nh].set(b_fgate[0])
    hm = _mlstm(proj, gates, gbias, qk_conv_w[0], qk_conv_b[0], mlstm_norm_g[0],
                bsz, seq, dm, dh, col0=2 * dc // dm)

    x1, hnp, top_idx_t, top_w_t, cnt = _outproj(
        u, hm, x2, gt1, sc2, sh2, g_ffn[0].reshape(1, d), w_out[0].astype(BF16),
        w_router[0].T, router_bias[0], seq)

    sizes = jnp.sum(cnt, axis=1).astype(I32)
    starts = jnp.cumsum(sizes) - sizes
    nblk = (sizes + mb - 1) // mb
    bstart = jnp.cumsum(nblk) - nblk
    dest = _plan(top_idx_t, starts.astype(F32).reshape(n_exp, 1))
    xs = _dispatch(dest, hnp, n_assign + mb, mb, d // 2 // LANES)
    ys = _moe(xs, starts, sizes, bstart, w_gate_up[0], w_down[0], mb, n_assign)

    out = _combine(dest, ys, top_w_t, x1, hnp, gt2, g_final,
                   ws_gate_up[0].astype(BF16), ws_down[0].astype(BF16), seq)
    return out.reshape(bsz, seq, d)
```

```python
import functools

import jax
import jax.numpy as jnp
from jax import lax
from jax.experimental import pallas as pl
from jax.experimental.pallas import tpu as pltpu

F32 = jnp.float32
BF16 = jnp.bfloat16
U32 = jnp.uint32
I32 = jnp.int32

EPS = 1e-6
STAB_INIT = -1e30
TOP_K = 8
N_GROUPS = 8
TOPK_GROUPS = 4
ROUTED_SCALE = 2.5
N_HEADS = 4

LANES = 128
SUBLANES = 8
VMEM_LIMIT = 56 * 1024 * 1024

MLSTM_CHUNK = 256
MOE_BLOCK = 128
NEG_INF = float("-inf")
HIGHEST = lax.Precision.HIGHEST


def _cparams(sem):
    return pltpu.CompilerParams(dimension_semantics=sem, vmem_limit_bytes=VMEM_LIMIT)


def _sigmoid(x):
    return 1.0 / (1.0 + jnp.exp(-x))


def _silu(x):
    return x * _sigmoid(x)


def _pack_halves(lo, hi):
    lo_b = lax.bitcast_convert_type(lo.astype(BF16).astype(F32), U32)
    hi_b = lax.bitcast_convert_type(hi.astype(BF16).astype(F32), U32)
    return (hi_b & jnp.uint32(0xFFFF0000)) | (lo_b >> 16)


def _unpack_halves(w):
    lo = lax.bitcast_convert_type(w << 16, F32)
    hi = lax.bitcast_convert_type(w & jnp.uint32(0xFFFF0000), F32)
    return lo, hi


def _row_slab(ref, r, s_per_row, n=1):
    return ref.at[pl.ds(pl.multiple_of(r * s_per_row, s_per_row), n * s_per_row), :]


def _store_rows(ref, words):
    m = words.shape[0]
    s_per_row = words.shape[1] // LANES
    for s in range(s_per_row):
        ref[pl.ds(s, m, stride=s_per_row), :] = words[:, s * LANES:(s + 1) * LANES]


def _load_rows(ref, s_per_row):
    m = ref.shape[0] // s_per_row
    return jnp.concatenate([ref[pl.ds(s, m, stride=s_per_row), :] for s in range(s_per_row)], axis=1)


def _ada_kernel(c_ref, w_ref, b_ref, o_ref):
    cs = _silu(c_ref[...])
    o_ref[...] = jnp.dot(cs, w_ref[...], preferred_element_type=F32, precision=HIGHEST) + b_ref[...]


def _ada(c, w_ada, b_ada):
    bsz, d = c.shape
    n = w_ada.shape[1]
    tn = 1536
    cp = jnp.zeros((SUBLANES, d), F32).at[:bsz].set(c)
    out = pl.pallas_call(
        _ada_kernel,
        grid=(n // tn,),
        in_specs=[pl.BlockSpec((SUBLANES, d), lambda j: (0, 0)),
                  pl.BlockSpec((d, tn), lambda j: (0, j)),
                  pl.BlockSpec((1, tn), lambda j: (0, j))],
        out_specs=pl.BlockSpec((SUBLANES, tn), lambda j: (0, j)),
        out_shape=jax.ShapeDtypeStruct((SUBLANES, n), F32),
        compiler_params=_cparams(("arbitrary",)),
        name="ada",
    )(cp, w_ada, b_ada.reshape(1, n))
    return out[:bsz]


_NORM_ROWS = 256


def _inproj_kernel(x_ref, sc_ref, sh_ref, g_ref, w_ref, wg_ref, o_ref, og_ref, h_scr):
    @pl.when(pl.program_id(1) == 0)
    def _():
        def rows(c, carry):
            r0 = pl.multiple_of(c * _NORM_ROWS, _NORM_ROWS)
            x = x_ref[pl.ds(r0, _NORM_ROWS), :]
            ms = jnp.mean(x * x, axis=-1, keepdims=True)
            h = (x * lax.rsqrt(ms + EPS) * g_ref[...]) * (1.0 + sc_ref[...]) + sh_ref[...]
            hb = h.astype(BF16)
            h_scr[pl.ds(r0, _NORM_ROWS), :] = hb
            og_ref[pl.ds(r0, _NORM_ROWS), :] = jnp.dot(hb, wg_ref[...], preferred_element_type=F32)
            return carry

        lax.fori_loop(0, x_ref.shape[0] // _NORM_ROWS, rows, 0)

    o_ref[...] = jnp.dot(h_scr[...], w_ref[...], preferred_element_type=F32).astype(o_ref.dtype)


def _inproj(x2, sc, sh, g, w_main, w_gate, seq, tm=1024, tn=1536):
    t, d = x2.shape
    n = w_main.shape[1]
    tpb = seq // tm
    return pl.pallas_call(
        _inproj_kernel,
        grid=(t // tm, n // tn),
        in_specs=[pl.BlockSpec((tm, d), lambda i, j: (i, 0)),
                  pl.BlockSpec((None, 1, d), lambda i, j: (i // tpb, 0, 0)),
                  pl.BlockSpec((None, 1, d), lambda i, j: (i // tpb, 0, 0)),
                  pl.BlockSpec((1, d), lambda i, j: (0, 0)),
                  pl.BlockSpec((d, tn), lambda i, j: (0, j)),
                  pl.BlockSpec((d, LANES), lambda i, j: (0, 0))],
        out_specs=[pl.BlockSpec((tm, tn), lambda i, j: (i, j)),
                   pl.BlockSpec((tm, LANES), lambda i, j: (i, 0))],
        out_shape=[jax.ShapeDtypeStruct((t, n), BF16),
                   jax.ShapeDtypeStruct((t, LANES), F32)],
        scratch_shapes=[pltpu.VMEM((tm, d), BF16)],
        compiler_params=_cparams(("arbitrary", "arbitrary")),
        name="inproj",
    )(x2, sc, sh, g, w_main, w_gate)


_CONV_RC = 128
_CONV_HIST = 32


def _convmod_kernel(a_ref, gate_ref, w_ref, b_ref, lg_ref, lb_ref, o_ref, ext_scr, y_scr, *, width):
    tt, ch = a_ref.shape
    hist = _CONV_HIST
    rc = _CONV_RC

    @pl.when(pl.program_id(1) == 0)
    def _():
        ext_scr[0:hist, :] = jnp.zeros((hist, ch), F32)

    @pl.when(pl.program_id(1) != 0)
    def _():
        ext_scr[0:hist, :] = ext_scr[tt:tt + hist, :]

    ext_scr[hist:hist + tt, :] = a_ref[...].astype(F32) * _sigmoid(gate_ref[...].astype(F32))

    off0 = hist - (width - 1)
    nrow = rc + hist
    for r0 in range(0, tt, rc):
        for c0 in range(0, ch, LANES):
            blk = ext_scr[r0:r0 + nrow, c0:c0 + LANES]
            acc = jnp.zeros((rc, LANES), F32)
            for r in range(SUBLANES):
                taps = [j for j in range(width) if (off0 + j) % SUBLANES == r]
                if not taps:
                    continue
                rolled = blk if r == 0 else pltpu.roll(blk, nrow - r, axis=0)
                for j in taps:
                    a0 = off0 + j - r
                    acc = acc + w_ref[j:j + 1, c0:c0 + LANES] * rolled[a0:a0 + rc, :]
            y_scr[r0:r0 + rc, c0:c0 + LANES] = acc

    y = y_scr[...] + b_ref[...]
    mu = jnp.mean(y, axis=-1, keepdims=True)
    yc = y - mu
    var = jnp.mean(yc * yc, axis=-1, keepdims=True)
    z = yc * lax.rsqrt(var + EPS) * lg_ref[...] + lb_ref[...]
    o_ref[...] = _silu(z).astype(o_ref.dtype)


def _convmod(proj, conv_w, conv_b, ln_g, ln_b, bsz, seq, tt=512):
    width, ch = conv_w.shape
    assert width - 1 <= _CONV_HIST and tt % _CONV_RC == 0
    tpb = seq // tt
    vec = lambda: pl.BlockSpec((1, ch), lambda b, i: (0, 0))
    return pl.pallas_call(
        functools.partial(_convmod_kernel, width=width),
        grid=(bsz, tpb),
        in_specs=[pl.BlockSpec((tt, ch), lambda b, i: (b * tpb + i, 0)),
                  pl.BlockSpec((tt, ch), lambda b, i: (b * tpb + i, 1)),
                  pl.BlockSpec((width, ch), lambda b, i: (0, 0)),
                  vec(), vec(), vec()],
        out_specs=pl.BlockSpec((tt, ch), lambda b, i: (b * tpb + i, 0)),
        out_shape=jax.ShapeDtypeStruct((bsz * seq, ch), BF16),
        scratch_shapes=[pltpu.VMEM((tt + _CONV_HIST, ch), F32),
                        pltpu.VMEM((tt, ch), F32)],
        compiler_params=_cparams(("arbitrary", "arbitrary")),
        name="convmod",
    )(proj, proj, conv_w, conv_b.reshape(1, ch), ln_g.reshape(1, ch), ln_b.reshape(1, ch))


def _log_sigmoid(x):
    return -(jnp.maximum(-x, 0.0) + jnp.log(1.0 + jnp.exp(-jnp.abs(x))))


def _mlstm_kernel(q_ref, k_ref, v_ref, og_ref, gates_ref, gbias_ref, cw_ref, cb_ref, ng_ref,
                  o_ref, hist_scr, c_scr, n_scr, m_scr, *, dh, qk_width):
    ln = q_ref.shape[0]
    dm = q_ref.shape[1]
    nh = dm // dh
    hist = SUBLANES

    @pl.when(pl.program_id(1) == 0)
    def _():
        hist_scr[...] = jnp.zeros(hist_scr.shape, F32)
        c_scr[...] = jnp.zeros(c_scr.shape, F32)
        n_scr[...] = jnp.zeros(n_scr.shape, F32)
        m_scr[...] = jnp.full(m_scr.shape, STAB_INIT, F32)

    gts = gates_ref[...] + gbias_ref[...]
    lf = _log_sigmoid(gts)
    row = lax.broadcasted_iota(I32, (ln, ln), 0)
    col = lax.broadcasted_iota(I32, (ln, ln), 1)
    causal = row >= col
    tri = causal.astype(F32)
    bcum = jnp.dot(tri, lf, preferred_element_type=F32, precision=HIGHEST)
    bcum_t = bcum.T
    gts_t = gts.T

    def short_conv(raw_ref, hist_off, c0):
        raw = raw_ref[:, c0:c0 + dh].astype(F32)
        prev = hist_scr[:, hist_off + c0:hist_off + c0 + dh]
        ext = jnp.concatenate([prev, raw], axis=0)
        n = hist + ln
        acc = jnp.zeros((ln, dh), F32) + cb_ref[:, hist_off + c0:hist_off + c0 + dh]
        for j in range(qk_width):
            s = hist - (qk_width - 1) + j
            w = cw_ref[j:j + 1, hist_off + c0:hist_off + c0 + dh]
            if s % SUBLANES == 0:
                acc = acc + w * ext[s:s + ln, :]
            else:
                acc = acc + w * pltpu.roll(ext, n - s, axis=0)[0:ln, :]
        return _silu(acc)

    for h in range(nh):
        c0 = h * dh
        q = short_conv(q_ref, 0, c0)
        k = short_conv(k_ref, dm, c0) * (dh ** -0.5)
        qb = q.astype(BF16)
        kb = k.astype(BF16)
        v = v_ref[:, c0:c0 + dh]
        vf = v.astype(F32)

        b_col = bcum[:, nh + h:nh + h + 1]
        b_row = bcum_t[nh + h:nh + h + 1, :]
        ig_col = gts[:, h:h + 1]
        ig_row = gts_t[h:h + 1, :]
        m_prev = m_scr[h, 0:1, 0:1]

        dmat = jnp.where(causal, b_col - b_row + ig_row, NEG_INF)
        m_inter = b_col + m_prev
        m_t = jnp.maximum(m_inter, jnp.max(dmat, axis=1, keepdims=True))
        pmat = jnp.exp(dmat - m_t)
        qk = lax.dot_general(qb, kb, (((1,), (1,)), ((), ())), preferred_element_type=F32)
        s = qk * pmat
        a = jnp.exp(m_inter - m_t)
        inter = jnp.dot(qb, c_scr[h].astype(BF16), preferred_element_type=F32)
        num = jnp.dot(s.astype(BF16), v, preferred_element_type=F32) + a * inter
        den = jnp.sum(s, axis=1, keepdims=True) + a * jnp.sum(q * n_scr[h], axis=1, keepdims=True)
        hv = num / jnp.maximum(jnp.abs(den), jnp.exp(-m_t))

        mu = jnp.mean(hv, axis=-1, keepdims=True)
        hc = hv - mu
        var = jnp.mean(hc * hc, axis=-1, keepdims=True)
        hn = hc * lax.rsqrt(var + EPS) * ng_ref[:, c0:c0 + dh]
        og = og_ref[:, c0:c0 + dh].astype(F32)
        o_ref[:, c0:c0 + dh] = (_sigmoid(og) * hn).astype(o_ref.dtype)

        b_last = b_col[ln - 1:ln, :]
        g_col = b_last - b_col + ig_col
        m_new = jnp.maximum(b_last + m_prev, jnp.max(g_col, axis=0, keepdims=True))
        wk = jnp.exp(g_col - m_new)
        decay = jnp.exp(b_last + m_prev - m_new)
        upd = lax.dot_general(kb, (vf * wk).astype(BF16), (((0,), (0,)), ((), ())),
                              preferred_element_type=F32)
        c_scr[h] = decay * c_scr[h] + upd
        n_scr[h] = decay * n_scr[h] + jnp.sum(k * wk, axis=0, keepdims=True)
        m_scr[h] = jnp.broadcast_to(m_new, m_scr.shape[1:])

    hist_scr[:, 0:dm] = q_ref[ln - hist:ln, :].astype(F32)
    hist_scr[:, dm:2 * dm] = k_ref[ln - hist:ln, :].astype(F32)


def _mlstm(proj, gates, gbias, qk_conv_w, qk_conv_b, norm_g, bsz, seq, dm, dh, col0):
    ln = MLSTM_CHUNK
    nc = seq // ln
    nh = dm // dh
    width = qk_conv_w.shape[0]
    blk = lambda off: pl.BlockSpec((ln, dm), lambda b, c: (b * nc + c, col0 + off))
    return pl.pallas_call(
        functools.partial(_mlstm_kernel, dh=dh, qk_width=width),
        grid=(bsz, nc),
        in_specs=[blk(0), blk(1), blk(2), blk(3),
                  pl.BlockSpec((ln, LANES), lambda b, c: (b * nc + c, 0)),
                  pl.BlockSpec((1, LANES), lambda b, c: (0, 0)),
                  pl.BlockSpec((width, 2 * dm), lambda b, c: (0, 0)),
                  pl.BlockSpec((1, 2 * dm), lambda b, c: (0, 0)),
                  pl.BlockSpec((1, dm), lambda b, c: (0, 0))],
        out_specs=pl.BlockSpec((ln, dm), lambda b, c: (b * nc + c, 0)),
        out_shape=jax.ShapeDtypeStruct((bsz * seq, dm), BF16),
        scratch_shapes=[pltpu.VMEM((SUBLANES, 2 * dm), F32),
                        pltpu.VMEM((nh, dh, dh), F32),
                        pltpu.VMEM((nh, 1, dh), F32),
                        pltpu.VMEM((nh, SUBLANES, LANES), F32)],
        compiler_params=_cparams(("arbitrary", "arbitrary")),
        name="mlstm",
    )(proj, proj, proj, proj, gates, gbias, qk_conv_w, qk_conv_b.reshape(1, 2 * dm),
      norm_g.reshape(1, dm))


def _route_chunk(logits_t, bias_col, n_exp):
    gsz = n_exp // N_GROUPS
    scores = _sigmoid(logits_t)
    choice = scores + bias_col
    rows_g = lax.broadcasted_iota(I32, (gsz, LANES), 0).astype(F32)
    gscore = []
    for g in range(N_GROUPS):
        v = choice[g * gsz:(g + 1) * gsz, :]
        m1 = jnp.max(v, axis=0, keepdims=True)
        i1 = jnp.min(jnp.where(v == m1, rows_g, float(gsz)), axis=0, keepdims=True)
        m2 = jnp.max(jnp.where(rows_g == i1, NEG_INF, v), axis=0, keepdims=True)
        gscore.append(m1 + m2)
    gs = jnp.concatenate(gscore, axis=0)
    giota = lax.broadcasted_iota(I32, (N_GROUPS, LANES), 0).astype(F32)
    keep = jnp.zeros((N_GROUPS, LANES), F32)
    for _ in range(TOPK_GROUPS):
        m = jnp.max(gs, axis=0, keepdims=True)
        gi = jnp.min(jnp.where(gs == m, giota, float(N_GROUPS)), axis=0, keepdims=True)
        sel = giota == gi
        keep = jnp.where(sel, 1.0, keep)
        gs = jnp.where(sel, NEG_INF, gs)
    mc = jnp.concatenate(
        [jnp.where(keep[g:g + 1, :] > 0.5, choice[g * gsz:(g + 1) * gsz, :], NEG_INF)
         for g in range(N_GROUPS)], axis=0)
    rows = lax.broadcasted_iota(I32, (n_exp, LANES), 0).astype(F32)
    idxs, ws = [], []
    picked = jnp.zeros((n_exp, LANES), F32)
    for _ in range(TOP_K):
        m = jnp.max(mc, axis=0, keepdims=True)
        idx = jnp.min(jnp.where(mc == m, rows, float(n_exp)), axis=0, keepdims=True)
        sel = rows == idx
        ws.append(jnp.sum(jnp.where(sel, scores, 0.0), axis=0, keepdims=True))
        idxs.append(idx)
        picked = jnp.where(sel, 1.0, picked)
        mc = jnp.where(sel, NEG_INF, mc)
    w = jnp.concatenate(ws, axis=0)
    w = w / jnp.sum(w, axis=0, keepdims=True) * ROUTED_SCALE
    return jnp.concatenate(idxs, axis=0).astype(I32), w, picked


def _outproj_kernel(u_ref, hm_ref, x_ref, gt_ref, sc_ref, sh_ref, g_ref, w_ref, wrh_ref, wrl_ref,
                    rb_ref, x1_ref, hnp_ref, idx_ref, tw_ref, cnt_ref):
    tm, d = x_ref.shape
    dc = u_ref.shape[1]
    half = d // 2
    n_exp = wrh_ref.shape[0]

    @pl.when(pl.program_id(0) == 0)
    def _():
        cnt_ref[...] = jnp.zeros(cnt_ref.shape, F32)

    mix = (jnp.dot(u_ref[...], w_ref[0:dc, :], preferred_element_type=F32)
           + jnp.dot(hm_ref[...], w_ref[dc:, :], preferred_element_type=F32))
    x1 = x_ref[...] + gt_ref[...] * mix
    x1_ref[...] = x1
    ms = jnp.mean(x1 * x1, axis=-1, keepdims=True)
    hn = (x1 * lax.rsqrt(ms + EPS) * g_ref[...]) * (1.0 + sc_ref[...]) + sh_ref[...]
    _store_rows(hnp_ref, _pack_halves(hn[:, :half], hn[:, half:]))
    hn_hi = hn.astype(BF16)
    hn_lo = (hn - hn_hi.astype(F32)).astype(BF16)
    nt_dot = lambda a, b: lax.dot_general(a, b, (((1,), (1,)), ((), ())), preferred_element_type=F32)
    logits_t = (nt_dot(wrh_ref[...], hn_hi) + nt_dot(wrh_ref[...], hn_lo)) + nt_dot(wrl_ref[...], hn_hi)
    cnt = cnt_ref[...]
    for c0 in range(0, tm, LANES):
        idx, w, picked = _route_chunk(logits_t[:, c0:c0 + LANES], rb_ref[...], n_exp)
        idx_ref[:, c0:c0 + LANES] = idx
        tw_ref[:, c0:c0 + LANES] = w
        cnt = cnt + picked
    cnt_ref[...] = cnt


def _outproj(u, hm, x2, gt1, sc2, sh2, g_ffn, w_out, w_router_t, router_bias, seq, tm=256):
    t, d = x2.shape
    dc = u.shape[1]
    n_exp = w_router_t.shape[0]
    tpb = seq // tm
    wr_hi = w_router_t.astype(BF16)
    wr_lo = (w_router_t - wr_hi.astype(F32)).astype(BF16)
    per_b = lambda: pl.BlockSpec((None, 1, d), lambda i: (i // tpb, 0, 0))
    return pl.pallas_call(
        _outproj_kernel,
        grid=(t // tm,),
        in_specs=[pl.BlockSpec((tm, dc), lambda i: (i, 0)),
                  pl.BlockSpec((tm, d - dc), lambda i: (i, 0)),
                  pl.BlockSpec((tm, d), lambda i: (i, 0)),
                  per_b(), per_b(), per_b(),
                  pl.BlockSpec((1, d), lambda i: (0, 0)),
                  pl.BlockSpec((d, d), lambda i: (0, 0)),
                  pl.BlockSpec((n_exp, d), lambda i: (0, 0)),
                  pl.BlockSpec((n_exp, d), lambda i: (0, 0)),
                  pl.BlockSpec((n_exp, 1), lambda i: (0, 0))],
        out_specs=[pl.BlockSpec((tm, d), lambda i: (i, 0)),
                   pl.BlockSpec((tm * (d // 2 // LANES), LANES), lambda i: (i, 0)),
                   pl.BlockSpec((TOP_K, tm), lambda i: (0, i)),
                   pl.BlockSpec((TOP_K, tm), lambda i: (0, i)),
                   pl.BlockSpec((n_exp, LANES), lambda i: (0, 0))],
        out_shape=[jax.ShapeDtypeStruct((t, d), F32),
                   jax.ShapeDtypeStruct((t * (d // 2 // LANES), LANES), U32),
                   jax.ShapeDtypeStruct((TOP_K, t), I32),
                   jax.ShapeDtypeStruct((TOP_K, t), F32),
                   jax.ShapeDtypeStruct((n_exp, LANES), F32)],
        compiler_params=_cparams(("arbitrary",)),
        name="outproj_route",
    )(u, hm, x2, gt1, sc2, sh2, g_ffn, w_out, wr_hi, wr_lo, router_bias.reshape(n_exp, 1))


def _plan_kernel(idx_ref, start_ref, dest_ref, run_scr):
    n_exp = start_ref.shape[0]
    tp = idx_ref.shape[1]

    @pl.when(pl.program_id(0) == 0)
    def _():
        run_scr[...] = start_ref[...]

    rows = lax.broadcasted_iota(I32, (n_exp, tp), 0)
    earlier = (lax.broadcasted_iota(I32, (tp, tp), 0)
               < lax.broadcasted_iota(I32, (tp, tp), 1)).astype(BF16)
    base = run_scr[...]
    for k in range(TOP_K):
        oh = rows == idx_ref[k:k + 1, :]
        ohf = oh.astype(F32)
        pref = jnp.dot(ohf.astype(BF16), earlier, preferred_element_type=F32)
        dest = jnp.sum(jnp.where(oh, pref + base, 0.0), axis=0, keepdims=True)
        dest_ref[k:k + 1, :] = dest.astype(I32)
        base = base + jnp.sum(ohf, axis=1, keepdims=True)
    run_scr[...] = base


def _plan(top_idx_t, starts_col, tp=512):
    k, t = top_idx_t.shape
    n_exp = starts_col.shape[0]
    return pl.pallas_call(
        _plan_kernel,
        grid=(t // tp,),
        in_specs=[pl.BlockSpec((k, tp), lambda i: (0, i)),
                  pl.BlockSpec((n_exp, 1), lambda i: (0, 0))],
        out_specs=pl.BlockSpec((k, tp), lambda i: (0, i)),
        out_shape=jax.ShapeDtypeStruct((k, t), I32),
        scratch_shapes=[pltpu.VMEM((n_exp, 1), F32)],
        compiler_params=_cparams(("arbitrary",)),
        name="plan",
    )(top_idx_t, starts_col)


def _dispatch_kernel(dest_hbm, hnp_ref, xs_hbm, idx_s, zero_scr, sem_i, sem_r, *, spr):
    i = pl.program_id(0)
    nt = pl.num_programs(0)
    td = hnp_ref.shape[0] // spr
    n_pick = idx_s.shape[1]

    def idx_copy(j):
        return pltpu.make_async_copy(dest_hbm.at[:, pl.ds(j * td, td)], idx_s.at[j % 2], sem_i.at[j % 2])

    @pl.when(i == 0)
    def _():
        idx_copy(0).start()

    @pl.when(i + 1 < nt)
    def _():
        idx_copy(i + 1).start()

    idx_copy(i).wait()
    slot = i % 2

    def body(t, carry):
        src = _row_slab(hnp_ref, t, spr)
        for k in range(n_pick):
            dst = idx_s[slot, k, t]
            pltpu.make_async_copy(src, _row_slab(xs_hbm, dst, spr), sem_r.at[0]).start(priority=k % 2)
        return carry

    lax.fori_loop(0, td, body, 0)

    for _ in range(n_pick):
        pltpu.make_async_copy(hnp_ref, _row_slab(xs_hbm, 0, spr, td), sem_r.at[0]).wait()

    @pl.when(i == nt - 1)
    def _():
        zero_scr[...] = jnp.zeros(zero_scr.shape, zero_scr.dtype)
        nz = zero_scr.shape[0]
        tail = pltpu.make_async_copy(zero_scr, xs_hbm.at[pl.ds(xs_hbm.shape[0] - nz, nz), :], sem_r.at[0])
        tail.start()
        tail.wait()


def _dispatch(dest, hnp, n_rows, tail_rows, spr, td=256):
    t = hnp.shape[0] // spr
    k = dest.shape[0]
    return pl.pallas_call(
        functools.partial(_dispatch_kernel, spr=spr),
        grid=(t // td,),
        in_specs=[pl.BlockSpec(memory_space=pl.ANY),
                  pl.BlockSpec((td * spr, LANES), lambda i: (i, 0))],
        out_specs=pl.BlockSpec(memory_space=pl.ANY),
        out_shape=jax.ShapeDtypeStruct((n_rows * spr, LANES), U32),
        scratch_shapes=[pltpu.SMEM((2, k, td), I32),
                        pltpu.VMEM((tail_rows * spr, LANES), U32),
                        pltpu.SemaphoreType.DMA((2,)),
                        pltpu.SemaphoreType.DMA((1,))],
        compiler_params=_cparams(("arbitrary",)),
        name="dispatch",
    )(dest, hnp)


def _moe_kernel(start_ref, size_ref, bstart_ref, xs_hbm, wgu_hbm, wd_hbm, ys_hbm,
                xbuf, ybuf, wgu_f, wd_f, wgu_b, wd_b, sem_x, sem_y, sem_w, *, n_assign):
    e = pl.program_id(0)
    n_exp = pl.num_programs(0)
    half = wgu_b.shape[0] // 2
    spr = half // LANES
    mb = ybuf.shape[0] // spr
    de = wd_b.shape[0]

    def x_copy(off, slot):
        return pltpu.make_async_copy(_row_slab(xs_hbm, off, spr, mb), xbuf.at[slot], sem_x.at[slot])

    def y_copy(off):
        return pltpu.make_async_copy(ybuf, _row_slab(ys_hbm, off, spr, mb), sem_y.at[0])

    def w_copies(j):
        slot = j % 2
        return (pltpu.make_async_copy(wgu_hbm.at[j], wgu_f.at[slot], sem_w.at[0, slot]),
                pltpu.make_async_copy(wd_hbm.at[j], wd_f.at[slot], sem_w.at[1, slot]))

    @pl.when(e == 0)
    def _():
        for cp in w_copies(0):
            cp.start(priority=1)
        x_copy(0, 0).start()

    @pl.when(e + 1 < n_exp)
    def _():
        for cp in w_copies(e + 1):
            cp.start(priority=1)

    for cp in w_copies(e):
        cp.wait()

    st = start_ref[e]
    sz = size_ref[e]
    nb = (sz + mb - 1) // mb
    b0 = bstart_ref[e]

    @pl.when(nb > 0)
    def _():
        wslot = e % 2
        wgu_b[...] = wgu_f[wslot].astype(BF16)
        wd_b[...] = wd_f[wslot].astype(BF16)

        def body(i, carry):
            b = b0 + i
            off = st + i * mb
            more = i + 1 < nb
            nxt = jnp.where(more, off + mb, st + sz)

            @pl.when(jnp.logical_or(more, nxt < n_assign))
            def _():
                x_copy(nxt, (b + 1) % 2).start()

            x_copy(off, b % 2).wait()
            lo, hi = _unpack_halves(_load_rows(xbuf.at[b % 2], spr))
            gu = (jnp.dot(lo.astype(BF16), wgu_b[0:half, :], preferred_element_type=F32)
                  + jnp.dot(hi.astype(BF16), wgu_b[half:, :], preferred_element_type=F32))
            act = _silu(gu[:, :de]) * gu[:, de:]
            y = jnp.dot(act.astype(BF16), wd_b[...], preferred_element_type=F32)

            @pl.when(b > 0)
            def _():
                y_copy(0).wait()

            _store_rows(ybuf, _pack_halves(y[:, :half], y[:, half:]))
            y_copy(off).start()
            return carry

        lax.fori_loop(0, nb, body, 0)

    @pl.when(e == n_exp - 1)
    def _():
        y_copy(0).wait()
        tail = y_copy(ys_hbm.shape[0] // spr - mb)
        tail.start()
        tail.wait()


def _moe(xs, starts, sizes, bstart, w_gate_up, w_down, mb, n_assign):
    n_exp, d, de2 = w_gate_up.shape
    s = d // 2 // LANES
    de = w_down.shape[1]
    grid_spec = pltpu.PrefetchScalarGridSpec(
        num_scalar_prefetch=3,
        grid=(n_exp,),
        in_specs=[pl.BlockSpec(memory_space=pl.ANY),
                  pl.BlockSpec(memory_space=pl.ANY),
                  pl.BlockSpec(memory_space=pl.ANY)],
        out_specs=pl.BlockSpec(memory_space=pl.ANY),
        scratch_shapes=[pltpu.VMEM((2, mb * s, LANES), U32),
                        pltpu.VMEM((mb * s, LANES), U32),
                        pltpu.VMEM((2, d, de2), F32),
                        pltpu.VMEM((2, de, d), F32),
                        pltpu.VMEM((d, de2), BF16),
                        pltpu.VMEM((de, d), BF16),
                        pltpu.SemaphoreType.DMA((2,)),
                        pltpu.SemaphoreType.DMA((1,)),
                        pltpu.SemaphoreType.DMA((2, 2))])
    return pl.pallas_call(
        functools.partial(_moe_kernel, n_assign=n_assign),
        grid_spec=grid_spec,
        out_shape=jax.ShapeDtypeStruct(xs.shape, U32),
        compiler_params=_cparams(("arbitrary",)),
        name="moe",
    )(starts, sizes, bstart, xs, w_gate_up, w_down)


def _combine_kernel(dest_hbm, ys_hbm, tw_ref, x1_ref, hnp_ref, gt_ref, gf_ref, wgu_ref, wd_ref,
                    o_ref, idx_s, gbuf, y_scr, w_scr, sem_i, sem_g):
    i = pl.program_id(0)
    nt = pl.num_programs(0)
    tc, d = x1_ref.shape
    half = d // 2
    spr = half // LANES
    de = wd_ref.shape[0]
    n_pick = tw_ref.shape[0]
    grp = SUBLANES

    def idx_copy(j):
        return pltpu.make_async_copy(dest_hbm.at[:, pl.ds(j * tc, tc)], idx_s.at[j % 3], sem_i.at[j % 3])

    def issue_rows(islot, gslot, t0):
        for j in range(grp):
            t = t0 + j
            for k in range(n_pick):
                src = idx_s[islot, k, t]
                pltpu.make_async_copy(_row_slab(ys_hbm, src, spr), _row_slab(gbuf.at[gslot, k], t, spr),
                                      sem_g.at[gslot]).start(priority=k % 2)

    def gather_wait(gslot):
        for k in range(n_pick):
            pltpu.make_async_copy(_row_slab(ys_hbm, 0, spr, tc), gbuf.at[gslot, k], sem_g.at[gslot]).wait()

    @pl.when(i == 0)
    def _():
        idx_copy(0).start()
        idx_copy(0).wait()

        def first(g, carry):
            issue_rows(0, 0, pl.multiple_of(g * grp, grp))
            return carry

        lax.fori_loop(0, tc // grp, first, 0)

        @pl.when(nt > 1)
        def _():
            idx_copy(1).start()

    @pl.when(i + 2 < nt)
    def _():
        idx_copy(i + 2).start()

    @pl.when(i + 1 < nt)
    def _():
        idx_copy(i + 1).wait()

    lo, hi = _unpack_halves(_load_rows(hnp_ref, spr))
    gu = (jnp.dot(lo.astype(BF16), wgu_ref[0:half, :], preferred_element_type=F32)
          + jnp.dot(hi.astype(BF16), wgu_ref[half:, :], preferred_element_type=F32))
    act = _silu(gu[:, :de]) * gu[:, de:]
    y_scr[...] = jnp.dot(act.astype(BF16), wd_ref[...], preferred_element_type=F32)

    tw = jnp.concatenate([tw_ref[...], jnp.zeros((LANES - n_pick, tc), F32)], axis=0)
    w_scr[...] = jnp.concatenate([tw[:, c0:c0 + LANES].T for c0 in range(0, tc, LANES)], axis=0)

    gslot = i % 2
    gather_wait(gslot)
    has_next = i + 1 < nt
    nxt_islot = jnp.where(has_next, (i + 1) % 3, i % 3)
    nxt_gslot = (i + 1) % 2
    gt = gt_ref[...]
    gf = gf_ref[...]

    def body(g, carry):
        t0 = pl.multiple_of(g * grp, grp)
        rows = pl.ds(t0, grp)
        acc_lo = y_scr[rows, 0:half]
        acc_hi = y_scr[rows, half:]
        wrow = w_scr[rows, :]
        for k in range(n_pick):
            src = gbuf.at[gslot, k]
            words = jnp.concatenate(
                [src[pl.ds(t0 * spr + s, grp, stride=spr), :] for s in range(spr)], axis=1)
            rlo, rhi = _unpack_halves(words)
            wk = wrow[:, k:k + 1]
            acc_lo = acc_lo + wk * rlo
            acc_hi = acc_hi + wk * rhi
        x2 = x1_ref[rows, :] + gt * jnp.concatenate([acc_lo, acc_hi], axis=1)
        ms = jnp.mean(x2 * x2, axis=-1, keepdims=True)
        o_ref[rows, :] = x2 * lax.rsqrt(ms + EPS) * gf
        issue_rows(nxt_islot, nxt_gslot, t0)
        return carry

    lax.fori_loop(0, tc // grp, body, 0)

    @pl.when(i == nt - 1)
    def _():
        gather_wait(nxt_gslot)


def _combine(dest, ys, top_w_t, x1, hnp, gt2, g_final, ws_gu, ws_d, seq, tc=128):
    t, d = x1.shape
    half = d // 2
    k = dest.shape[0]
    tpb = seq // tc
    de2 = ws_gu.shape[1]
    return pl.pallas_call(
        _combine_kernel,
        grid=(t // tc,),
        in_specs=[pl.BlockSpec(memory_space=pl.ANY),
                  pl.BlockSpec(memory_space=pl.ANY),
                  pl.BlockSpec((k, tc), lambda i: (0, i)),
                  pl.BlockSpec((tc, d), lambda i: (i, 0)),
                  pl.BlockSpec((tc * (half // LANES), LANES), lambda i: (i, 0)),
                  pl.BlockSpec((None, 1, d), lambda i: (i // tpb, 0, 0)),
                  pl.BlockSpec((1, d), lambda i: (0, 0)),
                  pl.BlockSpec((d, de2), lambda i: (0, 0)),
                  pl.BlockSpec((de2 // 2, d), lambda i: (0, 0))],
        out_specs=pl.BlockSpec((tc, d), lambda i: (i, 0)),
        out_shape=jax.ShapeDtypeStruct((t, d), F32),
        scratch_shapes=[pltpu.SMEM((3, k, tc), I32),
                        pltpu.VMEM((2, k, tc * (half // LANES), LANES), U32),
                        pltpu.VMEM((tc, d), F32),
                        pltpu.VMEM((tc, LANES), F32),
                        pltpu.SemaphoreType.DMA((3,)),
                        pltpu.SemaphoreType.DMA((2,))],
        compiler_params=_cparams(("arbitrary",)),
        name="combine",
    )(dest, ys, top_w_t, x1, hnp, gt2, g_final.reshape(1, d), ws_gu, ws_d)


def kernel(x, c, w_ada, b_ada, g_mix, w_in, conv_w, conv_b, conv_ln_g, conv_ln_b, qk_conv_w, qk_conv_b, b_igate, b_fgate, mlstm_norm_g, w_out, g_ffn, w_router, router_bias, w_gate_up, w_down, ws_gate_up, ws_down, g_final):
    bsz, seq, d = x.shape
    depth = w_ada.shape[0]
    assert depth == 1, "single-layer block"
    dc = conv_w.shape[2]
    dm = d - dc
    nh = b_igate.shape[1]
    dh = dm // nh
    n_exp = w_router.shape[2]
    t = bsz * seq
    n_main = 2 * dc + 4 * dm
    n_assign = TOP_K * t
    mb = MOE_BLOCK
    assert dc == dm and nh == N_HEADS and 2 * nh <= LANES and n_assign % mb == 0

    mod = _ada(c, w_ada[0], b_ada[0])
    sh1, sc1, gt1, sh2, sc2, gt2 = [m.reshape(bsz, 1, d) for m in jnp.split(mod, 6, axis=-1)]

    x2 = x.reshape(t, d)
    w_main = w_in[0, :, :n_main].astype(BF16)
    w_gate = jnp.zeros((d, LANES), BF16).at[:, :2 * nh].set(w_in[0, :, n_main:].astype(BF16))
    proj, gates = _inproj(x2, sc1, sh1, g_mix[0].reshape(1, d), w_main, w_gate, seq)

    u = _convmod(proj, conv_w[0], conv_b[0], conv_ln_g[0], conv_ln_b[0], bsz, seq)

    gbias = jnp.zeros((1, LANES), F32).at[0, :nh].set(b_igate[0]).at[0, nh:2 * nh].set(b_fgate[0])
    hm = _mlstm(proj, gates, gbias, qk_conv_w[0], qk_conv_b[0], mlstm_norm_g[0],
                bsz, seq, dm, dh, col0=2 * dc // dm)

    x1, hnp, top_idx_t, top_w_t, cnt = _outproj(
        u, hm, x2, gt1, sc2, sh2, g_ffn[0].reshape(1, d), w_out[0].astype(BF16),
        w_router[0].T, router_bias[0], seq)

    sizes = jnp.sum(cnt, axis=1).astype(I32)
    starts = jnp.cumsum(sizes) - sizes
    nblk = (sizes + mb - 1) // mb
    bstart = jnp.cumsum(nblk) - nblk
    dest = _plan(top_idx_t, starts.astype(F32).reshape(n_exp, 1))
    xs = _dispatch(dest, hnp, n_assign + mb, mb, d // 2 // LANES)
    ys = _moe(xs, starts, sizes, bstart, w_gate_up[0], w_down[0], mb, n_assign)

    out = _combine(dest, ys, top_w_t, x1, hnp, gt2, g_final,
                   ws_gate_up[0].astype(BF16), ws_down[0].astype(BF16), seq)
    return out.reshape(bsz, seq, d)
```

```python
import functools

import jax
import jax.numpy as jnp
from jax import lax
from jax.experimental import pallas as pl
from jax.experimental.pallas import tpu as pltpu

F32 = jnp.float32
BF16 = jnp.bfloat16
U32 = jnp.uint32
I32 = jnp.int32

EPS = 1e-6
STAB_INIT = -1e30
TOP_K = 8
N_GROUPS = 8
TOPK_GROUPS = 4
ROUTED_SCALE = 2.5
N_HEADS = 4

LANES = 128
SUBLANES = 8
VMEM_LIMIT = 56 * 1024 * 1024

MLSTM_CHUNK = 256
MOE_BLOCK = 128
NEG_INF = float("-inf")
HIGHEST = lax.Precision.HIGHEST


def _cparams(sem):
    return pltpu.CompilerParams(dimension_semantics=sem, vmem_limit_bytes=VMEM_LIMIT)


def _sigmoid(x):
    return 1.0 / (1.0 + jnp.exp(-x))


def _silu(x):
    return x * _sigmoid(x)


def _pack_halves(lo, hi):
    lo_b = lax.bitcast_convert_type(lo.astype(BF16).astype(F32), U32)
    hi_b = lax.bitcast_convert_type(hi.astype(BF16).astype(F32), U32)
    return (hi_b & jnp.uint32(0xFFFF0000)) | (lo_b >> 16)


def _unpack_halves(w):
    lo = lax.bitcast_convert_type(w << 16, F32)
    hi = lax.bitcast_convert_type(w & jnp.uint32(0xFFFF0000), F32)
    return lo, hi


def _row_slab(ref, r, s_per_row, n=1):
    return ref.at[pl.ds(pl.multiple_of(r * s_per_row, s_per_row), n * s_per_row), :]


def _store_rows(ref, words):
    m = words.shape[0]
    s_per_row = words.shape[1] // LANES
    for s in range(s_per_row):
        ref[pl.ds(s, m, stride=s_per_row), :] = words[:, s * LANES:(s + 1) * LANES]


def _load_rows(ref, s_per_row):
    m = ref.shape[0] // s_per_row
    return jnp.concatenate([ref[pl.ds(s, m, stride=s_per_row), :] for s in range(s_per_row)], axis=1)


def _ada_kernel(c_ref, w_ref, b_ref, o_ref):
    cs = _silu(c_ref[...])
    o_ref[...] = jnp.dot(cs, w_ref[...], preferred_element_type=F32, precision=HIGHEST) + b_ref[...]


def _ada(c, w_ada, b_ada):
    bsz, d = c.shape
    n = w_ada.shape[1]
    tn = 1536
    cp = jnp.zeros((SUBLANES, d), F32).at[:bsz].set(c)
    out = pl.pallas_call(
        _ada_kernel,
        grid=(n // tn,),
        in_specs=[pl.BlockSpec((SUBLANES, d), lambda j: (0, 0)),
                  pl.BlockSpec((d, tn), lambda j: (0, j)),
                  pl.BlockSpec((1, tn), lambda j: (0, j))],
        out_specs=pl.BlockSpec((SUBLANES, tn), lambda j: (0, j)),
        out_shape=jax.ShapeDtypeStruct((SUBLANES, n), F32),
        compiler_params=_cparams(("arbitrary",)),
        name="ada",
    )(cp, w_ada, b_ada.reshape(1, n))
    return out[:bsz]


_NORM_ROWS = 256


def _inproj_kernel(x_ref, sc_ref, sh_ref, g_ref, w_ref, wg_ref, o_ref, og_ref, h_scr):
    @pl.when(pl.program_id(1) == 0)
    def _():
        def rows(c, carry):
            r0 = pl.multiple_of(c * _NORM_ROWS, _NORM_ROWS)
            x = x_ref[pl.ds(r0, _NORM_ROWS), :]
            ms = jnp.mean(x * x, axis=-1, keepdims=True)
            h = (x * lax.rsqrt(ms + EPS) * g_ref[...]) * (1.0 + sc_ref[...]) + sh_ref[...]
            hb = h.astype(BF16)
            h_scr[pl.ds(r0, _NORM_ROWS), :] = hb
            og_ref[pl.ds(r0, _NORM_ROWS), :] = jnp.dot(hb, wg_ref[...], preferred_element_type=F32)
            return carry

        lax.fori_loop(0, x_ref.shape[0] // _NORM_ROWS, rows, 0)

    o_ref[...] = jnp.dot(h_scr[...], w_ref[...], preferred_element_type=F32).astype(o_ref.dtype)


def _inproj(x2, sc, sh, g, w_main, w_gate, seq, tm=1024, tn=1536):
    t, d = x2.shape
    n = w_main.shape[1]
    tpb = seq // tm
    return pl.pallas_call(
        _inproj_kernel,
        grid=(t // tm, n // tn),
        in_specs=[pl.BlockSpec((tm, d), lambda i, j: (i, 0)),
                  pl.BlockSpec((None, 1, d), lambda i, j: (i // tpb, 0, 0)),
                  pl.BlockSpec((None, 1, d), lambda i, j: (i // tpb, 0, 0)),
                  pl.BlockSpec((1, d), lambda i, j: (0, 0)),
                  pl.BlockSpec((d, tn), lambda i, j: (0, j)),
                  pl.BlockSpec((d, LANES), lambda i, j: (0, 0))],
        out_specs=[pl.BlockSpec((tm, tn), lambda i, j: (i, j)),
                   pl.BlockSpec((tm, LANES), lambda i, j: (i, 0))],
        out_shape=[jax.ShapeDtypeStruct((t, n), BF16),
                   jax.ShapeDtypeStruct((t, LANES), F32)],
        scratch_shapes=[pltpu.VMEM((tm, d), BF16)],
        compiler_params=_cparams(("arbitrary", "arbitrary")),
        name="inproj",
    )(x2, sc, sh, g, w_main, w_gate)


_CONV_RC = 128
_CONV_HIST = 32


def _convmod_kernel(a_ref, gate_ref, w_ref, b_ref, lg_ref, lb_ref, o_ref, ext_scr, y_scr, *, width):
    tt, ch = a_ref.shape
    hist = _CONV_HIST
    rc = _CONV_RC

    @pl.when(pl.program_id(1) == 0)
    def _():
        ext_scr[0:hist, :] = jnp.zeros((hist, ch), F32)

    @pl.when(pl.program_id(1) != 0)
    def _():
        ext_scr[0:hist, :] = ext_scr[tt:tt + hist, :]

    ext_scr[hist:hist + tt, :] = a_ref[...].astype(F32) * _sigmoid(gate_ref[...].astype(F32))

    off0 = hist - (width - 1)
    nrow = rc + hist
    for r0 in range(0, tt, rc):
        for c0 in range(0, ch, LANES):
            blk = ext_scr[r0:r0 + nrow, c0:c0 + LANES]
            acc = jnp.zeros((rc, LANES), F32)
            for r in range(SUBLANES):
                taps = [j for j in range(width) if (off0 + j) % SUBLANES == r]
                if not taps:
                    continue
                rolled = blk if r == 0 else pltpu.roll(blk, nrow - r, axis=0)
                for j in taps:
                    a0 = off0 + j - r
                    acc = acc + w_ref[j:j + 1, c0:c0 + LANES] * rolled[a0:a0 + rc, :]
            y_scr[r0:r0 + rc, c0:c0 + LANES] = acc

    y = y_scr[...] + b_ref[...]
    mu = jnp.mean(y, axis=-1, keepdims=True)
    yc = y - mu
    var = jnp.mean(yc * yc, axis=-1, keepdims=True)
    z = yc * lax.rsqrt(var + EPS) * lg_ref[...] + lb_ref[...]
    o_ref[...] = _silu(z).astype(o_ref.dtype)


def _convmod(proj, conv_w, conv_b, ln_g, ln_b, bsz, seq, tt=512):
    width, ch = conv_w.shape
    assert width - 1 <= _CONV_HIST and tt % _CONV_RC == 0
    tpb = seq // tt
    vec = lambda: pl.BlockSpec((1, ch), lambda b, i: (0, 0))
    return pl.pallas_call(
        functools.partial(_convmod_kernel, width=width),
        grid=(bsz, tpb),
        in_specs=[pl.BlockSpec((tt, ch), lambda b, i: (b * tpb + i, 0)),
                  pl.BlockSpec((tt, ch), lambda b, i: (b * tpb + i, 1)),
                  pl.BlockSpec((width, ch), lambda b, i: (0, 0)),
                  vec(), vec(), vec()],
        out_specs=pl.BlockSpec((tt, ch), lambda b, i: (b * tpb + i, 0)),
        out_shape=jax.ShapeDtypeStruct((bsz * seq, ch), BF16),
        scratch_shapes=[pltpu.VMEM((tt + _CONV_HIST, ch), F32),
                        pltpu.VMEM((tt, ch), F32)],
        compiler_params=_cparams(("arbitrary", "arbitrary")),
        name="convmod",
    )(proj, proj, conv_w, conv_b.reshape(1, ch), ln_g.reshape(1, ch), ln_b.reshape(1, ch))


def _log_sigmoid(x):
    return -(jnp.maximum(-x, 0.0) + jnp.log(1.0 + jnp.exp(-jnp.abs(x))))


def _mlstm_kernel(q_ref, k_ref, v_ref, og_ref, gates_ref, gbias_ref, cw_ref, cb_ref, ng_ref,
                  o_ref, hist_scr, c_scr, n_scr, m_scr, *, dh, qk_width):
    ln = q_ref.shape[0]
    dm = q_ref.shape[1]
    nh = dm // dh
    hist = SUBLANES

    @pl.when(pl.program_id(1) == 0)
    def _():
        hist_scr[...] = jnp.zeros(hist_scr.shape, F32)
        c_scr[...] = jnp.zeros(c_scr.shape, F32)
        n_scr[...] = jnp.zeros(n_scr.shape, F32)
        m_scr[...] = jnp.full(m_scr.shape, STAB_INIT, F32)

    gts = gates_ref[...] + gbias_ref[...]
    lf = _log_sigmoid(gts)
    row = lax.broadcasted_iota(I32, (ln, ln), 0)
    col = lax.broadcasted_iota(I32, (ln, ln), 1)
    causal = row >= col
    tri = causal.astype(F32)
    bcum = jnp.dot(tri, lf, preferred_element_type=F32, precision=HIGHEST)
    bcum_t = bcum.T
    gts_t = gts.T

    def short_conv(raw_ref, hist_off, c0):
        raw = raw_ref[:, c0:c0 + dh].astype(F32)
        prev = hist_scr[:, hist_off + c0:hist_off + c0 + dh]
        ext = jnp.concatenate([prev, raw], axis=0)
        n = hist + ln
        acc = jnp.zeros((ln, dh), F32) + cb_ref[:, hist_off + c0:hist_off + c0 + dh]
        for j in range(qk_width):
            s = hist - (qk_width - 1) + j
            w = cw_ref[j:j + 1, hist_off + c0:hist_off + c0 + dh]
            if s % SUBLANES == 0:
                acc = acc + w * ext[s:s + ln, :]
            else:
                acc = acc + w * pltpu.roll(ext, n - s, axis=0)[0:ln, :]
        return _silu(acc)

    for h in range(nh):
        c0 = h * dh
        q = short_conv(q_ref, 0, c0)
        k = short_conv(k_ref, dm, c0) * (dh ** -0.5)
        qb = q.astype(BF16)
        kb = k.astype(BF16)
        v = v_ref[:, c0:c0 + dh]
        vf = v.astype(F32)

        b_col = bcum[:, nh + h:nh + h + 1]
        b_row = bcum_t[nh + h:nh + h + 1, :]
        ig_col = gts[:, h:h + 1]
        ig_row = gts_t[h:h + 1, :]
        m_prev = m_scr[h, 0:1, 0:1]

        dmat = jnp.where(causal, b_col - b_row + ig_row, NEG_INF)
        m_inter = b_col + m_prev
        m_t = jnp.maximum(m_inter, jnp.max(dmat, axis=1, keepdims=True))
        pmat = jnp.exp(dmat - m_t)
        qk = lax.dot_general(qb, kb, (((1,), (1,)), ((), ())), preferred_element_type=F32)
        s = qk * pmat
        a = jnp.exp(m_inter - m_t)
        inter = jnp.dot(qb, c_scr[h].astype(BF16), preferred_element_type=F32)
        num = jnp.dot(s.astype(BF16), v, preferred_element_type=F32) + a * inter
        den = jnp.sum(s, axis=1, keepdims=True) + a * jnp.sum(q * n_scr[h], axis=1, keepdims=True)
        hv = num / jnp.maximum(jnp.abs(den), jnp.exp(-m_t))

        mu = jnp.mean(hv, axis=-1, keepdims=True)
        hc = hv - mu
        var = jnp.mean(hc * hc, axis=-1, keepdims=True)
        hn = hc * lax.rsqrt(var + EPS) * ng_ref[:, c0:c0 + dh]
        og = og_ref[:, c0:c0 + dh].astype(F32)
        o_ref[:, c0:c0 + dh] = (_sigmoid(og) * hn).astype(o_ref.dtype)

        b_last = b_col[ln - 1:ln, :]
        g_col = b_last - b_col + ig_col
        m_new = jnp.maximum(b_last + m_prev, jnp.max(g_col, axis=0, keepdims=True))
        wk = jnp.exp(g_col - m_new)
        decay = jnp.exp(b_last + m_prev - m_new)
        upd = lax.dot_general(kb, (vf * wk).astype(BF16), (((0,), (0,)), ((), ())),
                              preferred_element_type=F32)
        c_scr[h] = decay * c_scr[h] + upd
        n_scr[h] = decay * n_scr[h] + jnp.sum(k * wk, axis=0, keepdims=True)
        m_scr[h] = jnp.broadcast_to(m_new, m_scr.shape[1:])

    hist_scr[:, 0:dm] = q_ref[ln - hist:ln, :].astype(F32)
    hist_scr[:, dm:2 * dm] = k_ref[ln - hist:ln, :].astype(F32)


def _mlstm(proj, gates, gbias, qk_conv_w, qk_conv_b, norm_g, bsz, seq, dm, dh, col0):
    ln = MLSTM_CHUNK
    nc = seq // ln
    nh = dm // dh
    width = qk_conv_w.shape[0]
    blk = lambda off: pl.BlockSpec((ln, dm), lambda b, c: (b * nc + c, col0 + off))
    return pl.pallas_call(
        functools.partial(_mlstm_kernel, dh=dh, qk_width=width),
        grid=(bsz, nc),
        in_specs=[blk(0), blk(1), blk(2), blk(3),
                  pl.BlockSpec((ln, LANES), lambda b, c: (b * nc + c, 0)),
                  pl.BlockSpec((1, LANES), lambda b, c: (0, 0)),
                  pl.BlockSpec((width, 2 * dm), lambda b, c: (0, 0)),
                  pl.BlockSpec((1, 2 * dm), lambda b, c: (0, 0)),
                  pl.BlockSpec((1, dm), lambda b, c: (0, 0))],
        out_specs=pl.BlockSpec((ln, dm), lambda b, c: (b * nc + c, 0)),
        out_shape=jax.ShapeDtypeStruct((bsz * seq, dm), BF16),
        scratch_shapes=[pltpu.VMEM((SUBLANES, 2 * dm), F32),
                        pltpu.VMEM((nh, dh, dh), F32),
                        pltpu.VMEM((nh, 1, dh), F32),
                        pltpu.VMEM((nh, SUBLANES, LANES), F32)],
        compiler_params=_cparams(("arbitrary", "arbitrary")),
        name="mlstm",
    )(proj, proj, proj, proj, gates, gbias, qk_conv_w, qk_conv_b.reshape(1, 2 * dm),
      norm_g.reshape(1, dm))


def _route_chunk(logits_t, bias_col, n_exp):
    gsz = n_exp // N_GROUPS
    scores = _sigmoid(logits_t)
    choice = scores + bias_col
    rows_g = lax.broadcasted_iota(I32, (gsz, LANES), 0).astype(F32)
    gscore = []
    for g in range(N_GROUPS):
        v = choice[g * gsz:(g + 1) * gsz, :]
        m1 = jnp.max(v, axis=0, keepdims=True)
        i1 = jnp.min(jnp.where(v == m1, rows_g, float(gsz)), axis=0, keepdims=True)
        m2 = jnp.max(jnp.where(rows_g == i1, NEG_INF, v), axis=0, keepdims=True)
        gscore.append(m1 + m2)
    gs = jnp.concatenate(gscore, axis=0)
    giota = lax.broadcasted_iota(I32, (N_GROUPS, LANES), 0).astype(F32)
    keep = jnp.zeros((N_GROUPS, LANES), F32)
    for _ in range(TOPK_GROUPS):
        m = jnp.max(gs, axis=0, keepdims=True)
        gi = jnp.min(jnp.where(gs == m, giota, float(N_GROUPS)), axis=0, keepdims=True)
        sel = giota == gi
        keep = jnp.where(sel, 1.0, keep)
        gs = jnp.where(sel, NEG_INF, gs)
    mc = jnp.concatenate(
        [jnp.where(keep[g:g + 1, :] > 0.5, choice[g * gsz:(g + 1) * gsz, :], NEG_INF)
         for g in range(N_GROUPS)], axis=0)
    rows = lax.broadcasted_iota(I32, (n_exp, LANES), 0).astype(F32)
    idxs, ws = [], []
    picked = jnp.zeros((n_exp, LANES), F32)
    for _ in range(TOP_K):
        m = jnp.max(mc, axis=0, keepdims=True)
        idx = jnp.min(jnp.where(mc == m, rows, float(n_exp)), axis=0, keepdims=True)
        sel = rows == idx
        ws.append(jnp.sum(jnp.where(sel, scores, 0.0), axis=0, keepdims=True))
        idxs.append(idx)
        picked = jnp.where(sel, 1.0, picked)
        mc = jnp.where(sel, NEG_INF, mc)
    w = jnp.concatenate(ws, axis=0)
    w = w / jnp.sum(w, axis=0, keepdims=True) * ROUTED_SCALE
    return jnp.concatenate(idxs, axis=0).astype(I32), w, picked


def _outproj_kernel(u_ref, hm_ref, x_ref, gt_ref, sc_ref, sh_ref, g_ref, w_ref, wrh_ref, wrl_ref,
                    rb_ref, x1_ref, hnp_ref, idx_ref, tw_ref, cnt_ref):
    tm, d = x_ref.shape
    dc = u_ref.shape[1]
    half = d // 2
    n_exp = wrh_ref.shape[1]

    @pl.when(pl.program_id(0) == 0)
    def _():
        cnt_ref[...] = jnp.zeros(cnt_ref.shape, F32)

    mix = (jnp.dot(u_ref[...], w_ref[0:dc, :], preferred_element_type=F32)
           + jnp.dot(hm_ref[...], w_ref[dc:, :], preferred_element_type=F32))
    x1 = x_ref[...] + gt_ref[...] * mix
    x1_ref[...] = x1
    ms = jnp.mean(x1 * x1, axis=-1, keepdims=True)
    hn = (x1 * lax.rsqrt(ms + EPS) * g_ref[...]) * (1.0 + sc_ref[...]) + sh_ref[...]
    _store_rows(hnp_ref, _pack_halves(hn[:, :half], hn[:, half:]))
    hn_hi = hn.astype(BF16)
    hn_lo = (hn - hn_hi.astype(F32)).astype(BF16)
    dot = lambda a, b: jnp.dot(a, b, preferred_element_type=F32)
    logits = (dot(hn_hi, wrh_ref[...]) + dot(hn_lo, wrh_ref[...])) + dot(hn_hi, wrl_ref[...])
    logits_t = jnp.concatenate([logits[c0:c0 + LANES, :].T for c0 in range(0, tm, LANES)], axis=1)
    cnt = cnt_ref[...]
    for c0 in range(0, tm, LANES):
        idx, w, picked = _route_chunk(logits_t[:, c0:c0 + LANES], rb_ref[...], n_exp)
        idx_ref[:, c0:c0 + LANES] = idx
        tw_ref[:, c0:c0 + LANES] = w
        cnt = cnt + picked
    cnt_ref[...] = cnt


def _outproj(u, hm, x2, gt1, sc2, sh2, g_ffn, w_out, w_router, router_bias, seq, tm=256):
    t, d = x2.shape
    dc = u.shape[1]
    n_exp = w_router.shape[1]
    tpb = seq // tm
    wr_hi = w_router.astype(BF16)
    wr_lo = (w_router - wr_hi.astype(F32)).astype(BF16)
    per_b = lambda: pl.BlockSpec((None, 1, d), lambda i: (i // tpb, 0, 0))
    return pl.pallas_call(
        _outproj_kernel,
        grid=(t // tm,),
        in_specs=[pl.BlockSpec((tm, dc), lambda i: (i, 0)),
                  pl.BlockSpec((tm, d - dc), lambda i: (i, 0)),
                  pl.BlockSpec((tm, d), lambda i: (i, 0)),
                  per_b(), per_b(), per_b(),
                  pl.BlockSpec((1, d), lambda i: (0, 0)),
                  pl.BlockSpec((d, d), lambda i: (0, 0)),
                  pl.BlockSpec((d, n_exp), lambda i: (0, 0)),
                  pl.BlockSpec((d, n_exp), lambda i: (0, 0)),
                  pl.BlockSpec((n_exp, 1), lambda i: (0, 0))],
        out_specs=[pl.BlockSpec((tm, d), lambda i: (i, 0)),
                   pl.BlockSpec((tm * (d // 2 // LANES), LANES), lambda i: (i, 0)),
                   pl.BlockSpec((TOP_K, tm), lambda i: (0, i)),
                   pl.BlockSpec((TOP_K, tm), lambda i: (0, i)),
                   pl.BlockSpec((n_exp, LANES), lambda i: (0, 0))],
        out_shape=[jax.ShapeDtypeStruct((t, d), F32),
                   jax.ShapeDtypeStruct((t * (d // 2 // LANES), LANES), U32),
                   jax.ShapeDtypeStruct((TOP_K, t), I32),
                   jax.ShapeDtypeStruct((TOP_K, t), F32),
                   jax.ShapeDtypeStruct((n_exp, LANES), F32)],
        compiler_params=_cparams(("arbitrary",)),
        name="outproj_route",
    )(u, hm, x2, gt1, sc2, sh2, g_ffn, w_out, wr_hi, wr_lo, router_bias.reshape(n_exp, 1))


def _plan_kernel(idx_ref, start_ref, dest_ref, run_scr):
    n_exp = start_ref.shape[0]
    tp = idx_ref.shape[1]

    @pl.when(pl.program_id(0) == 0)
    def _():
        run_scr[...] = start_ref[...]

    rows = lax.broadcasted_iota(I32, (n_exp, tp), 0)
    earlier = (lax.broadcasted_iota(I32, (tp, tp), 0)
               < lax.broadcasted_iota(I32, (tp, tp), 1)).astype(BF16)
    base = run_scr[...]
    for k in range(TOP_K):
        oh = rows == idx_ref[k:k + 1, :]
        ohf = oh.astype(F32)
        pref = jnp.dot(ohf.astype(BF16), earlier, preferred_element_type=F32)
        dest = jnp.sum(jnp.where(oh, pref + base, 0.0), axis=0, keepdims=True)
        dest_ref[k:k + 1, :] = dest.astype(I32)
        base = base + jnp.sum(ohf, axis=1, keepdims=True)
    run_scr[...] = base


def _plan(top_idx_t, starts_col, tp=512):
    k, t = top_idx_t.shape
    n_exp = starts_col.shape[0]
    return pl.pallas_call(
        _plan_kernel,
        grid=(t // tp,),
        in_specs=[pl.BlockSpec((k, tp), lambda i: (0, i)),
                  pl.BlockSpec((n_exp, 1), lambda i: (0, 0))],
        out_specs=pl.BlockSpec((k, tp), lambda i: (0, i)),
        out_shape=jax.ShapeDtypeStruct((k, t), I32),
        scratch_shapes=[pltpu.VMEM((n_exp, 1), F32)],
        compiler_params=_cparams(("arbitrary",)),
        name="plan",
    )(top_idx_t, starts_col)


def _dispatch_kernel(dest_hbm, hnp_ref, xs_hbm, idx_s, zero_scr, sem_i, sem_r, *, spr):
    i = pl.program_id(0)
    nt = pl.num_programs(0)
    td = hnp_ref.shape[0] // spr
    n_pick = idx_s.shape[1]

    def idx_copy(j):
        return pltpu.make_async_copy(dest_hbm.at[:, pl.ds(j * td, td)], idx_s.at[j % 2], sem_i.at[j % 2])

    @pl.when(i == 0)
    def _():
        idx_copy(0).start()

    @pl.when(i + 1 < nt)
    def _():
        idx_copy(i + 1).start()

    idx_copy(i).wait()
    slot = i % 2

    def body(t, carry):
        src = _row_slab(hnp_ref, t, spr)
        for k in range(n_pick):
            dst = idx_s[slot, k, t]
            pltpu.make_async_copy(src, _row_slab(xs_hbm, dst, spr), sem_r.at[0]).start(priority=k % 2)
        return carry

    lax.fori_loop(0, td, body, 0)

    for _ in range(n_pick):
        pltpu.make_async_copy(hnp_ref, _row_slab(xs_hbm, 0, spr, td), sem_r.at[0]).wait()

    @pl.when(i == nt - 1)
    def _():
        zero_scr[...] = jnp.zeros(zero_scr.shape, zero_scr.dtype)
        nz = zero_scr.shape[0]
        tail = pltpu.make_async_copy(zero_scr, xs_hbm.at[pl.ds(xs_hbm.shape[0] - nz, nz), :], sem_r.at[0])
        tail.start()
        tail.wait()


def _dispatch(dest, hnp, n_rows, tail_rows, spr, td=256):
    t = hnp.shape[0] // spr
    k = dest.shape[0]
    return pl.pallas_call(
        functools.partial(_dispatch_kernel, spr=spr),
        grid=(t // td,),
        in_specs=[pl.BlockSpec(memory_space=pl.ANY),
                  pl.BlockSpec((td * spr, LANES), lambda i: (i, 0))],
        out_specs=pl.BlockSpec(memory_space=pl.ANY),
        out_shape=jax.ShapeDtypeStruct((n_rows * spr, LANES), U32),
        scratch_shapes=[pltpu.SMEM((2, k, td), I32),
                        pltpu.VMEM((tail_rows * spr, LANES), U32),
                        pltpu.SemaphoreType.DMA((2,)),
                        pltpu.SemaphoreType.DMA((1,))],
        compiler_params=_cparams(("arbitrary",)),
        name="dispatch",
    )(dest, hnp)


_X_SLOTS = 3


def _block_offsets(sizes, starts, mb, n_blocks):
    nblk = (sizes + mb - 1) // mb
    blk_end = jnp.cumsum(nblk)
    bstart = blk_end - nblk
    b = jnp.arange(n_blocks, dtype=I32)
    owner = (blk_end[None, :] > b[:, None]) & (bstart[None, :] <= b[:, None])
    base = jnp.sum(jnp.where(owner, (starts - bstart * mb)[None, :], 0), axis=1)
    off = jnp.where(b < blk_end[-1], base + b * mb, 0)
    return off.astype(I32), nblk.astype(I32), bstart.astype(I32), blk_end[-1:].astype(I32)


def _moe_kernel(off_ref, nblk_ref, bstart_ref, nu_ref, xs_hbm, wgu_hbm, wd_hbm, ys_hbm,
                xbuf, ybuf, wgu_f, wd_f, wgu_b, wd_b, sem_x, sem_y, sem_w):
    e = pl.program_id(0)
    n_exp = pl.num_programs(0)
    n_used = nu_ref[0]
    half = wgu_b.shape[0] // 2
    spr = half // LANES
    mb = ybuf.shape[1] // spr
    de = wd_b.shape[0]
    ahead = _X_SLOTS - 1

    def x_copy(b):
        slot = b % _X_SLOTS
        return pltpu.make_async_copy(_row_slab(xs_hbm, off_ref[b], spr, mb), xbuf.at[slot], sem_x.at[slot])

    def y_copy(off, slot):
        return pltpu.make_async_copy(ybuf.at[slot], _row_slab(ys_hbm, off, spr, mb), sem_y.at[slot])

    def w_copies(j):
        slot = j % 2
        return (pltpu.make_async_copy(wgu_hbm.at[j], wgu_f.at[slot], sem_w.at[0, slot]),
                pltpu.make_async_copy(wd_hbm.at[j], wd_f.at[slot], sem_w.at[1, slot]))

    @pl.when(e == 0)
    def _():
        for cp in w_copies(0):
            cp.start(priority=1)
        for j in range(ahead):
            pl.when(j < n_used)(x_copy(j).start)

    @pl.when(e + 1 < n_exp)
    def _():
        for cp in w_copies(e + 1):
            cp.start(priority=1)

    for cp in w_copies(e):
        cp.wait()

    nb = nblk_ref[e]
    b0 = bstart_ref[e]

    @pl.when(nb > 0)
    def _():
        wslot = e % 2
        wgu_b[...] = wgu_f[wslot].astype(BF16)
        wd_b[...] = wd_f[wslot].astype(BF16)

        def body(i, carry):
            b = b0 + i

            @pl.when(b + ahead < n_used)
            def _():
                x_copy(b + ahead).start()

            x_copy(b).wait()
            lo, hi = _unpack_halves(_load_rows(xbuf.at[b % _X_SLOTS], spr))
            gu = (jnp.dot(lo.astype(BF16), wgu_b[0:half, :], preferred_element_type=F32)
                  + jnp.dot(hi.astype(BF16), wgu_b[half:, :], preferred_element_type=F32))
            act = _silu(gu[:, :de]) * gu[:, de:]
            y = jnp.dot(act.astype(BF16), wd_b[...], preferred_element_type=F32)
            yslot = b % 2
            _store_rows(ybuf.at[yslot], _pack_halves(y[:, :half], y[:, half:]))

            @pl.when(b > 0)
            def _():
                y_copy(0, 1 - yslot).wait()

            y_copy(off_ref[b], yslot).start()
            return carry

        lax.fori_loop(0, nb, body, 0)

    @pl.when(e == n_exp - 1)
    def _():
        last = (n_used - 1) % 2
        y_copy(0, last).wait()
        tail = y_copy(ys_hbm.shape[0] // spr - mb, last)
        tail.start()
        tail.wait()


def _moe(xs, block_off, nblk, bstart, n_used, w_gate_up, w_down, mb):
    n_exp, d, de2 = w_gate_up.shape
    s = d // 2 // LANES
    de = w_down.shape[1]
    grid_spec = pltpu.PrefetchScalarGridSpec(
        num_scalar_prefetch=4,
        grid=(n_exp,),
        in_specs=[pl.BlockSpec(memory_space=pl.ANY),
                  pl.BlockSpec(memory_space=pl.ANY),
                  pl.BlockSpec(memory_space=pl.ANY)],
        out_specs=pl.BlockSpec(memory_space=pl.ANY),
        scratch_shapes=[pltpu.VMEM((_X_SLOTS, mb * s, LANES), U32),
                        pltpu.VMEM((2, mb * s, LANES), U32),
                        pltpu.VMEM((2, d, de2), F32),
                        pltpu.VMEM((2, de, d), F32),
                        pltpu.VMEM((d, de2), BF16),
                        pltpu.VMEM((de, d), BF16),
                        pltpu.SemaphoreType.DMA((_X_SLOTS,)),
                        pltpu.SemaphoreType.DMA((2,)),
                        pltpu.SemaphoreType.DMA((2, 2))])
    return pl.pallas_call(
        _moe_kernel,
        grid_spec=grid_spec,
        out_shape=jax.ShapeDtypeStruct(xs.shape, U32),
        compiler_params=_cparams(("arbitrary",)),
        name="moe",
    )(block_off, nblk, bstart, n_used, xs, w_gate_up, w_down)


def _combine_kernel(dest_hbm, ys_hbm, tw_ref, x1_ref, hnp_ref, gt_ref, gf_ref, wgu_ref, wd_ref,
                    o_ref, idx_s, gbuf, y_scr, w_scr, sem_i, sem_g):
    i = pl.program_id(0)
    nt = pl.num_programs(0)
    tc, d = x1_ref.shape
    half = d // 2
    spr = half // LANES
    de = wd_ref.shape[0]
    n_pick = tw_ref.shape[0]
    grp = SUBLANES

    def idx_copy(j):
        return pltpu.make_async_copy(dest_hbm.at[:, pl.ds(j * tc, tc)], idx_s.at[j % 3], sem_i.at[j % 3])

    def issue_rows(islot, gslot, t0):
        for j in range(grp):
            t = t0 + j
            for k in range(n_pick):
                src = idx_s[islot, k, t]
                pltpu.make_async_copy(_row_slab(ys_hbm, src, spr), _row_slab(gbuf.at[gslot, k], t, spr),
                                      sem_g.at[gslot]).start(priority=k % 2)

    def gather_wait(gslot):
        for k in range(n_pick):
            pltpu.make_async_copy(_row_slab(ys_hbm, 0, spr, tc), gbuf.at[gslot, k], sem_g.at[gslot]).wait()

    @pl.when(i == 0)
    def _():
        idx_copy(0).start()
        idx_copy(0).wait()

        def first(g, carry):
            issue_rows(0, 0, pl.multiple_of(g * grp, grp))
            return carry

        lax.fori_loop(0, tc // grp, first, 0)

        @pl.when(nt > 1)
        def _():
            idx_copy(1).start()

    @pl.when(i + 2 < nt)
    def _():
        idx_copy(i + 2).start()

    @pl.when(i + 1 < nt)
    def _():
        idx_copy(i + 1).wait()

    lo, hi = _unpack_halves(_load_rows(hnp_ref, spr))
    gu = (jnp.dot(lo.astype(BF16), wgu_ref[0:half, :], preferred_element_type=F32)
          + jnp.dot(hi.astype(BF16), wgu_ref[half:, :], preferred_element_type=F32))
    act = _silu(gu[:, :de]) * gu[:, de:]
    y_scr[...] = jnp.dot(act.astype(BF16), wd_ref[...], preferred_element_type=F32)

    tw = jnp.concatenate([tw_ref[...], jnp.zeros((LANES - n_pick, tc), F32)], axis=0)
    w_scr[...] = jnp.concatenate([tw[:, c0:c0 + LANES].T for c0 in range(0, tc, LANES)], axis=0)

    gslot = i % 2
    gather_wait(gslot)
    has_next = i + 1 < nt
    nxt_islot = jnp.where(has_next, (i + 1) % 3, i % 3)
    nxt_gslot = (i + 1) % 2
    gt = gt_ref[...]
    gf = gf_ref[...]

    def body(g, carry):
        t0 = pl.multiple_of(g * grp, grp)
        rows = pl.ds(t0, grp)
        acc_lo = y_scr[rows, 0:half]
        acc_hi = y_scr[rows, half:]
        wrow = w_scr[rows, :]
        for k in range(n_pick):
            src = gbuf.at[gslot, k]
            words = jnp.concatenate(
                [src[pl.ds(t0 * spr + s, grp, stride=spr), :] for s in range(spr)], axis=1)
            rlo, rhi = _unpack_halves(words)
            wk = wrow[:, k:k + 1]
            acc_lo = acc_lo + wk * rlo
            acc_hi = acc_hi + wk * rhi
        x2 = x1_ref[rows, :] + gt * jnp.concatenate([acc_lo, acc_hi], axis=1)
        ms = jnp.mean(x2 * x2, axis=-1, keepdims=True)
        o_ref[rows, :] = x2 * lax.rsqrt(ms + EPS) * gf
        issue_rows(nxt_islot, nxt_gslot, t0)
        return carry

    lax.fori_loop(0, tc // grp, body, 0)

    @pl.when(i == nt - 1)
    def _():
        gather_wait(nxt_gslot)


def _combine(dest, ys, top_w_t, x1, hnp, gt2, g_final, ws_gu, ws_d, seq, tc=128):
    t, d = x1.shape
    half = d // 2
    k = dest.shape[0]
    tpb = seq // tc
    de2 = ws_gu.shape[1]
    return pl.pallas_call(
        _combine_kernel,
        grid=(t // tc,),
        in_specs=[pl.BlockSpec(memory_space=pl.ANY),
                  pl.BlockSpec(memory_space=pl.ANY),
                  pl.BlockSpec((k, tc), lambda i: (0, i)),
                  pl.BlockSpec((tc, d), lambda i: (i, 0)),
                  pl.BlockSpec((tc * (half // LANES), LANES), lambda i: (i, 0)),
                  pl.BlockSpec((None, 1, d), lambda i: (i // tpb, 0, 0)),
                  pl.BlockSpec((1, d), lambda i: (0, 0)),
                  pl.BlockSpec((d, de2), lambda i: (0, 0)),
                  pl.BlockSpec((de2 // 2, d), lambda i: (0, 0))],
        out_specs=pl.BlockSpec((tc, d), lambda i: (i, 0)),
        out_shape=jax.ShapeDtypeStruct((t, d), F32),
        scratch_shapes=[pltpu.SMEM((3, k, tc), I32),
                        pltpu.VMEM((2, k, tc * (half // LANES), LANES), U32),
                        pltpu.VMEM((tc, d), F32),
                        pltpu.VMEM((tc, LANES), F32),
                        pltpu.SemaphoreType.DMA((3,)),
                        pltpu.SemaphoreType.DMA((2,))],
        compiler_params=_cparams(("arbitrary",)),
        name="combine",
    )(dest, ys, top_w_t, x1, hnp, gt2, g_final.reshape(1, d), ws_gu, ws_d)


def kernel(x, c, w_ada, b_ada, g_mix, w_in, conv_w, conv_b, conv_ln_g, conv_ln_b, qk_conv_w, qk_conv_b, b_igate, b_fgate, mlstm_norm_g, w_out, g_ffn, w_router, router_bias, w_gate_up, w_down, ws_gate_up, ws_down, g_final):
    bsz, seq, d = x.shape
    depth = w_ada.shape[0]
    assert depth == 1, "single-layer block"
    dc = conv_w.shape[2]
    dm = d - dc
    nh = b_igate.shape[1]
    dh = dm // nh
    n_exp = w_router.shape[2]
    t = bsz * seq
    n_main = 2 * dc + 4 * dm
    n_assign = TOP_K * t
    mb = MOE_BLOCK
    assert dc == dm and nh == N_HEADS and 2 * nh <= LANES and n_assign % mb == 0

    mod = _ada(c, w_ada[0], b_ada[0])
    sh1, sc1, gt1, sh2, sc2, gt2 = [m.reshape(bsz, 1, d) for m in jnp.split(mod, 6, axis=-1)]

    x2 = x.reshape(t, d)
    w_main = w_in[0, :, :n_main].astype(BF16)
    w_gate = jnp.zeros((d, LANES), BF16).at[:, :2 * nh].set(w_in[0, :, n_main:].astype(BF16))
    proj, gates = _inproj(x2, sc1, sh1, g_mix[0].reshape(1, d), w_main, w_gate, seq)

    u = _convmod(proj, conv_w[0], conv_b[0], conv_ln_g[0], conv_ln_b[0], bsz, seq)

    gbias = jnp.zeros((1, LANES), F32).at[0, :nh].set(b_igate[0]).at[0, nh:2 * nh].set(b_fgate[0])
    hm = _mlstm(proj, gates, gbias, qk_conv_w[0], qk_conv_b[0], mlstm_norm_g[0],
                bsz, seq, dm, dh, col0=2 * dc // dm)

    x1, hnp, top_idx_t, top_w_t, cnt = _outproj(
        u, hm, x2, gt1, sc2, sh2, g_ffn[0].reshape(1, d), w_out[0].astype(BF16),
        w_router[0], router_bias[0], seq)

    sizes = jnp.sum(cnt, axis=1).astype(I32)
    starts = jnp.cumsum(sizes) - sizes
    dest = _plan(top_idx_t, starts.astype(F32).reshape(n_exp, 1))
    xs = _dispatch(dest, hnp, n_assign + mb, mb, d // 2 // LANES)
    block_off, nblk, bstart, n_used = _block_offsets(sizes, starts, mb, n_assign // mb + n_exp)
    ys = _moe(xs, block_off, nblk, bstart, n_used, w_gate_up[0], w_down[0], mb)

    out = _combine(dest, ys, top_w_t, x1, hnp, gt2, g_final,
                   ws_gate_up[0].astype(BF16), ws_down[0].astype(BF16), seq)
    return out.reshape(bsz, seq, d)
```

```python
import functools

import jax
import jax.numpy as jnp
from jax import lax
from jax.experimental import pallas as pl
from jax.experimental.pallas import tpu as pltpu

F32 = jnp.float32
BF16 = jnp.bfloat16
U32 = jnp.uint32
I32 = jnp.int32

EPS = 1e-6
STAB_INIT = -1e30
TOP_K = 8
N_GROUPS = 8
TOPK_GROUPS = 4
ROUTED_SCALE = 2.5
N_HEADS = 4

LANES = 128
SUBLANES = 8
VMEM_LIMIT = 56 * 1024 * 1024

MLSTM_CHUNK = 256
MOE_BLOCK = 128
NEG_INF = float("-inf")
HIGHEST = lax.Precision.HIGHEST


def _cparams(sem):
    return pltpu.CompilerParams(dimension_semantics=sem, vmem_limit_bytes=VMEM_LIMIT)


def _sigmoid(x):
    return 1.0 / (1.0 + jnp.exp(-x))


def _silu(x):
    return x * _sigmoid(x)


def _pack_halves(lo, hi):
    lo_b = lax.bitcast_convert_type(lo.astype(BF16).astype(F32), U32)
    hi_b = lax.bitcast_convert_type(hi.astype(BF16).astype(F32), U32)
    return (hi_b & jnp.uint32(0xFFFF0000)) | (lo_b >> 16)


def _unpack_halves(w):
    lo = lax.bitcast_convert_type(w << 16, F32)
    hi = lax.bitcast_convert_type(w & jnp.uint32(0xFFFF0000), F32)
    return lo, hi


def _row_slab(ref, r, s_per_row, n=1):
    return ref.at[pl.ds(pl.multiple_of(r * s_per_row, s_per_row), n * s_per_row), :]


def _store_rows(ref, words):
    m = words.shape[0]
    s_per_row = words.shape[1] // LANES
    for s in range(s_per_row):
        ref[pl.ds(s, m, stride=s_per_row), :] = words[:, s * LANES:(s + 1) * LANES]


def _load_rows(ref, s_per_row):
    m = ref.shape[0] // s_per_row
    return jnp.concatenate([ref[pl.ds(s, m, stride=s_per_row), :] for s in range(s_per_row)], axis=1)


def _ada_kernel(c_ref, w_ref, b_ref, o_ref):
    cs = _silu(c_ref[...])
    o_ref[...] = jnp.dot(cs, w_ref[...], preferred_element_type=F32, precision=HIGHEST) + b_ref[...]


def _ada(c, w_ada, b_ada):
    bsz, d = c.shape
    n = w_ada.shape[1]
    tn = 1536
    cp = jnp.zeros((SUBLANES, d), F32).at[:bsz].set(c)
    out = pl.pallas_call(
        _ada_kernel,
        grid=(n // tn,),
        in_specs=[pl.BlockSpec((SUBLANES, d), lambda j: (0, 0)),
                  pl.BlockSpec((d, tn), lambda j: (0, j)),
                  pl.BlockSpec((1, tn), lambda j: (0, j))],
        out_specs=pl.BlockSpec((SUBLANES, tn), lambda j: (0, j)),
        out_shape=jax.ShapeDtypeStruct((SUBLANES, n), F32),
        compiler_params=_cparams(("arbitrary",)),
        name="ada",
    )(cp, w_ada, b_ada.reshape(1, n))
    return out[:bsz]


_NORM_ROWS = 256


def _inproj_kernel(x_ref, sc_ref, sh_ref, g_ref, w_ref, wg_ref, o_ref, og_ref, h_scr):
    @pl.when(pl.program_id(1) == 0)
    def _():
        def rows(c, carry):
            r0 = pl.multiple_of(c * _NORM_ROWS, _NORM_ROWS)
            x = x_ref[pl.ds(r0, _NORM_ROWS), :]
            ms = jnp.mean(x * x, axis=-1, keepdims=True)
            h = (x * lax.rsqrt(ms + EPS) * g_ref[...]) * (1.0 + sc_ref[...]) + sh_ref[...]
            hb = h.astype(BF16)
            h_scr[pl.ds(r0, _NORM_ROWS), :] = hb
            og_ref[pl.ds(r0, _NORM_ROWS), :] = jnp.dot(hb, wg_ref[...], preferred_element_type=F32)
            return carry

        lax.fori_loop(0, x_ref.shape[0] // _NORM_ROWS, rows, 0)

    o_ref[...] = jnp.dot(h_scr[...], w_ref[...], preferred_element_type=F32).astype(o_ref.dtype)


def _inproj(x2, sc, sh, g, w_main, w_gate, seq, tm=1024, tn=1536):
    t, d = x2.shape
    n = w_main.shape[1]
    tpb = seq // tm
    return pl.pallas_call(
        _inproj_kernel,
        grid=(t // tm, n // tn),
        in_specs=[pl.BlockSpec((tm, d), lambda i, j: (i, 0)),
                  pl.BlockSpec((None, 1, d), lambda i, j: (i // tpb, 0, 0)),
                  pl.BlockSpec((None, 1, d), lambda i, j: (i // tpb, 0, 0)),
                  pl.BlockSpec((1, d), lambda i, j: (0, 0)),
                  pl.BlockSpec((d, tn), lambda i, j: (0, j)),
                  pl.BlockSpec((d, LANES), lambda i, j: (0, 0))],
        out_specs=[pl.BlockSpec((tm, tn), lambda i, j: (i, j)),
                   pl.BlockSpec((tm, LANES), lambda i, j: (i, 0))],
        out_shape=[jax.ShapeDtypeStruct((t, n), BF16),
                   jax.ShapeDtypeStruct((t, LANES), F32)],
        scratch_shapes=[pltpu.VMEM((tm, d), BF16)],
        compiler_params=_cparams(("arbitrary", "arbitrary")),
        name="inproj",
    )(x2, sc, sh, g, w_main, w_gate)


_CONV_RC = 128
_CONV_HIST = 32


def _convmod_kernel(a_ref, gate_ref, w_ref, b_ref, lg_ref, lb_ref, o_ref, ext_scr, y_scr, *, width):
    tt, ch = a_ref.shape
    hist = _CONV_HIST
    rc = _CONV_RC

    @pl.when(pl.program_id(1) == 0)
    def _():
        ext_scr[:, 0:hist, :] = jnp.zeros((ch // LANES, hist, LANES), F32)

    @pl.when(pl.program_id(1) != 0)
    def _():
        ext_scr[:, 0:hist, :] = ext_scr[:, tt:tt + hist, :]

    for c in range(ch // LANES):
        cols = slice(c * LANES, (c + 1) * LANES)
        ext_scr[c, hist:hist + tt, :] = a_ref[:, cols].astype(F32) * _sigmoid(gate_ref[:, cols].astype(F32))

    off0 = hist - (width - 1)
    n_rb = tt // rc
    n_acc = rc // SUBLANES

    def block(it, carry):
        c = it // n_rb
        r0 = pl.multiple_of((it % n_rb) * rc, rc)
        acc = [jnp.broadcast_to(b_ref[c], (SUBLANES, LANES))] * n_acc
        for j in range(width):
            wj = jnp.broadcast_to(w_ref[c, j:j + 1, :], (SUBLANES, LANES))
            for i in range(n_acc):
                acc[i] = acc[i] + wj * ext_scr[c, pl.ds(r0 + off0 + j + i * SUBLANES, SUBLANES), :]
        for i in range(n_acc):
            y_scr[c, pl.ds(r0 + i * SUBLANES, SUBLANES), :] = acc[i]
        return carry

    lax.fori_loop(0, (ch // LANES) * n_rb, block, 0)

    y = jnp.concatenate([y_scr[c] for c in range(ch // LANES)], axis=1)
    mu = jnp.mean(y, axis=-1, keepdims=True)
    yc = y - mu
    var = jnp.mean(yc * yc, axis=-1, keepdims=True)
    z = yc * lax.rsqrt(var + EPS) * lg_ref[...] + lb_ref[...]
    o_ref[...] = _silu(z).astype(o_ref.dtype)


def _convmod(proj, conv_w, conv_b, ln_g, ln_b, bsz, seq, tt=512):
    width, ch = conv_w.shape
    assert width - 1 <= _CONV_HIST and tt % _CONV_RC == 0
    tpb = seq // tt
    n_pl = ch // LANES
    vec = lambda: pl.BlockSpec((1, ch), lambda b, i: (0, 0))
    w_pl = conv_w.reshape(width, n_pl, LANES).transpose(1, 0, 2)
    b_pl = conv_b.reshape(n_pl, 1, LANES)
    return pl.pallas_call(
        functools.partial(_convmod_kernel, width=width),
        grid=(bsz, tpb),
        in_specs=[pl.BlockSpec((tt, ch), lambda b, i: (b * tpb + i, 0)),
                  pl.BlockSpec((tt, ch), lambda b, i: (b * tpb + i, 1)),
                  pl.BlockSpec((n_pl, width, LANES), lambda b, i: (0, 0, 0)),
                  pl.BlockSpec((n_pl, 1, LANES), lambda b, i: (0, 0, 0)),
                  vec(), vec()],
        out_specs=pl.BlockSpec((tt, ch), lambda b, i: (b * tpb + i, 0)),
        out_shape=jax.ShapeDtypeStruct((bsz * seq, ch), BF16),
        scratch_shapes=[pltpu.VMEM((n_pl, tt + _CONV_HIST, LANES), F32),
                        pltpu.VMEM((n_pl, tt, LANES), F32)],
        compiler_params=_cparams(("arbitrary", "arbitrary")),
        name="convmod",
    )(proj, proj, w_pl, b_pl, ln_g.reshape(1, ch), ln_b.reshape(1, ch))


def _log_sigmoid(x):
    return -(jnp.maximum(-x, 0.0) + jnp.log(1.0 + jnp.exp(-jnp.abs(x))))


def _mlstm_kernel(q_ref, k_ref, v_ref, og_ref, gates_ref, gbias_ref, cw_ref, cb_ref, ng_ref,
                  o_ref, ext_scr, c_scr, n_scr, m_scr, *, dh, qk_width):
    ln = q_ref.shape[0]
    dm = q_ref.shape[1]
    nh = dm // dh
    hist = SUBLANES

    n_pl = dm // LANES
    @pl.when(pl.program_id(1) == 0)
    def _():
        ext_scr[:, 0:hist, :] = jnp.zeros((2 * n_pl, hist, LANES), F32)
        c_scr[...] = jnp.zeros(c_scr.shape, F32)
        n_scr[...] = jnp.zeros(n_scr.shape, F32)
        m_scr[...] = jnp.full(m_scr.shape, STAB_INIT, F32)

    @pl.when(pl.program_id(1) != 0)
    def _():
        ext_scr[:, 0:hist, :] = ext_scr[:, ln:ln + hist, :]

    for c in range(n_pl):
        cols = slice(c * LANES, (c + 1) * LANES)
        ext_scr[c, hist:hist + ln, :] = q_ref[:, cols].astype(F32)
        ext_scr[n_pl + c, hist:hist + ln, :] = k_ref[:, cols].astype(F32)

    gts = gates_ref[...] + gbias_ref[...]
    lf = _log_sigmoid(gts)
    row = lax.broadcasted_iota(I32, (ln, ln), 0)
    col = lax.broadcasted_iota(I32, (ln, ln), 1)
    causal = row >= col
    tri = causal.astype(F32)
    bcum = jnp.dot(tri, lf, preferred_element_type=F32, precision=HIGHEST)
    bcum_t = bcum.T
    gts_t = gts.T

    def short_conv(c0):
        parts = []
        for c in range(c0 // LANES, (c0 + dh) // LANES):
            cols = slice(c * LANES, (c + 1) * LANES)
            acc = jnp.zeros((ln, LANES), F32) + cb_ref[:, cols]
            for j in range(qk_width):
                s = hist - (qk_width - 1) + j
                acc = acc + cw_ref[j:j + 1, cols] * ext_scr[c, s:s + ln, :]
            parts.append(acc)
        return _silu(jnp.concatenate(parts, axis=1))

    for h in range(nh):
        c0 = h * dh
        q = short_conv(c0)
        k = short_conv(dm + c0) * (dh ** -0.5)
        qb = q.astype(BF16)
        kb = k.astype(BF16)
        v = v_ref[:, c0:c0 + dh]
        vf = v.astype(F32)

        b_col = bcum[:, nh + h:nh + h + 1]
        b_row = bcum_t[nh + h:nh + h + 1, :]
        ig_col = gts[:, h:h + 1]
        ig_row = gts_t[h:h + 1, :]
        m_prev = m_scr[h, 0:1, 0:1]

        dmat = jnp.where(causal, b_col - b_row + ig_row, NEG_INF)
        m_inter = b_col + m_prev
        m_t = jnp.maximum(m_inter, jnp.max(dmat, axis=1, keepdims=True))
        pmat = jnp.exp(dmat - m_t)
        qk = lax.dot_general(qb, kb, (((1,), (1,)), ((), ())), preferred_element_type=F32)
        s = qk * pmat
        a = jnp.exp(m_inter - m_t)
        inter = jnp.dot(qb, c_scr[h].astype(BF16), preferred_element_type=F32)
        num = jnp.dot(s.astype(BF16), v, preferred_element_type=F32) + a * inter
        den = jnp.sum(s, axis=1, keepdims=True) + a * jnp.sum(q * n_scr[h], axis=1, keepdims=True)
        hv = num / jnp.maximum(jnp.abs(den), jnp.exp(-m_t))

        mu = jnp.mean(hv, axis=-1, keepdims=True)
        hc = hv - mu
        var = jnp.mean(hc * hc, axis=-1, keepdims=True)
        hn = hc * lax.rsqrt(var + EPS) * ng_ref[:, c0:c0 + dh]
        og = og_ref[:, c0:c0 + dh].astype(F32)
        o_ref[:, c0:c0 + dh] = (_sigmoid(og) * hn).astype(o_ref.dtype)

        b_last = b_col[ln - 1:ln, :]
        g_col = b_last - b_col + ig_col
        m_new = jnp.maximum(b_last + m_prev, jnp.max(g_col, axis=0, keepdims=True))
        wk = jnp.exp(g_col - m_new)
        decay = jnp.exp(b_last + m_prev - m_new)
        upd = lax.dot_general(kb, (vf * wk).astype(BF16), (((0,), (0,)), ((), ())),
                              preferred_element_type=F32)
        c_scr[h] = decay * c_scr[h] + upd
        n_scr[h] = decay * n_scr[h] + jnp.sum(k * wk, axis=0, keepdims=True)
        m_scr[h] = jnp.broadcast_to(m_new, m_scr.shape[1:])


def _mlstm(proj, gates, gbias, qk_conv_w, qk_conv_b, norm_g, bsz, seq, dm, dh, col0):
    ln = MLSTM_CHUNK
    nc = seq // ln
    nh = dm // dh
    width = qk_conv_w.shape[0]
    blk = lambda off: pl.BlockSpec((ln, dm), lambda b, c: (b * nc + c, col0 + off))
    return pl.pallas_call(
        functools.partial(_mlstm_kernel, dh=dh, qk_width=width),
        grid=(bsz, nc),
        in_specs=[blk(0), blk(1), blk(2), blk(3),
                  pl.BlockSpec((ln, LANES), lambda b, c: (b * nc + c, 0)),
                  pl.BlockSpec((1, LANES), lambda b, c: (0, 0)),
                  pl.BlockSpec((width, 2 * dm), lambda b, c: (0, 0)),
                  pl.BlockSpec((1, 2 * dm), lambda b, c: (0, 0)),
                  pl.BlockSpec((1, dm), lambda b, c: (0, 0))],
        out_specs=pl.BlockSpec((ln, dm), lambda b, c: (b * nc + c, 0)),
        out_shape=jax.ShapeDtypeStruct((bsz * seq, dm), BF16),
        scratch_shapes=[pltpu.VMEM((2 * dm // LANES, SUBLANES + ln, LANES), F32),
                        pltpu.VMEM((nh, dh, dh), F32),
                        pltpu.VMEM((nh, 1, dh), F32),
                        pltpu.VMEM((nh, SUBLANES, LANES), F32)],
        compiler_params=_cparams(("arbitrary", "arbitrary")),
        name="mlstm",
    )(proj, proj, proj, proj, gates, gbias, qk_conv_w, qk_conv_b.reshape(1, 2 * dm),
      norm_g.reshape(1, dm))


def _route_chunk(logits_t, bias_col, n_exp):
    gsz = n_exp // N_GROUPS
    scores = _sigmoid(logits_t)
    choice = scores + bias_col
    rows_g = lax.broadcasted_iota(I32, (gsz, LANES), 0).astype(F32)
    gscore = []
    for g in range(N_GROUPS):
        v = choice[g * gsz:(g + 1) * gsz, :]
        m1 = jnp.max(v, axis=0, keepdims=True)
        i1 = jnp.min(jnp.where(v == m1, rows_g, float(gsz)), axis=0, keepdims=True)
        m2 = jnp.max(jnp.where(rows_g == i1, NEG_INF, v), axis=0, keepdims=True)
        gscore.append(m1 + m2)
    gs = jnp.concatenate(gscore, axis=0)
    giota = lax.broadcasted_iota(I32, (N_GROUPS, LANES), 0).astype(F32)
    keep = jnp.zeros((N_GROUPS, LANES), F32)
    for _ in range(TOPK_GROUPS):
        m = jnp.max(gs, axis=0, keepdims=True)
        gi = jnp.min(jnp.where(gs == m, giota, float(N_GROUPS)), axis=0, keepdims=True)
        sel = giota == gi
        keep = jnp.where(sel, 1.0, keep)
        gs = jnp.where(sel, NEG_INF, gs)
    mc = jnp.concatenate(
        [jnp.where(keep[g:g + 1, :] > 0.5, choice[g * gsz:(g + 1) * gsz, :], NEG_INF)
         for g in range(N_GROUPS)], axis=0)
    rows = lax.broadcasted_iota(I32, (n_exp, LANES), 0).astype(F32)
    idxs, ws = [], []
    picked = jnp.zeros((n_exp, LANES), F32)
    for _ in range(TOP_K):
        m = jnp.max(mc, axis=0, keepdims=True)
        idx = jnp.min(jnp.where(mc == m, rows, float(n_exp)), axis=0, keepdims=True)
        sel = rows == idx
        ws.append(jnp.sum(jnp.where(sel, scores, 0.0), axis=0, keepdims=True))
        idxs.append(idx)
        picked = jnp.where(sel, 1.0, picked)
        mc = jnp.where(sel, NEG_INF, mc)
    w = jnp.concatenate(ws, axis=0)
    w = w / jnp.sum(w, axis=0, keepdims=True) * ROUTED_SCALE
    return jnp.concatenate(idxs, axis=0).astype(I32), w, picked


def _outproj_kernel(u_ref, hm_ref, x_ref, gt_ref, sc_ref, sh_ref, g_ref, w_ref, wrh_ref, wrl_ref,
                    rb_ref, x1_ref, hnp_ref, idx_ref, tw_ref, cnt_ref):
    tm, d = x_ref.shape
    dc = u_ref.shape[1]
    half = d // 2
    n_exp = wrh_ref.shape[1]

    @pl.when(pl.program_id(0) == 0)
    def _():
        cnt_ref[...] = jnp.zeros(cnt_ref.shape, F32)

    mix = (jnp.dot(u_ref[...], w_ref[0:dc, :], preferred_element_type=F32)
           + jnp.dot(hm_ref[...], w_ref[dc:, :], preferred_element_type=F32))
    x1 = x_ref[...] + gt_ref[...] * mix
    x1_ref[...] = x1
    ms = jnp.mean(x1 * x1, axis=-1, keepdims=True)
    hn = (x1 * lax.rsqrt(ms + EPS) * g_ref[...]) * (1.0 + sc_ref[...]) + sh_ref[...]
    _store_rows(hnp_ref, _pack_halves(hn[:, :half], hn[:, half:]))
    hn_hi = hn.astype(BF16)
    hn_lo = (hn - hn_hi.astype(F32)).astype(BF16)
    dot = lambda a, b: jnp.dot(a, b, preferred_element_type=F32)
    logits = (dot(hn_hi, wrh_ref[...]) + dot(hn_lo, wrh_ref[...])) + dot(hn_hi, wrl_ref[...])
    logits_t = jnp.concatenate([logits[c0:c0 + LANES, :].T for c0 in range(0, tm, LANES)], axis=1)
    cnt = cnt_ref[...]
    for c0 in range(0, tm, LANES):
        idx, w, picked = _route_chunk(logits_t[:, c0:c0 + LANES], rb_ref[...], n_exp)
        idx_ref[:, c0:c0 + LANES] = idx
        tw_ref[:, c0:c0 + LANES] = w
        cnt = cnt + picked
    cnt_ref[...] = cnt


def _outproj(u, hm, x2, gt1, sc2, sh2, g_ffn, w_out, w_router, router_bias, seq, tm=256):
    t, d = x2.shape
    dc = u.shape[1]
    n_exp = w_router.shape[1]
    tpb = seq // tm
    wr_hi = w_router.astype(BF16)
    wr_lo = (w_router - wr_hi.astype(F32)).astype(BF16)
    per_b = lambda: pl.BlockSpec((None, 1, d), lambda i: (i // tpb, 0, 0))
    return pl.pallas_call(
        _outproj_kernel,
        grid=(t // tm,),
        in_specs=[pl.BlockSpec((tm, dc), lambda i: (i, 0)),
                  pl.BlockSpec((tm, d - dc), lambda i: (i, 0)),
                  pl.BlockSpec((tm, d), lambda i: (i, 0)),
                  per_b(), per_b(), per_b(),
                  pl.BlockSpec((1, d), lambda i: (0, 0)),
                  pl.BlockSpec((d, d), lambda i: (0, 0)),
                  pl.BlockSpec((d, n_exp), lambda i: (0, 0)),
                  pl.BlockSpec((d, n_exp), lambda i: (0, 0)),
                  pl.BlockSpec((n_exp, 1), lambda i: (0, 0))],
        out_specs=[pl.BlockSpec((tm, d), lambda i: (i, 0)),
                   pl.BlockSpec((tm * (d // 2 // LANES), LANES), lambda i: (i, 0)),
                   pl.BlockSpec((TOP_K, tm), lambda i: (0, i)),
                   pl.BlockSpec((TOP_K, tm), lambda i: (0, i)),
                   pl.BlockSpec((n_exp, LANES), lambda i: (0, 0))],
        out_shape=[jax.ShapeDtypeStruct((t, d), F32),
                   jax.ShapeDtypeStruct((t * (d // 2 // LANES), LANES), U32),
                   jax.ShapeDtypeStruct((TOP_K, t), I32),
                   jax.ShapeDtypeStruct((TOP_K, t), F32),
                   jax.ShapeDtypeStruct((n_exp, LANES), F32)],
        compiler_params=_cparams(("arbitrary",)),
        name="outproj_route",
    )(u, hm, x2, gt1, sc2, sh2, g_ffn, w_out, wr_hi, wr_lo, router_bias.reshape(n_exp, 1))


def _plan_kernel(idx_ref, start_ref, dest_ref, run_scr):
    n_exp = start_ref.shape[0]
    tp = idx_ref.shape[1]

    @pl.when(pl.program_id(0) == 0)
    def _():
        run_scr[...] = start_ref[...]

    rows = lax.broadcasted_iota(I32, (n_exp, tp), 0)
    earlier = (lax.broadcasted_iota(I32, (tp, tp), 0)
               < lax.broadcasted_iota(I32, (tp, tp), 1)).astype(BF16)
    base = run_scr[...]
    for k in range(TOP_K):
        oh = rows == idx_ref[k:k + 1, :]
        ohf = oh.astype(F32)
        pref = jnp.dot(ohf.astype(BF16), earlier, preferred_element_type=F32)
        dest = jnp.sum(jnp.where(oh, pref + base, 0.0), axis=0, keepdims=True)
        dest_ref[k:k + 1, :] = dest.astype(I32)
        base = base + jnp.sum(ohf, axis=1, keepdims=True)
    run_scr[...] = base


def _plan(top_idx_t, starts_col, tp=512):
    k, t = top_idx_t.shape
    n_exp = starts_col.shape[0]
    return pl.pallas_call(
        _plan_kernel,
        grid=(t // tp,),
        in_specs=[pl.BlockSpec((k, tp), lambda i: (0, i)),
                  pl.BlockSpec((n_exp, 1), lambda i: (0, 0))],
        out_specs=pl.BlockSpec((k, tp), lambda i: (0, i)),
        out_shape=jax.ShapeDtypeStruct((k, t), I32),
        scratch_shapes=[pltpu.VMEM((n_exp, 1), F32)],
        compiler_params=_cparams(("arbitrary",)),
        name="plan",
    )(top_idx_t, starts_col)


def _dispatch_kernel(dest_hbm, hnp_ref, xs_hbm, idx_s, zero_scr, sem_i, sem_r, *, spr):
    i = pl.program_id(0)
    nt = pl.num_programs(0)
    td = hnp_ref.shape[0] // spr
    n_pick = idx_s.shape[1]

    def idx_copy(j):
        return pltpu.make_async_copy(dest_hbm.at[:, pl.ds(j * td, td)], idx_s.at[j % 2], sem_i.at[j % 2])

    @pl.when(i == 0)
    def _():
        idx_copy(0).start()

    @pl.when(i + 1 < nt)
    def _():
        idx_copy(i + 1).start()

    idx_copy(i).wait()
    slot = i % 2

    def body(t, carry):
        src = _row_slab(hnp_ref, t, spr)
        for k in range(n_pick):
            dst = idx_s[slot, k, t]
            pltpu.make_async_copy(src, _row_slab(xs_hbm, dst, spr), sem_r.at[0]).start(priority=k % 2)
        return carry

    lax.fori_loop(0, td, body, 0)

    for _ in range(n_pick):
        pltpu.make_async_copy(hnp_ref, _row_slab(xs_hbm, 0, spr, td), sem_r.at[0]).wait()

    @pl.when(i == nt - 1)
    def _():
        zero_scr[...] = jnp.zeros(zero_scr.shape, zero_scr.dtype)
        nz = zero_scr.shape[0]
        tail = pltpu.make_async_copy(zero_scr, xs_hbm.at[pl.ds(xs_hbm.shape[0] - nz, nz), :], sem_r.at[0])
        tail.start()
        tail.wait()


def _dispatch(dest, hnp, n_rows, tail_rows, spr, td=256):
    t = hnp.shape[0] // spr
    k = dest.shape[0]
    return pl.pallas_call(
        functools.partial(_dispatch_kernel, spr=spr),
        grid=(t // td,),
        in_specs=[pl.BlockSpec(memory_space=pl.ANY),
                  pl.BlockSpec((td * spr, LANES), lambda i: (i, 0))],
        out_specs=pl.BlockSpec(memory_space=pl.ANY),
        out_shape=jax.ShapeDtypeStruct((n_rows * spr, LANES), U32),
        scratch_shapes=[pltpu.SMEM((2, k, td), I32),
                        pltpu.VMEM((tail_rows * spr, LANES), U32),
                        pltpu.SemaphoreType.DMA((2,)),
                        pltpu.SemaphoreType.DMA((1,))],
        compiler_params=_cparams(("arbitrary",)),
        name="dispatch",
    )(dest, hnp)


_X_SLOTS = 3
_W_SLOTS = 3


def _block_offsets(sizes, starts, mb, n_blocks):
    nblk = (sizes + mb - 1) // mb
    blk_end = jnp.cumsum(nblk)
    bstart = blk_end - nblk
    b = jnp.arange(n_blocks, dtype=I32)
    owner = (blk_end[None, :] > b[:, None]) & (bstart[None, :] <= b[:, None])
    base = jnp.sum(jnp.where(owner, (starts - bstart * mb)[None, :], 0), axis=1)
    off = jnp.where(b < blk_end[-1], base + b * mb, 0)
    return off.astype(I32), nblk.astype(I32), bstart.astype(I32), blk_end[-1:].astype(I32)


def _moe_kernel(off_ref, nblk_ref, bstart_ref, nu_ref, xs_hbm, wgu_hbm, wd_hbm, ys_hbm,
                xbuf, ybuf, wgu_f, wd_f, wgu_b, wd_b, sem_x, sem_y, sem_w):
    e = pl.program_id(0)
    n_exp = pl.num_programs(0)
    n_used = nu_ref[0]
    half = wgu_b.shape[0] // 2
    spr = half // LANES
    mb = ybuf.shape[1] // spr
    de = wd_b.shape[0]
    ahead = _X_SLOTS - 1

    def x_copy(b):
        slot = b % _X_SLOTS
        return pltpu.make_async_copy(_row_slab(xs_hbm, off_ref[b], spr, mb), xbuf.at[slot], sem_x.at[slot])

    def y_copy(off, slot):
        return pltpu.make_async_copy(ybuf.at[slot], _row_slab(ys_hbm, off, spr, mb), sem_y.at[slot])

    def w_copies(j):
        slot = j % _W_SLOTS
        return (pltpu.make_async_copy(wgu_hbm.at[j], wgu_f.at[slot], sem_w.at[0, slot]),
                pltpu.make_async_copy(wd_hbm.at[j], wd_f.at[slot], sem_w.at[1, slot]))

    @pl.when(e == 0)
    def _():
        for j in range(_W_SLOTS - 1):
            for cp in w_copies(j):
                pl.when(j < n_exp)(functools.partial(cp.start, priority=1))
        for j in range(ahead):
            pl.when(j < n_used)(x_copy(j).start)

    @pl.when(e + _W_SLOTS - 1 < n_exp)
    def _():
        for cp in w_copies(e + _W_SLOTS - 1):
            cp.start(priority=1)

    nb = nblk_ref[e]
    b0 = bstart_ref[e]
    wslot = e % _W_SLOTS
    copy_gu, copy_d = w_copies(e)

    copy_gu.wait()

    @pl.when(nb > 0)
    def _():
        wgu_b[...] = wgu_f[wslot].astype(BF16)

    copy_d.wait()

    @pl.when(nb > 0)
    def _():
        wd_b[...] = wd_f[wslot].astype(BF16)

        def body(i, carry):
            b = b0 + i

            @pl.when(b + ahead < n_used)
            def _():
                x_copy(b + ahead).start()

            x_copy(b).wait()
            lo, hi = _unpack_halves(_load_rows(xbuf.at[b % _X_SLOTS], spr))
            gu = (jnp.dot(lo.astype(BF16), wgu_b[0:half, :], preferred_element_type=F32)
                  + jnp.dot(hi.astype(BF16), wgu_b[half:, :], preferred_element_type=F32))
            act = _silu(gu[:, :de]) * gu[:, de:]
            y = jnp.dot(act.astype(BF16), wd_b[...], preferred_element_type=F32)
            yslot = b % 2
            _store_rows(ybuf.at[yslot], _pack_halves(y[:, :half], y[:, half:]))

            @pl.when(b > 0)
            def _():
                y_copy(0, 1 - yslot).wait()

            y_copy(off_ref[b], yslot).start()
            return carry

        lax.fori_loop(0, nb, body, 0)

    @pl.when(e == n_exp - 1)
    def _():
        last = (n_used - 1) % 2
        y_copy(0, last).wait()
        tail = y_copy(ys_hbm.shape[0] // spr - mb, last)
        tail.start()
        tail.wait()


def _moe(xs, block_off, nblk, bstart, n_used, w_gate_up, w_down, mb):
    n_exp, d, de2 = w_gate_up.shape
    s = d // 2 // LANES
    de = w_down.shape[1]
    grid_spec = pltpu.PrefetchScalarGridSpec(
        num_scalar_prefetch=4,
        grid=(n_exp,),
        in_specs=[pl.BlockSpec(memory_space=pl.ANY),
                  pl.BlockSpec(memory_space=pl.ANY),
                  pl.BlockSpec(memory_space=pl.ANY)],
        out_specs=pl.BlockSpec(memory_space=pl.ANY),
        scratch_shapes=[pltpu.VMEM((_X_SLOTS, mb * s, LANES), U32),
                        pltpu.VMEM((2, mb * s, LANES), U32),
                        pltpu.VMEM((_W_SLOTS, d, de2), F32),
                        pltpu.VMEM((_W_SLOTS, de, d), F32),
                        pltpu.VMEM((d, de2), BF16),
                        pltpu.VMEM((de, d), BF16),
                        pltpu.SemaphoreType.DMA((_X_SLOTS,)),
                        pltpu.SemaphoreType.DMA((2,)),
                        pltpu.SemaphoreType.DMA((2, _W_SLOTS))])
    return pl.pallas_call(
        _moe_kernel,
        grid_spec=grid_spec,
        out_shape=jax.ShapeDtypeStruct(xs.shape, U32),
        compiler_params=_cparams(("arbitrary",)),
        name="moe",
    )(block_off, nblk, bstart, n_used, xs, w_gate_up, w_down)


def _combine_kernel(dest_hbm, ys_hbm, tw_ref, x1_ref, hnp_ref, gt_ref, gf_ref, wgu_ref, wd_ref,
                    o_ref, idx_s, gbuf, y_scr, w_scr, sem_i, sem_g):
    i = pl.program_id(0)
    nt = pl.num_programs(0)
    tc, d = x1_ref.shape
    half = d // 2
    spr = half // LANES
    de = wd_ref.shape[0]
    n_pick = tw_ref.shape[0]
    grp = SUBLANES

    def idx_copy(j):
        return pltpu.make_async_copy(dest_hbm.at[:, pl.ds(j * tc, tc)], idx_s.at[j % 3], sem_i.at[j % 3])

    def issue_rows(islot, gslot, t0):
        for j in range(grp):
            t = t0 + j
            for k in range(n_pick):
                src = idx_s[islot, k, t]
                pltpu.make_async_copy(_row_slab(ys_hbm, src, spr), _row_slab(gbuf.at[gslot, k], t, spr),
                                      sem_g.at[gslot]).start(priority=k % 2)

    def gather_wait(gslot):
        for k in range(n_pick):
            pltpu.make_async_copy(_row_slab(ys_hbm, 0, spr, tc), gbuf.at[gslot, k], sem_g.at[gslot]).wait()

    @pl.when(i == 0)
    def _():
        idx_copy(0).start()
        idx_copy(0).wait()

        def first(g, carry):
            issue_rows(0, 0, pl.multiple_of(g * grp, grp))
            return carry

        lax.fori_loop(0, tc // grp, first, 0)

        @pl.when(nt > 1)
        def _():
            idx_copy(1).start()

    @pl.when(i + 2 < nt)
    def _():
        idx_copy(i + 2).start()

    @pl.when(i + 1 < nt)
    def _():
        idx_copy(i + 1).wait()

    lo, hi = _unpack_halves(_load_rows(hnp_ref, spr))
    gu = (jnp.dot(lo.astype(BF16), wgu_ref[0:half, :], preferred_element_type=F32)
          + jnp.dot(hi.astype(BF16), wgu_ref[half:, :], preferred_element_type=F32))
    act = _silu(gu[:, :de]) * gu[:, de:]
    y_scr[...] = jnp.dot(act.astype(BF16), wd_ref[...], preferred_element_type=F32)

    tw = jnp.concatenate([tw_ref[...], jnp.zeros((LANES - n_pick, tc), F32)], axis=0)
    w_scr[...] = jnp.concatenate([tw[:, c0:c0 + LANES].T for c0 in range(0, tc, LANES)], axis=0)

    gslot = i % 2
    gather_wait(gslot)
    has_next = i + 1 < nt
    nxt_islot = jnp.where(has_next, (i + 1) % 3, i % 3)
    nxt_gslot = (i + 1) % 2
    gt = gt_ref[...]
    gf = gf_ref[...]

    def body(g, carry):
        t0 = pl.multiple_of(g * grp, grp)
        rows = pl.ds(t0, grp)
        acc_lo = y_scr[rows, 0:half]
        acc_hi = y_scr[rows, half:]
        wrow = w_scr[rows, :]
        for k in range(n_pick):
            src = gbuf.at[gslot, k]
            words = jnp.concatenate(
                [src[pl.ds(t0 * spr + s, grp, stride=spr), :] for s in range(spr)], axis=1)
            rlo, rhi = _unpack_halves(words)
            wk = wrow[:, k:k + 1]
            acc_lo = acc_lo + wk * rlo
            acc_hi = acc_hi + wk * rhi
        x2 = x1_ref[rows, :] + gt * jnp.concatenate([acc_lo, acc_hi], axis=1)
        ms = jnp.mean(x2 * x2, axis=-1, keepdims=True)
        o_ref[rows, :] = x2 * lax.rsqrt(ms + EPS) * gf
        issue_rows(nxt_islot, nxt_gslot, t0)
        return carry

    lax.fori_loop(0, tc // grp, body, 0)

    @pl.when(i == nt - 1)
    def _():
        gather_wait(nxt_gslot)


def _combine(dest, ys, top_w_t, x1, hnp, gt2, g_final, ws_gu, ws_d, seq, tc=128):
    t, d = x1.shape
    half = d // 2
    k = dest.shape[0]
    tpb = seq // tc
    de2 = ws_gu.shape[1]
    return pl.pallas_call(
        _combine_kernel,
        grid=(t // tc,),
        in_specs=[pl.BlockSpec(memory_space=pl.ANY),
                  pl.BlockSpec(memory_space=pl.ANY),
                  pl.BlockSpec((k, tc), lambda i: (0, i)),
                  pl.BlockSpec((tc, d), lambda i: (i, 0)),
                  pl.BlockSpec((tc * (half // LANES), LANES), lambda i: (i, 0)),
                  pl.BlockSpec((None, 1, d), lambda i: (i // tpb, 0, 0)),
                  pl.BlockSpec((1, d), lambda i: (0, 0)),
                  pl.BlockSpec((d, de2), lambda i: (0, 0)),
                  pl.BlockSpec((de2 // 2, d), lambda i: (0, 0))],
        out_specs=pl.BlockSpec((tc, d), lambda i: (i, 0)),
        out_shape=jax.ShapeDtypeStruct((t, d), F32),
        scratch_shapes=[pltpu.SMEM((3, k, tc), I32),
                        pltpu.VMEM((2, k, tc * (half // LANES), LANES), U32),
                        pltpu.VMEM((tc, d), F32),
                        pltpu.VMEM((tc, LANES), F32),
                        pltpu.SemaphoreType.DMA((3,)),
                        pltpu.SemaphoreType.DMA((2,))],
        compiler_params=_cparams(("arbitrary",)),
        name="combine",
    )(dest, ys, top_w_t, x1, hnp, gt2, g_final.reshape(1, d), ws_gu, ws_d)


def kernel(x, c, w_ada, b_ada, g_mix, w_in, conv_w, conv_b, conv_ln_g, conv_ln_b, qk_conv_w, qk_conv_b, b_igate, b_fgate, mlstm_norm_g, w_out, g_ffn, w_router, router_bias, w_gate_up, w_down, ws_gate_up, ws_down, g_final):
    bsz, seq, d = x.shape
    depth = w_ada.shape[0]
    assert depth == 1, "single-layer block"
    dc = conv_w.shape[2]
    dm = d - dc
    nh = b_igate.shape[1]
    dh = dm // nh
    n_exp = w_router.shape[2]
    t = bsz * seq
    n_main = 2 * dc + 4 * dm
    n_assign = TOP_K * t
    mb = MOE_BLOCK
    assert dc == dm and nh == N_HEADS and 2 * nh <= LANES and n_assign % mb == 0

    mod = _ada(c, w_ada[0], b_ada[0])
    sh1, sc1, gt1, sh2, sc2, gt2 = [m.reshape(bsz, 1, d) for m in jnp.split(mod, 6, axis=-1)]

    x2 = x.reshape(t, d)
    w_main = w_in[0, :, :n_main].astype(BF16)
    w_gate = jnp.zeros((d, LANES), BF16).at[:, :2 * nh].set(w_in[0, :, n_main:].astype(BF16))
    proj, gates = _inproj(x2, sc1, sh1, g_mix[0].reshape(1, d), w_main, w_gate, seq)

    u = _convmod(proj, conv_w[0], conv_b[0], conv_ln_g[0], conv_ln_b[0], bsz, seq)

    gbias = jnp.zeros((1, LANES), F32).at[0, :nh].set(b_igate[0]).at[0, nh:2 * nh].set(b_fgate[0])
    hm = _mlstm(proj, gates, gbias, qk_conv_w[0], qk_conv_b[0], mlstm_norm_g[0],
                bsz, seq, dm, dh, col0=2 * dc // dm)

    x1, hnp, top_idx_t, top_w_t, cnt = _outproj(
        u, hm, x2, gt1, sc2, sh2, g_ffn[0].reshape(1, d), w_out[0].astype(BF16),
        w_router[0], router_bias[0], seq)

    sizes = jnp.sum(cnt, axis=1).astype(I32)
    starts = jnp.cumsum(sizes) - sizes
    dest = _plan(top_idx_t, starts.astype(F32).reshape(n_exp, 1))
    xs = _dispatch(dest, hnp, n_assign + mb, mb, d // 2 // LANES)
    block_off, nblk, bstart, n_used = _block_offsets(sizes, starts, mb, n_assign // mb + n_exp)
    ys = _moe(xs, block_off, nblk, bstart, n_used, w_gate_up[0], w_down[0], mb)

    out = _combine(dest, ys, top_w_t, x1, hnp, gt2, g_final,
                   ws_gate_up[0].astype(BF16), ws_down[0].astype(BF16), seq)
    return out.reshape(bsz, seq, d)
```

```python
import functools

import jax
import jax.numpy as jnp
from jax import lax
from jax.experimental import pallas as pl
from jax.experimental.pallas import tpu as pltpu

F32 = jnp.float32
BF16 = jnp.bfloat16
U32 = jnp.uint32
I32 = jnp.int32

EPS = 1e-6
STAB_INIT = -1e30
TOP_K = 8
N_GROUPS = 8
TOPK_GROUPS = 4
ROUTED_SCALE = 2.5
N_HEADS = 4

LANES = 128
SUBLANES = 8
VMEM_LIMIT = 56 * 1024 * 1024

MLSTM_CHUNK = 256
MOE_BLOCK = 128
NEG_INF = float("-inf")
HIGHEST = lax.Precision.HIGHEST


def _cparams(sem):
    return pltpu.CompilerParams(dimension_semantics=sem, vmem_limit_bytes=VMEM_LIMIT)


def _sigmoid(x):
    return 1.0 / (1.0 + jnp.exp(-x))


def _silu(x):
    return x * _sigmoid(x)


def _pack_halves(lo, hi):
    lo_b = lax.bitcast_convert_type(lo.astype(BF16).astype(F32), U32)
    hi_b = lax.bitcast_convert_type(hi.astype(BF16).astype(F32), U32)
    return (hi_b & jnp.uint32(0xFFFF0000)) | (lo_b >> 16)


def _unpack_halves(w):
    lo = lax.bitcast_convert_type(w << 16, F32)
    hi = lax.bitcast_convert_type(w & jnp.uint32(0xFFFF0000), F32)
    return lo, hi


def _row_slab(ref, r, s_per_row, n=1):
    return ref.at[pl.ds(pl.multiple_of(r * s_per_row, s_per_row), n * s_per_row), :]


def _store_rows(ref, words):
    m = words.shape[0]
    s_per_row = words.shape[1] // LANES
    for s in range(s_per_row):
        ref[pl.ds(s, m, stride=s_per_row), :] = words[:, s * LANES:(s + 1) * LANES]


def _load_rows(ref, s_per_row):
    m = ref.shape[0] // s_per_row
    return jnp.concatenate([ref[pl.ds(s, m, stride=s_per_row), :] for s in range(s_per_row)], axis=1)


def _ada_kernel(c_ref, w_ref, b_ref, o_ref):
    cs = _silu(c_ref[...])
    o_ref[...] = jnp.dot(cs, w_ref[...], preferred_element_type=F32, precision=HIGHEST) + b_ref[...]


def _ada(c, w_ada, b_ada):
    bsz, d = c.shape
    n = w_ada.shape[1]
    tn = 1536
    cp = jnp.zeros((SUBLANES, d), F32).at[:bsz].set(c)
    out = pl.pallas_call(
        _ada_kernel,
        grid=(n // tn,),
        in_specs=[pl.BlockSpec((SUBLANES, d), lambda j: (0, 0)),
                  pl.BlockSpec((d, tn), lambda j: (0, j)),
                  pl.BlockSpec((1, tn), lambda j: (0, j))],
        out_specs=pl.BlockSpec((SUBLANES, tn), lambda j: (0, j)),
        out_shape=jax.ShapeDtypeStruct((SUBLANES, n), F32),
        compiler_params=_cparams(("arbitrary",)),
        name="ada",
    )(cp, w_ada, b_ada.reshape(1, n))
    return out[:bsz]


_NORM_ROWS = 256


def _inproj_kernel(x_ref, sc_ref, sh_ref, g_ref, w_ref, wg_ref, o_ref, og_ref, h_scr):
    @pl.when(pl.program_id(1) == 0)
    def _():
        def rows(c, carry):
            r0 = pl.multiple_of(c * _NORM_ROWS, _NORM_ROWS)
            x = x_ref[pl.ds(r0, _NORM_ROWS), :]
            ms = jnp.mean(x * x, axis=-1, keepdims=True)
            h = (x * lax.rsqrt(ms + EPS) * g_ref[...]) * (1.0 + sc_ref[...]) + sh_ref[...]
            hb = h.astype(BF16)
            h_scr[pl.ds(r0, _NORM_ROWS), :] = hb
            og_ref[pl.ds(r0, _NORM_ROWS), :] = jnp.dot(hb, wg_ref[...], preferred_element_type=F32)
            return carry

        lax.fori_loop(0, x_ref.shape[0] // _NORM_ROWS, rows, 0)

    o_ref[...] = jnp.dot(h_scr[...], w_ref[...], preferred_element_type=F32).astype(o_ref.dtype)


def _inproj(x2, sc, sh, g, w_main, w_gate, seq, tm=1024, tn=1536):
    t, d = x2.shape
    n = w_main.shape[1]
    tpb = seq // tm
    return pl.pallas_call(
        _inproj_kernel,
        grid=(t // tm, n // tn),
        in_specs=[pl.BlockSpec((tm, d), lambda i, j: (i, 0)),
                  pl.BlockSpec((None, 1, d), lambda i, j: (i // tpb, 0, 0)),
                  pl.BlockSpec((None, 1, d), lambda i, j: (i // tpb, 0, 0)),
                  pl.BlockSpec((1, d), lambda i, j: (0, 0)),
                  pl.BlockSpec((d, tn), lambda i, j: (0, j)),
                  pl.BlockSpec((d, LANES), lambda i, j: (0, 0))],
        out_specs=[pl.BlockSpec((tm, tn), lambda i, j: (i, j)),
                   pl.BlockSpec((tm, LANES), lambda i, j: (i, 0))],
        out_shape=[jax.ShapeDtypeStruct((t, n), BF16),
                   jax.ShapeDtypeStruct((t, LANES), F32)],
        scratch_shapes=[pltpu.VMEM((tm, d), BF16)],
        compiler_params=_cparams(("arbitrary", "arbitrary")),
        name="inproj",
    )(x2, sc, sh, g, w_main, w_gate)


_CONV_RC = 128
_CONV_HIST = 32


def _convmod_kernel(a_ref, gate_ref, w_ref, b_ref, lg_ref, lb_ref, o_ref, ext_scr, y_scr, *, width):
    tt, ch = a_ref.shape
    hist = _CONV_HIST
    rc = _CONV_RC

    @pl.when(pl.program_id(1) == 0)
    def _():
        ext_scr[:, 0:hist, :] = jnp.zeros((ch // LANES, hist, LANES), F32)

    @pl.when(pl.program_id(1) != 0)
    def _():
        ext_scr[:, 0:hist, :] = ext_scr[:, tt:tt + hist, :]

    for c in range(ch // LANES):
        cols = slice(c * LANES, (c + 1) * LANES)
        ext_scr[c, hist:hist + tt, :] = a_ref[:, cols].astype(F32) * _sigmoid(gate_ref[:, cols].astype(F32))

    off0 = hist - (width - 1)
    n_rb = tt // rc
    n_acc = rc // SUBLANES

    def block(it, carry):
        c = it // n_rb
        r0 = pl.multiple_of((it % n_rb) * rc, rc)
        acc = [jnp.broadcast_to(b_ref[c], (SUBLANES, LANES))] * n_acc
        for j in range(width):
            wj = jnp.broadcast_to(w_ref[c, j:j + 1, :], (SUBLANES, LANES))
            for i in range(n_acc):
                acc[i] = acc[i] + wj * ext_scr[c, pl.ds(r0 + off0 + j + i * SUBLANES, SUBLANES), :]
        for i in range(n_acc):
            y_scr[c, pl.ds(r0 + i * SUBLANES, SUBLANES), :] = acc[i]
        return carry

    lax.fori_loop(0, (ch // LANES) * n_rb, block, 0)

    y = jnp.concatenate([y_scr[c] for c in range(ch // LANES)], axis=1)
    mu = jnp.mean(y, axis=-1, keepdims=True)
    yc = y - mu
    var = jnp.mean(yc * yc, axis=-1, keepdims=True)
    z = yc * lax.rsqrt(var + EPS) * lg_ref[...] + lb_ref[...]
    o_ref[...] = _silu(z).astype(o_ref.dtype)


def _convmod(proj, conv_w, conv_b, ln_g, ln_b, bsz, seq, tt=512):
    width, ch = conv_w.shape
    assert width - 1 <= _CONV_HIST and tt % _CONV_RC == 0
    tpb = seq // tt
    n_pl = ch // LANES
    vec = lambda: pl.BlockSpec((1, ch), lambda b, i: (0, 0))
    w_pl = conv_w.reshape(width, n_pl, LANES).transpose(1, 0, 2)
    b_pl = conv_b.reshape(n_pl, 1, LANES)
    return pl.pallas_call(
        functools.partial(_convmod_kernel, width=width),
        grid=(bsz, tpb),
        in_specs=[pl.BlockSpec((tt, ch), lambda b, i: (b * tpb + i, 0)),
                  pl.BlockSpec((tt, ch), lambda b, i: (b * tpb + i, 1)),
                  pl.BlockSpec((n_pl, width, LANES), lambda b, i: (0, 0, 0)),
                  pl.BlockSpec((n_pl, 1, LANES), lambda b, i: (0, 0, 0)),
                  vec(), vec()],
        out_specs=pl.BlockSpec((tt, ch), lambda b, i: (b * tpb + i, 0)),
        out_shape=jax.ShapeDtypeStruct((bsz * seq, ch), BF16),
        scratch_shapes=[pltpu.VMEM((n_pl, tt + _CONV_HIST, LANES), F32),
                        pltpu.VMEM((n_pl, tt, LANES), F32)],
        compiler_params=_cparams(("arbitrary", "arbitrary")),
        name="convmod",
    )(proj, proj, w_pl, b_pl, ln_g.reshape(1, ch), ln_b.reshape(1, ch))


def _log_sigmoid(x):
    return -(jnp.maximum(-x, 0.0) + jnp.log(1.0 + jnp.exp(-jnp.abs(x))))


def _mlstm_kernel(q_ref, k_ref, v_ref, og_ref, gates_ref, gbias_ref, cw_ref, cb_ref, ng_ref,
                  o_ref, ext_scr, c_scr, n_scr, m_scr, *, dh, qk_width):
    ln = q_ref.shape[0]
    dm = q_ref.shape[1]
    nh = dm // dh
    hist = SUBLANES

    n_pl = dm // LANES
    @pl.when(pl.program_id(1) == 0)
    def _():
        ext_scr[:, 0:hist, :] = jnp.zeros((2 * n_pl, hist, LANES), F32)
        c_scr[...] = jnp.zeros(c_scr.shape, F32)
        n_scr[...] = jnp.zeros(n_scr.shape, F32)
        m_scr[...] = jnp.full(m_scr.shape, STAB_INIT, F32)

    @pl.when(pl.program_id(1) != 0)
    def _():
        ext_scr[:, 0:hist, :] = ext_scr[:, ln:ln + hist, :]

    for c in range(n_pl):
        cols = slice(c * LANES, (c + 1) * LANES)
        ext_scr[c, hist:hist + ln, :] = q_ref[:, cols].astype(F32)
        ext_scr[n_pl + c, hist:hist + ln, :] = k_ref[:, cols].astype(F32)

    gts = gates_ref[...] + gbias_ref[...]
    lf = _log_sigmoid(gts)
    row = lax.broadcasted_iota(I32, (ln, ln), 0)
    col = lax.broadcasted_iota(I32, (ln, ln), 1)
    causal = row >= col
    tri = causal.astype(F32)
    bcum = jnp.dot(tri, lf, preferred_element_type=F32, precision=HIGHEST)
    bcum_t = bcum.T
    gts_t = gts.T

    def short_conv(c0):
        parts = []
        for c in range(c0 // LANES, (c0 + dh) // LANES):
            cols = slice(c * LANES, (c + 1) * LANES)
            acc = jnp.zeros((ln, LANES), F32) + cb_ref[:, cols]
            for j in range(qk_width):
                s = hist - (qk_width - 1) + j
                acc = acc + cw_ref[j:j + 1, cols] * ext_scr[c, s:s + ln, :]
            parts.append(acc)
        return _silu(jnp.concatenate(parts, axis=1))

    for h in range(nh):
        c0 = h * dh
        q = short_conv(c0)
        k = short_conv(dm + c0) * (dh ** -0.5)
        qb = q.astype(BF16)
        kb = k.astype(BF16)
        v = v_ref[:, c0:c0 + dh]
        vf = v.astype(F32)

        b_col = bcum[:, nh + h:nh + h + 1]
        b_row = bcum_t[nh + h:nh + h + 1, :]
        ig_col = gts[:, h:h + 1]
        ig_row = gts_t[h:h + 1, :]
        m_prev = m_scr[h, 0:1, 0:1]

        dmat = jnp.where(causal, b_col - b_row + ig_row, NEG_INF)
        m_inter = b_col + m_prev
        m_t = jnp.maximum(m_inter, jnp.max(dmat, axis=1, keepdims=True))
        pmat = jnp.exp(dmat - m_t)
        qk = lax.dot_general(qb, kb, (((1,), (1,)), ((), ())), preferred_element_type=F32)
        s = qk * pmat
        a = jnp.exp(m_inter - m_t)
        inter = jnp.dot(qb, c_scr[h].astype(BF16), preferred_element_type=F32)
        num = jnp.dot(s.astype(BF16), v, preferred_element_type=F32) + a * inter
        den = jnp.sum(s, axis=1, keepdims=True) + a * jnp.sum(q * n_scr[h], axis=1, keepdims=True)
        hv = num / jnp.maximum(jnp.abs(den), jnp.exp(-m_t))

        mu = jnp.mean(hv, axis=-1, keepdims=True)
        hc = hv - mu
        var = jnp.mean(hc * hc, axis=-1, keepdims=True)
        hn = hc * lax.rsqrt(var + EPS) * ng_ref[:, c0:c0 + dh]
        og = og_ref[:, c0:c0 + dh].astype(F32)
        o_ref[:, c0:c0 + dh] = (_sigmoid(og) * hn).astype(o_ref.dtype)

        b_last = b_col[ln - 1:ln, :]
        g_col = b_last - b_col + ig_col
        m_new = jnp.maximum(b_last + m_prev, jnp.max(g_col, axis=0, keepdims=True))
        wk = jnp.exp(g_col - m_new)
        decay = jnp.exp(b_last + m_prev - m_new)
        upd = lax.dot_general(kb, (vf * wk).astype(BF16), (((0,), (0,)), ((), ())),
                              preferred_element_type=F32)
        c_scr[h] = decay * c_scr[h] + upd
        n_scr[h] = decay * n_scr[h] + jnp.sum(k * wk, axis=0, keepdims=True)
        m_scr[h] = jnp.broadcast_to(m_new, m_scr.shape[1:])


def _mlstm(proj, gates, gbias, qk_conv_w, qk_conv_b, norm_g, bsz, seq, dm, dh, col0):
    ln = MLSTM_CHUNK
    nc = seq // ln
    nh = dm // dh
    width = qk_conv_w.shape[0]
    blk = lambda off: pl.BlockSpec((ln, dm), lambda b, c: (b * nc + c, col0 + off))
    return pl.pallas_call(
        functools.partial(_mlstm_kernel, dh=dh, qk_width=width),
        grid=(bsz, nc),
        in_specs=[blk(0), blk(1), blk(2), blk(3),
                  pl.BlockSpec((ln, LANES), lambda b, c: (b * nc + c, 0)),
                  pl.BlockSpec((1, LANES), lambda b, c: (0, 0)),
                  pl.BlockSpec((width, 2 * dm), lambda b, c: (0, 0)),
                  pl.BlockSpec((1, 2 * dm), lambda b, c: (0, 0)),
                  pl.BlockSpec((1, dm), lambda b, c: (0, 0))],
        out_specs=pl.BlockSpec((ln, dm), lambda b, c: (b * nc + c, 0)),
        out_shape=jax.ShapeDtypeStruct((bsz * seq, dm), BF16),
        scratch_shapes=[pltpu.VMEM((2 * dm // LANES, SUBLANES + ln, LANES), F32),
                        pltpu.VMEM((nh, dh, dh), F32),
                        pltpu.VMEM((nh, 1, dh), F32),
                        pltpu.VMEM((nh, SUBLANES, LANES), F32)],
        compiler_params=_cparams(("arbitrary", "arbitrary")),
        name="mlstm",
    )(proj, proj, proj, proj, gates, gbias, qk_conv_w, qk_conv_b.reshape(1, 2 * dm),
      norm_g.reshape(1, dm))


def _route_chunk(logits_t, bias_col, n_exp):
    gsz = n_exp // N_GROUPS
    scores = _sigmoid(logits_t)
    choice = scores + bias_col
    rows_g = lax.broadcasted_iota(I32, (gsz, LANES), 0).astype(F32)
    gscore = []
    for g in range(N_GROUPS):
        v = choice[g * gsz:(g + 1) * gsz, :]
        m1 = jnp.max(v, axis=0, keepdims=True)
        i1 = jnp.min(jnp.where(v == m1, rows_g, float(gsz)), axis=0, keepdims=True)
        m2 = jnp.max(jnp.where(rows_g == i1, NEG_INF, v), axis=0, keepdims=True)
        gscore.append(m1 + m2)
    gs = jnp.concatenate(gscore, axis=0)
    giota = lax.broadcasted_iota(I32, (N_GROUPS, LANES), 0).astype(F32)
    keep = jnp.zeros((N_GROUPS, LANES), F32)
    for _ in range(TOPK_GROUPS):
        m = jnp.max(gs, axis=0, keepdims=True)
        gi = jnp.min(jnp.where(gs == m, giota, float(N_GROUPS)), axis=0, keepdims=True)
        sel = giota == gi
        keep = jnp.where(sel, 1.0, keep)
        gs = jnp.where(sel, NEG_INF, gs)
    mc = jnp.concatenate(
        [jnp.where(keep[g:g + 1, :] > 0.5, choice[g * gsz:(g + 1) * gsz, :], NEG_INF)
         for g in range(N_GROUPS)], axis=0)
    rows = lax.broadcasted_iota(I32, (n_exp, LANES), 0).astype(F32)
    idxs, ws = [], []
    picked = jnp.zeros((n_exp, LANES), F32)
    for _ in range(TOP_K):
        m = jnp.max(mc, axis=0, keepdims=True)
        idx = jnp.min(jnp.where(mc == m, rows, float(n_exp)), axis=0, keepdims=True)
        sel = rows == idx
        ws.append(jnp.sum(jnp.where(sel, scores, 0.0), axis=0, keepdims=True))
        idxs.append(idx)
        picked = jnp.where(sel, 1.0, picked)
        mc = jnp.where(sel, NEG_INF, mc)
    w = jnp.concatenate(ws, axis=0)
    w = w / jnp.sum(w, axis=0, keepdims=True) * ROUTED_SCALE
    return jnp.concatenate(idxs, axis=0).astype(I32), w, picked


def _outproj_kernel(u_ref, hm_ref, x_ref, gt_ref, sc_ref, sh_ref, g_ref, w_ref, wrh_ref, wrl_ref,
                    rb_ref, x1_ref, hnp_ref, idx_ref, tw_ref, cnt_ref):
    tm, d = x_ref.shape
    dc = u_ref.shape[1]
    half = d // 2
    n_exp = wrh_ref.shape[1]

    @pl.when(pl.program_id(0) == 0)
    def _():
        cnt_ref[...] = jnp.zeros(cnt_ref.shape, F32)

    mix = (jnp.dot(u_ref[...], w_ref[0:dc, :], preferred_element_type=F32)
           + jnp.dot(hm_ref[...], w_ref[dc:, :], preferred_element_type=F32))
    x1 = x_ref[...] + gt_ref[...] * mix
    x1_ref[...] = x1
    ms = jnp.mean(x1 * x1, axis=-1, keepdims=True)
    hn = (x1 * lax.rsqrt(ms + EPS) * g_ref[...]) * (1.0 + sc_ref[...]) + sh_ref[...]
    _store_rows(hnp_ref, _pack_halves(hn[:, :half], hn[:, half:]))
    hn_hi = hn.astype(BF16)
    hn_lo = (hn - hn_hi.astype(F32)).astype(BF16)
    dot = lambda a, b: jnp.dot(a, b, preferred_element_type=F32)
    logits = (dot(hn_hi, wrh_ref[...]) + dot(hn_lo, wrh_ref[...])) + dot(hn_hi, wrl_ref[...])
    cnt = cnt_ref[...]
    for c0 in range(0, tm, LANES):
        idx, w, picked = _route_chunk(logits[c0:c0 + LANES, :].T, rb_ref[...], n_exp)
        idx_ref[:, c0:c0 + LANES] = idx
        tw_ref[:, c0:c0 + LANES] = w
        cnt = cnt + picked
    cnt_ref[...] = cnt


def _outproj(u, hm, x2, gt1, sc2, sh2, g_ffn, w_out, w_router, router_bias, seq, tm=512):
    t, d = x2.shape
    dc = u.shape[1]
    n_exp = w_router.shape[1]
    tpb = seq // tm
    nt = t // tm
    wr_hi = w_router.astype(BF16)
    wr_lo = (w_router - wr_hi.astype(F32)).astype(BF16)
    per_b = lambda: pl.BlockSpec((None, 1, d), lambda i: (i // tpb, 0, 0))
    const = lambda shape: pl.BlockSpec(shape, lambda i: (0, 0), pipeline_mode=pl.Buffered(1))
    return pl.pallas_call(
        _outproj_kernel,
        grid=(nt,),
        in_specs=[pl.BlockSpec((tm, dc), lambda i: (i, 0)),
                  pl.BlockSpec((tm, d - dc), lambda i: (i, 0)),
                  pl.BlockSpec((tm, d), lambda i: (i, 0)),
                  per_b(), per_b(), per_b(),
                  pl.BlockSpec((1, d), lambda i: (0, 0)),
                  const((d, d)), const((d, n_exp)), const((d, n_exp)),
                  pl.BlockSpec((n_exp, 1), lambda i: (0, 0))],
        out_specs=[pl.BlockSpec((tm, d), lambda i: (i, 0)),
                   pl.BlockSpec((tm * (d // 2 // LANES), LANES), lambda i: (i, 0)),
                   pl.BlockSpec((TOP_K, tm), lambda i: (0, i)),
                   pl.BlockSpec((TOP_K, tm), lambda i: (0, i)),
                   pl.BlockSpec((n_exp, LANES), lambda i: (0, 0))],
        out_shape=[jax.ShapeDtypeStruct((t, d), F32),
                   jax.ShapeDtypeStruct((t * (d // 2 // LANES), LANES), U32),
                   jax.ShapeDtypeStruct((TOP_K, t), I32),
                   jax.ShapeDtypeStruct((TOP_K, t), F32),
                   jax.ShapeDtypeStruct((n_exp, LANES), F32)],
        compiler_params=_cparams(("arbitrary",)),
        name="outproj_route",
    )(u, hm, x2, gt1, sc2, sh2, g_ffn, w_out, wr_hi, wr_lo, router_bias.reshape(n_exp, 1))


def _plan_kernel(idx_ref, start_ref, dest_ref, run_scr):
    n_exp = start_ref.shape[0]
    tp = idx_ref.shape[1]

    @pl.when(pl.program_id(0) == 0)
    def _():
        run_scr[...] = start_ref[...]

    rows = lax.broadcasted_iota(I32, (n_exp, tp), 0)
    earlier = (lax.broadcasted_iota(I32, (tp, tp), 0)
               < lax.broadcasted_iota(I32, (tp, tp), 1)).astype(BF16)
    base = run_scr[...]
    for k in range(TOP_K):
        oh = rows == idx_ref[k:k + 1, :]
        ohf = oh.astype(F32)
        pref = jnp.dot(ohf.astype(BF16), earlier, preferred_element_type=F32)
        dest = jnp.sum(jnp.where(oh, pref + base, 0.0), axis=0, keepdims=True)
        dest_ref[k:k + 1, :] = dest.astype(I32)
        base = base + jnp.sum(ohf, axis=1, keepdims=True)
    run_scr[...] = base


def _plan(top_idx_t, starts_col, tp=512):
    k, t = top_idx_t.shape
    n_exp = starts_col.shape[0]
    return pl.pallas_call(
        _plan_kernel,
        grid=(t // tp,),
        in_specs=[pl.BlockSpec((k, tp), lambda i: (0, i)),
                  pl.BlockSpec((n_exp, 1), lambda i: (0, 0))],
        out_specs=pl.BlockSpec((k, tp), lambda i: (0, i)),
        out_shape=jax.ShapeDtypeStruct((k, t), I32),
        scratch_shapes=[pltpu.VMEM((n_exp, 1), F32)],
        compiler_params=_cparams(("arbitrary",)),
        name="plan",
    )(top_idx_t, starts_col)


def _dispatch_kernel(dest_hbm, hnp_ref, xs_hbm, idx_s, zero_scr, sem_i, sem_r, *, spr):
    i = pl.program_id(0)
    nt = pl.num_programs(0)
    td = hnp_ref.shape[0] // spr
    n_pick = idx_s.shape[1]

    def idx_copy(j):
        return pltpu.make_async_copy(dest_hbm.at[:, pl.ds(j * td, td)], idx_s.at[j % 2], sem_i.at[j % 2])

    @pl.when(i == 0)
    def _():
        idx_copy(0).start()

    @pl.when(i + 1 < nt)
    def _():
        idx_copy(i + 1).start()

    idx_copy(i).wait()
    slot = i % 2

    def body(t, carry):
        src = _row_slab(hnp_ref, t, spr)
        for k in range(n_pick):
            dst = idx_s[slot, k, t]
            pltpu.make_async_copy(src, _row_slab(xs_hbm, dst, spr), sem_r.at[0]).start(priority=k % 2)
        return carry

    lax.fori_loop(0, td, body, 0)

    for _ in range(n_pick):
        pltpu.make_async_copy(hnp_ref, _row_slab(xs_hbm, 0, spr, td), sem_r.at[0]).wait()

    @pl.when(i == nt - 1)
    def _():
        zero_scr[...] = jnp.zeros(zero_scr.shape, zero_scr.dtype)
        nz = zero_scr.shape[0]
        tail = pltpu.make_async_copy(zero_scr, xs_hbm.at[pl.ds(xs_hbm.shape[0] - nz, nz), :], sem_r.at[0])
        tail.start()
        tail.wait()


def _dispatch(dest, hnp, n_rows, tail_rows, spr, td=256):
    t = hnp.shape[0] // spr
    k = dest.shape[0]
    return pl.pallas_call(
        functools.partial(_dispatch_kernel, spr=spr),
        grid=(t // td,),
        in_specs=[pl.BlockSpec(memory_space=pl.ANY),
                  pl.BlockSpec((td * spr, LANES), lambda i: (i, 0))],
        out_specs=pl.BlockSpec(memory_space=pl.ANY),
        out_shape=jax.ShapeDtypeStruct((n_rows * spr, LANES), U32),
        scratch_shapes=[pltpu.SMEM((2, k, td), I32),
                        pltpu.VMEM((tail_rows * spr, LANES), U32),
                        pltpu.SemaphoreType.DMA((2,)),
                        pltpu.SemaphoreType.DMA((1,))],
        compiler_params=_cparams(("arbitrary",)),
        name="dispatch",
    )(dest, hnp)


_X_SLOTS = 4
_W_SLOTS = 3


def _block_offsets(sizes, starts, mb, n_blocks):
    nblk = (sizes + mb - 1) // mb
    blk_end = jnp.cumsum(nblk)
    bstart = blk_end - nblk
    b = jnp.arange(n_blocks, dtype=I32)
    owner = (blk_end[None, :] > b[:, None]) & (bstart[None, :] <= b[:, None])
    base = jnp.sum(jnp.where(owner, (starts - bstart * mb)[None, :], 0), axis=1)
    off = jnp.where(b < blk_end[-1], base + b * mb, 0)
    return off.astype(I32), nblk.astype(I32), bstart.astype(I32), blk_end[-1:].astype(I32)


def _moe_kernel(off_ref, nblk_ref, bstart_ref, nu_ref, xs_hbm, wgu_hbm, wd_hbm, ys_hbm,
                xbuf, ybuf, wgu_f, wd_f, wgu_b, wd_b, sem_x, sem_y, sem_w):
    e = pl.program_id(0)
    n_exp = pl.num_programs(0)
    n_used = nu_ref[0]
    half = wgu_b.shape[0] // 2
    spr = half // LANES
    mb = ybuf.shape[1] // spr
    de = wd_b.shape[0]
    ahead = _X_SLOTS - 2

    def x_copy(b):
        slot = b % _X_SLOTS
        return pltpu.make_async_copy(_row_slab(xs_hbm, off_ref[b], spr, mb), xbuf.at[slot], sem_x.at[slot])

    def y_copy(off, slot):
        return pltpu.make_async_copy(ybuf.at[slot], _row_slab(ys_hbm, off, spr, mb), sem_y.at[slot])

    def w_copies(j):
        slot = j % _W_SLOTS
        return (pltpu.make_async_copy(wgu_hbm.at[j], wgu_f.at[slot], sem_w.at[0, slot]),
                pltpu.make_async_copy(wd_hbm.at[j], wd_f.at[slot], sem_w.at[1, slot]))

    @pl.when(e == 0)
    def _():
        for j in range(_W_SLOTS - 1):
            for cp in w_copies(j):
                pl.when(j < n_exp)(functools.partial(cp.start, priority=1))
        for j in range(ahead + 1):
            pl.when(j < n_used)(x_copy(j).start)
        ybuf[1] = jnp.zeros(ybuf.shape[1:], ybuf.dtype)
        y_copy(ys_hbm.shape[0] // spr - mb, 1).start()

    @pl.when(e + _W_SLOTS - 1 < n_exp)
    def _():
        for cp in w_copies(e + _W_SLOTS - 1):
            cp.start(priority=1)

    nb = nblk_ref[e]
    b0 = bstart_ref[e]
    wslot = e % _W_SLOTS
    copy_gu, copy_d = w_copies(e)

    copy_gu.wait()

    @pl.when(nb > 0)
    def _():
        wgu_b[...] = wgu_f[wslot].astype(BF16)

    copy_d.wait()

    @pl.when(nb > 0)
    def _():
        wd_b[...] = wd_f[wslot].astype(BF16)

        def prefetch(b):
            @pl.when(b + ahead + 1 < n_used)
            def _():
                x_copy(b + ahead + 1).start()

        def block(b):
            x_copy(b).wait()
            lo, hi = _unpack_halves(_load_rows(xbuf.at[b % _X_SLOTS], spr))
            gu = (jnp.dot(lo.astype(BF16), wgu_b[0:half, :], preferred_element_type=F32)
                  + jnp.dot(hi.astype(BF16), wgu_b[half:, :], preferred_element_type=F32))
            act = _silu(gu[:, :de]) * gu[:, de:]
            y = jnp.dot(act.astype(BF16), wd_b[...], preferred_element_type=F32)
            yslot = b % 2
            _store_rows(ybuf.at[yslot], _pack_halves(y[:, :half], y[:, half:]))

            y_copy(0, 1 - yslot).wait()
            y_copy(off_ref[b], yslot).start()

        def body(i, carry):
            prefetch(b0 + i)
            block(b0 + i)
            return carry

        lax.fori_loop(0, nb, body, 0)

    @pl.when(e == n_exp - 1)
    def _():
        last = (n_used - 1) % 2
        y_copy(0, last).wait()
        tail = y_copy(ys_hbm.shape[0] // spr - mb, last)
        tail.start()
        tail.wait()


def _moe(xs, block_off, nblk, bstart, n_used, w_gate_up, w_down, mb):
    n_exp, d, de2 = w_gate_up.shape
    s = d // 2 // LANES
    de = w_down.shape[1]
    grid_spec = pltpu.PrefetchScalarGridSpec(
        num_scalar_prefetch=4,
        grid=(n_exp,),
        in_specs=[pl.BlockSpec(memory_space=pl.ANY),
                  pl.BlockSpec(memory_space=pl.ANY),
                  pl.BlockSpec(memory_space=pl.ANY)],
        out_specs=pl.BlockSpec(memory_space=pl.ANY),
        scratch_shapes=[pltpu.VMEM((_X_SLOTS, mb * s, LANES), U32),
                        pltpu.VMEM((2, mb * s, LANES), U32),
                        pltpu.VMEM((_W_SLOTS, d, de2), F32),
                        pltpu.VMEM((_W_SLOTS, de, d), F32),
                        pltpu.VMEM((d, de2), BF16),
                        pltpu.VMEM((de, d), BF16),
                        pltpu.SemaphoreType.DMA((_X_SLOTS,)),
                        pltpu.SemaphoreType.DMA((2,)),
                        pltpu.SemaphoreType.DMA((2, _W_SLOTS))])
    return pl.pallas_call(
        _moe_kernel,
        grid_spec=grid_spec,
        out_shape=jax.ShapeDtypeStruct(xs.shape, U32),
        compiler_params=_cparams(("arbitrary",)),
        name="moe",
    )(block_off, nblk, bstart, n_used, xs, w_gate_up, w_down)


def _combine_kernel(dest_hbm, ys_hbm, tw_ref, x1_ref, hnp_ref, gt_ref, gf_ref, wgu_ref, wd_ref,
                    o_ref, idx_s, gbuf, y_scr, w_scr, sem_i, sem_g):
    i = pl.program_id(0)
    nt = pl.num_programs(0)
    tc, d = x1_ref.shape
    half = d // 2
    spr = half // LANES
    de = wd_ref.shape[0]
    n_pick = tw_ref.shape[0]
    grp = SUBLANES

    def idx_copy(j):
        return pltpu.make_async_copy(dest_hbm.at[:, pl.ds(j * tc, tc)], idx_s.at[j % 3], sem_i.at[j % 3])

    def issue_rows(islot, gslot, t0):
        for j in range(grp):
            t = t0 + j
            for k in range(n_pick):
                src = idx_s[islot, k, t]
                pltpu.make_async_copy(_row_slab(ys_hbm, src, spr), _row_slab(gbuf.at[gslot, k], t, spr),
                                      sem_g.at[gslot]).start(priority=k % 2)

    def gather_wait(gslot):
        for k in range(n_pick):
            pltpu.make_async_copy(_row_slab(ys_hbm, 0, spr, tc), gbuf.at[gslot, k], sem_g.at[gslot]).wait()

    @pl.when(i == 0)
    def _():
        idx_copy(0).start()
        idx_copy(0).wait()

        def first(g, carry):
            issue_rows(0, 0, pl.multiple_of(g * grp, grp))
            return carry

        lax.fori_loop(0, tc // grp, first, 0)

        @pl.when(nt > 1)
        def _():
            idx_copy(1).start()

    @pl.when(i + 2 < nt)
    def _():
        idx_copy(i + 2).start()

    @pl.when(i + 1 < nt)
    def _():
        idx_copy(i + 1).wait()

    lo, hi = _unpack_halves(_load_rows(hnp_ref, spr))
    gu = (jnp.dot(lo.astype(BF16), wgu_ref[0:half, :], preferred_element_type=F32)
          + jnp.dot(hi.astype(BF16), wgu_ref[half:, :], preferred_element_type=F32))
    act = _silu(gu[:, :de]) * gu[:, de:]
    y_scr[...] = jnp.dot(act.astype(BF16), wd_ref[...], preferred_element_type=F32)

    tw = jnp.concatenate([tw_ref[...], jnp.zeros((LANES - n_pick, tc), F32)], axis=0)
    w_scr[...] = jnp.concatenate([tw[:, c0:c0 + LANES].T for c0 in range(0, tc, LANES)], axis=0)

    gslot = i % 2
    gather_wait(gslot)
    has_next = i + 1 < nt
    nxt_islot = jnp.where(has_next, (i + 1) % 3, i % 3)
    nxt_gslot = (i + 1) % 2
    gt = gt_ref[...]
    gf = gf_ref[...]

    def body(g, carry):
        t0 = pl.multiple_of(g * grp, grp)
        rows = pl.ds(t0, grp)
        acc_lo = y_scr[rows, 0:half]
        acc_hi = y_scr[rows, half:]
        wrow = w_scr[rows, :]
        for k in range(n_pick):
            src = gbuf.at[gslot, k]
            words = jnp.concatenate(
                [src[pl.ds(t0 * spr + s, grp, stride=spr), :] for s in range(spr)], axis=1)
            rlo, rhi = _unpack_halves(words)
            wk = wrow[:, k:k + 1]
            acc_lo = acc_lo + wk * rlo
            acc_hi = acc_hi + wk * rhi
        x2 = x1_ref[rows, :] + gt * jnp.concatenate([acc_lo, acc_hi], axis=1)
        ms = jnp.mean(x2 * x2, axis=-1, keepdims=True)
        o_ref[rows, :] = x2 * lax.rsqrt(ms + EPS) * gf
        issue_rows(nxt_islot, nxt_gslot, t0)
        return carry

    lax.fori_loop(0, tc // grp, body, 0)

    @pl.when(i == nt - 1)
    def _():
        gather_wait(nxt_gslot)


def _combine(dest, ys, top_w_t, x1, hnp, gt2, g_final, ws_gu, ws_d, seq, tc=128):
    t, d = x1.shape
    half = d // 2
    k = dest.shape[0]
    tpb = seq // tc
    de2 = ws_gu.shape[1]
    return pl.pallas_call(
        _combine_kernel,
        grid=(t // tc,),
        in_specs=[pl.BlockSpec(memory_space=pl.ANY),
                  pl.BlockSpec(memory_space=pl.ANY),
                  pl.BlockSpec((k, tc), lambda i: (0, i)),
                  pl.BlockSpec((tc, d), lambda i: (i, 0)),
                  pl.BlockSpec((tc * (half // LANES), LANES), lambda i: (i, 0)),
                  pl.BlockSpec((None, 1, d), lambda i: (i // tpb, 0, 0)),
                  pl.BlockSpec((1, d), lambda i: (0, 0)),
                  pl.BlockSpec((d, de2), lambda i: (0, 0)),
                  pl.BlockSpec((de2 // 2, d), lambda i: (0, 0))],
        out_specs=pl.BlockSpec((tc, d), lambda i: (i, 0)),
        out_shape=jax.ShapeDtypeStruct((t, d), F32),
        scratch_shapes=[pltpu.SMEM((3, k, tc), I32),
                        pltpu.VMEM((2, k, tc * (half // LANES), LANES), U32),
                        pltpu.VMEM((tc, d), F32),
                        pltpu.VMEM((tc, LANES), F32),
                        pltpu.SemaphoreType.DMA((3,)),
                        pltpu.SemaphoreType.DMA((2,))],
        compiler_params=_cparams(("arbitrary",)),
        name="combine",
    )(dest, ys, top_w_t, x1, hnp, gt2, g_final.reshape(1, d), ws_gu, ws_d)


def kernel(x, c, w_ada, b_ada, g_mix, w_in, conv_w, conv_b, conv_ln_g, conv_ln_b, qk_conv_w, qk_conv_b, b_igate, b_fgate, mlstm_norm_g, w_out, g_ffn, w_router, router_bias, w_gate_up, w_down, ws_gate_up, ws_down, g_final):
    bsz, seq, d = x.shape
    depth = w_ada.shape[0]
    assert depth == 1, "single-layer block"
    dc = conv_w.shape[2]
    dm = d - dc
    nh = b_igate.shape[1]
    dh = dm // nh
    n_exp = w_router.shape[2]
    t = bsz * seq
    n_main = 2 * dc + 4 * dm
    n_assign = TOP_K * t
    mb = MOE_BLOCK
    assert dc == dm and nh == N_HEADS and 2 * nh <= LANES and n_assign % mb == 0

    mod = _ada(c, w_ada[0], b_ada[0])
    sh1, sc1, gt1, sh2, sc2, gt2 = [m.reshape(bsz, 1, d) for m in jnp.split(mod, 6, axis=-1)]

    x2 = x.reshape(t, d)
    w_main = w_in[0, :, :n_main].astype(BF16)
    w_gate = jnp.zeros((d, LANES), BF16).at[:, :2 * nh].set(w_in[0, :, n_main:].astype(BF16))
    proj, gates = _inproj(x2, sc1, sh1, g_mix[0].reshape(1, d), w_main, w_gate, seq)

    u = _convmod(proj, conv_w[0], conv_b[0], conv_ln_g[0], conv_ln_b[0], bsz, seq)

    gbias = jnp.zeros((1, LANES), F32).at[0, :nh].set(b_igate[0]).at[0, nh:2 * nh].set(b_fgate[0])
    hm = _mlstm(proj, gates, gbias, qk_conv_w[0], qk_conv_b[0], mlstm_norm_g[0],
                bsz, seq, dm, dh, col0=2 * dc // dm)

    x1, hnp, top_idx_t, top_w_t, cnt = _outproj(
        u, hm, x2, gt1, sc2, sh2, g_ffn[0].reshape(1, d), w_out[0].astype(BF16),
        w_router[0], router_bias[0], seq)

    sizes = jnp.sum(cnt, axis=1).astype(I32)
    starts = jnp.cumsum(sizes) - sizes
    dest = _plan(top_idx_t, starts.astype(F32).reshape(n_exp, 1))
    xs = _dispatch(dest, hnp, n_assign + mb, mb, d // 2 // LANES)
    block_off, nblk, bstart, n_used = _block_offsets(sizes, starts, mb, n_assign // mb + n_exp)
    ys = _moe(xs, block_off, nblk, bstart, n_used, w_gate_up[0], w_down[0], mb)

    out = _combine(dest, ys, top_w_t, x1, hnp, gt2, g_final,
                   ws_gate_up[0].astype(BF16), ws_down[0].astype(BF16), seq)
    return out.reshape(bsz, seq, d)
```

```python
import functools

import jax
import jax.numpy as jnp
from jax import lax
from jax.experimental import pallas as pl
from jax.experimental.pallas import tpu as pltpu

F32 = jnp.float32
BF16 = jnp.bfloat16
U32 = jnp.uint32
I32 = jnp.int32

EPS = 1e-6
STAB_INIT = -1e30
TOP_K = 8
N_GROUPS = 8
TOPK_GROUPS = 4
ROUTED_SCALE = 2.5
N_HEADS = 4

LANES = 128
SUBLANES = 8
VMEM_LIMIT = 56 * 1024 * 1024

MLSTM_CHUNK = 256
MOE_BLOCK = 128
NEG_INF = float("-inf")
HIGHEST = lax.Precision.HIGHEST


def _cparams(sem):
    return pltpu.CompilerParams(dimension_semantics=sem, vmem_limit_bytes=VMEM_LIMIT)


def _sigmoid(x):
    return 1.0 / (1.0 + jnp.exp(-x))


def _silu(x):
    return x * _sigmoid(x)


def _pack_halves(lo, hi):
    lo_b = lax.bitcast_convert_type(lo.astype(BF16).astype(F32), U32)
    hi_b = lax.bitcast_convert_type(hi.astype(BF16).astype(F32), U32)
    return (hi_b & jnp.uint32(0xFFFF0000)) | (lo_b >> 16)


def _unpack_halves(w):
    lo = lax.bitcast_convert_type(w << 16, F32)
    hi = lax.bitcast_convert_type(w & jnp.uint32(0xFFFF0000), F32)
    return lo, hi


def _row_slab(ref, r, s_per_row, n=1):
    return ref.at[pl.ds(pl.multiple_of(r * s_per_row, s_per_row), n * s_per_row), :]


def _store_rows(ref, words):
    m = words.shape[0]
    s_per_row = words.shape[1] // LANES
    for s in range(s_per_row):
        ref[pl.ds(s, m, stride=s_per_row), :] = words[:, s * LANES:(s + 1) * LANES]


def _load_rows(ref, s_per_row):
    m = ref.shape[0] // s_per_row
    return jnp.concatenate([ref[pl.ds(s, m, stride=s_per_row), :] for s in range(s_per_row)], axis=1)


def _ada_kernel(c_ref, w_ref, b_ref, o_ref):
    cs = _silu(c_ref[...])
    cs_hi = cs.astype(BF16)
    cs_lo = (cs - cs_hi.astype(F32)).astype(BF16)
    w = w_ref[...].astype(BF16)
    o_ref[...] = (jnp.dot(cs_hi, w, preferred_element_type=F32)
                  + jnp.dot(cs_lo, w, preferred_element_type=F32)) + b_ref[...]


def _ada(c, w_ada, b_ada):
    bsz, d = c.shape
    n = w_ada.shape[1]
    tn = 1536
    rows = 2 * SUBLANES
    cp = jnp.zeros((rows, d), F32).at[:bsz].set(c)
    out = pl.pallas_call(
        _ada_kernel,
        grid=(n // tn,),
        in_specs=[pl.BlockSpec((rows, d), lambda j: (0, 0)),
                  pl.BlockSpec((d, tn), lambda j: (0, j)),
                  pl.BlockSpec((1, tn), lambda j: (0, j))],
        out_specs=pl.BlockSpec((rows, tn), lambda j: (0, j)),
        out_shape=jax.ShapeDtypeStruct((rows, n), F32),
        compiler_params=_cparams(("arbitrary",)),
        name="ada",
    )(cp, w_ada, b_ada.reshape(1, n))
    return out[:bsz]


_NORM_ROWS = 256


def _inproj_kernel(x_ref, sc_ref, sh_ref, g_ref, w_ref, wg_ref, o_ref, og_ref, h_scr):
    @pl.when(pl.program_id(1) == 0)
    def _():
        def rows(c, carry):
            r0 = pl.multiple_of(c * _NORM_ROWS, _NORM_ROWS)
            x = x_ref[pl.ds(r0, _NORM_ROWS), :]
            ms = jnp.mean(x * x, axis=-1, keepdims=True)
            h = (x * lax.rsqrt(ms + EPS) * g_ref[...]) * (1.0 + sc_ref[...]) + sh_ref[...]
            hb = h.astype(BF16)
            h_scr[pl.ds(r0, _NORM_ROWS), :] = hb
            og_ref[pl.ds(r0, _NORM_ROWS), :] = jnp.dot(hb, wg_ref[...], preferred_element_type=F32)
            return carry

        lax.fori_loop(0, x_ref.shape[0] // _NORM_ROWS, rows, 0)

    o_ref[...] = jnp.dot(h_scr[...], w_ref[...], preferred_element_type=F32).astype(o_ref.dtype)


def _inproj(x2, sc, sh, g, w_main, w_gate, seq, tm=1024, tn=1536):
    t, d = x2.shape
    n = w_main.shape[1]
    tpb = seq // tm
    return pl.pallas_call(
        _inproj_kernel,
        grid=(t // tm, n // tn),
        in_specs=[pl.BlockSpec((tm, d), lambda i, j: (i, 0)),
                  pl.BlockSpec((None, 1, d), lambda i, j: (i // tpb, 0, 0)),
                  pl.BlockSpec((None, 1, d), lambda i, j: (i // tpb, 0, 0)),
                  pl.BlockSpec((1, d), lambda i, j: (0, 0)),
                  pl.BlockSpec((d, tn), lambda i, j: (0, j)),
                  pl.BlockSpec((d, LANES), lambda i, j: (0, 0))],
        out_specs=[pl.BlockSpec((tm, tn), lambda i, j: (i, j)),
                   pl.BlockSpec((tm, LANES), lambda i, j: (i, 0))],
        out_shape=[jax.ShapeDtypeStruct((t, n), BF16),
                   jax.ShapeDtypeStruct((t, LANES), F32)],
        scratch_shapes=[pltpu.VMEM((tm, d), BF16)],
        compiler_params=_cparams(("arbitrary", "arbitrary")),
        name="inproj",
    )(x2, sc, sh, g, w_main, w_gate)


_CONV_RC = 128
_CONV_HIST = 32


def _convmod_kernel(a_ref, gate_ref, w_ref, b_ref, lg_ref, lb_ref, o_ref, ext_scr, y_scr, *, width):
    tt, ch = a_ref.shape
    hist = _CONV_HIST
    rc = _CONV_RC

    @pl.when(pl.program_id(1) == 0)
    def _():
        ext_scr[:, 0:hist, :] = jnp.zeros((ch // LANES, hist, LANES), F32)

    @pl.when(pl.program_id(1) != 0)
    def _():
        ext_scr[:, 0:hist, :] = ext_scr[:, tt:tt + hist, :]

    for c in range(ch // LANES):
        cols = slice(c * LANES, (c + 1) * LANES)
        ext_scr[c, hist:hist + tt, :] = a_ref[:, cols].astype(F32) * _sigmoid(gate_ref[:, cols].astype(F32))

    off0 = hist - (width - 1)
    n_rb = tt // rc
    n_acc = rc // SUBLANES

    def block(it, carry):
        c = it // n_rb
        r0 = pl.multiple_of((it % n_rb) * rc, rc)
        acc = [jnp.broadcast_to(b_ref[c], (SUBLANES, LANES))] * n_acc
        for j in range(width):
            wj = jnp.broadcast_to(w_ref[c, j:j + 1, :], (SUBLANES, LANES))
            for i in range(n_acc):
                acc[i] = acc[i] + wj * ext_scr[c, pl.ds(r0 + off0 + j + i * SUBLANES, SUBLANES), :]
        for i in range(n_acc):
            y_scr[c, pl.ds(r0 + i * SUBLANES, SUBLANES), :] = acc[i]
        return carry

    lax.fori_loop(0, (ch // LANES) * n_rb, block, 0)

    y = jnp.concatenate([y_scr[c] for c in range(ch // LANES)], axis=1)
    mu = jnp.mean(y, axis=-1, keepdims=True)
    yc = y - mu
    var = jnp.mean(yc * yc, axis=-1, keepdims=True)
    z = yc * lax.rsqrt(var + EPS) * lg_ref[...] + lb_ref[...]
    o_ref[...] = _silu(z).astype(o_ref.dtype)


def _convmod(proj, conv_w, conv_b, ln_g, ln_b, bsz, seq, tt=512):
    width, ch = conv_w.shape
    assert width - 1 <= _CONV_HIST and tt % _CONV_RC == 0
    tpb = seq // tt
    n_pl = ch // LANES
    vec = lambda: pl.BlockSpec((1, ch), lambda b, i: (0, 0))
    w_pl = conv_w.reshape(width, n_pl, LANES).transpose(1, 0, 2)
    b_pl = conv_b.reshape(n_pl, 1, LANES)
    return pl.pallas_call(
        functools.partial(_convmod_kernel, width=width),
        grid=(bsz, tpb),
        in_specs=[pl.BlockSpec((tt, ch), lambda b, i: (b * tpb + i, 0)),
                  pl.BlockSpec((tt, ch), lambda b, i: (b * tpb + i, 1)),
                  pl.BlockSpec((n_pl, width, LANES), lambda b, i: (0, 0, 0)),
                  pl.BlockSpec((n_pl, 1, LANES), lambda b, i: (0, 0, 0)),
                  vec(), vec()],
        out_specs=pl.BlockSpec((tt, ch), lambda b, i: (b * tpb + i, 0)),
        out_shape=jax.ShapeDtypeStruct((bsz * seq, ch), BF16),
        scratch_shapes=[pltpu.VMEM((n_pl, tt + _CONV_HIST, LANES), F32),
                        pltpu.VMEM((n_pl, tt, LANES), F32)],
        compiler_params=_cparams(("arbitrary", "arbitrary")),
        name="convmod",
    )(proj, proj, w_pl, b_pl, ln_g.reshape(1, ch), ln_b.reshape(1, ch))


def _log_sigmoid(x):
    return -(jnp.maximum(-x, 0.0) + jnp.log(1.0 + jnp.exp(-jnp.abs(x))))


def _mlstm_kernel(q_ref, k_ref, v_ref, og_ref, gates_ref, gbias_ref, cw_ref, cb_ref, ng_ref,
                  o_ref, ext_scr, c_scr, n_scr, m_scr, *, dh, qk_width):
    ln = q_ref.shape[0]
    dm = q_ref.shape[1]
    nh = dm // dh
    hist = SUBLANES

    n_pl = dm // LANES
    @pl.when(pl.program_id(1) == 0)
    def _():
        ext_scr[:, 0:hist, :] = jnp.zeros((2 * n_pl, hist, LANES), F32)
        c_scr[...] = jnp.zeros(c_scr.shape, F32)
        n_scr[...] = jnp.zeros(n_scr.shape, F32)
        m_scr[...] = jnp.full(m_scr.shape, STAB_INIT, F32)

    @pl.when(pl.program_id(1) != 0)
    def _():
        ext_scr[:, 0:hist, :] = ext_scr[:, ln:ln + hist, :]

    for c in range(n_pl):
        cols = slice(c * LANES, (c + 1) * LANES)
        ext_scr[c, hist:hist + ln, :] = q_ref[:, cols].astype(F32)
        ext_scr[n_pl + c, hist:hist + ln, :] = k_ref[:, cols].astype(F32)

    gts = gates_ref[...] + gbias_ref[...]
    lf = _log_sigmoid(gts)
    row = lax.broadcasted_iota(I32, (ln, ln), 0)
    col = lax.broadcasted_iota(I32, (ln, ln), 1)
    causal = row >= col
    tri = causal.astype(F32)
    bcum = jnp.dot(tri, lf, preferred_element_type=F32, precision=HIGHEST)
    bcum_t = bcum.T
    gts_t = gts.T

    def short_conv(c0):
        parts = []
        for c in range(c0 // LANES, (c0 + dh) // LANES):
            cols = slice(c * LANES, (c + 1) * LANES)
            acc = jnp.zeros((ln, LANES), F32) + cb_ref[:, cols]
            for j in range(qk_width):
                s = hist - (qk_width - 1) + j
                acc = acc + cw_ref[j:j + 1, cols] * ext_scr[c, s:s + ln, :]
            parts.append(acc)
        return _silu(jnp.concatenate(parts, axis=1))

    for h in range(nh):
        c0 = h * dh
        q = short_conv(c0)
        k = short_conv(dm + c0) * (dh ** -0.5)
        qb = q.astype(BF16)
        kb = k.astype(BF16)
        v = v_ref[:, c0:c0 + dh]
        vf = v.astype(F32)

        b_col = bcum[:, nh + h:nh + h + 1]
        b_row = bcum_t[nh + h:nh + h + 1, :]
        ig_col = gts[:, h:h + 1]
        ig_row = gts_t[h:h + 1, :]
        m_prev = m_scr[h, 0:1, 0:1]

        dmat = jnp.where(causal, b_col - b_row + ig_row, NEG_INF)
        m_inter = b_col + m_prev
        m_t = jnp.maximum(m_inter, jnp.max(dmat, axis=1, keepdims=True))
        pmat = jnp.exp(dmat - m_t)
        qk = lax.dot_general(qb, kb, (((1,), (1,)), ((), ())), preferred_element_type=F32)
        s = qk * pmat
        a = jnp.exp(m_inter - m_t)
        inter = jnp.dot(qb, c_scr[h].astype(BF16), preferred_element_type=F32)
        num = jnp.dot(s.astype(BF16), v, preferred_element_type=F32) + a * inter
        den = jnp.sum(s, axis=1, keepdims=True) + a * jnp.sum(q * n_scr[h], axis=1, keepdims=True)
        hv = num / jnp.maximum(jnp.abs(den), jnp.exp(-m_t))

        mu = jnp.mean(hv, axis=-1, keepdims=True)
        hc = hv - mu
        var = jnp.mean(hc * hc, axis=-1, keepdims=True)
        hn = hc * lax.rsqrt(var + EPS) * ng_ref[:, c0:c0 + dh]
        og = og_ref[:, c0:c0 + dh].astype(F32)
        o_ref[:, c0:c0 + dh] = (_sigmoid(og) * hn).astype(o_ref.dtype)

        b_last = b_col[ln - 1:ln, :]
        g_col = b_last - b_col + ig_col
        m_new = jnp.maximum(b_last + m_prev, jnp.max(g_col, axis=0, keepdims=True))
        wk = jnp.exp(g_col - m_new)
        decay = jnp.exp(b_last + m_prev - m_new)
        upd = lax.dot_general(kb, (vf * wk).astype(BF16), (((0,), (0,)), ((), ())),
                              preferred_element_type=F32)
        c_scr[h] = decay * c_scr[h] + upd
        n_scr[h] = decay * n_scr[h] + jnp.sum(k * wk, axis=0, keepdims=True)
        m_scr[h] = jnp.broadcast_to(m_new, m_scr.shape[1:])


def _mlstm(proj, gates, gbias, qk_conv_w, qk_conv_b, norm_g, bsz, seq, dm, dh, col0):
    ln = MLSTM_CHUNK
    nc = seq // ln
    nh = dm // dh
    width = qk_conv_w.shape[0]
    blk = lambda off: pl.BlockSpec((ln, dm), lambda b, c: (b * nc + c, col0 + off))
    return pl.pallas_call(
        functools.partial(_mlstm_kernel, dh=dh, qk_width=width),
        grid=(bsz, nc),
        in_specs=[blk(0), blk(1), blk(2), blk(3),
                  pl.BlockSpec((ln, LANES), lambda b, c: (b * nc + c, 0)),
                  pl.BlockSpec((1, LANES), lambda b, c: (0, 0)),
                  pl.BlockSpec((width, 2 * dm), lambda b, c: (0, 0)),
                  pl.BlockSpec((1, 2 * dm), lambda b, c: (0, 0)),
                  pl.BlockSpec((1, dm), lambda b, c: (0, 0))],
        out_specs=pl.BlockSpec((ln, dm), lambda b, c: (b * nc + c, 0)),
        out_shape=jax.ShapeDtypeStruct((bsz * seq, dm), BF16),
        scratch_shapes=[pltpu.VMEM((2 * dm // LANES, SUBLANES + ln, LANES), F32),
                        pltpu.VMEM((nh, dh, dh), F32),
                        pltpu.VMEM((nh, 1, dh), F32),
                        pltpu.VMEM((nh, SUBLANES, LANES), F32)],
        compiler_params=_cparams(("arbitrary", "arbitrary")),
        name="mlstm",
    )(proj, proj, proj, proj, gates, gbias, qk_conv_w, qk_conv_b.reshape(1, 2 * dm),
      norm_g.reshape(1, dm))


def _route_chunk(logits_t, bias_col, n_exp):
    gsz = n_exp // N_GROUPS
    scores = _sigmoid(logits_t)
    choice = scores + bias_col
    rows_g = lax.broadcasted_iota(I32, (gsz, LANES), 0).astype(F32)
    gscore = []
    for g in range(N_GROUPS):
        v = choice[g * gsz:(g + 1) * gsz, :]
        m1 = jnp.max(v, axis=0, keepdims=True)
        i1 = jnp.min(jnp.where(v == m1, rows_g, float(gsz)), axis=0, keepdims=True)
        m2 = jnp.max(jnp.where(rows_g == i1, NEG_INF, v), axis=0, keepdims=True)
        gscore.append(m1 + m2)
    gs = jnp.concatenate(gscore, axis=0)
    giota = lax.broadcasted_iota(I32, (N_GROUPS, LANES), 0).astype(F32)
    keep = jnp.zeros((N_GROUPS, LANES), F32)
    for _ in range(TOPK_GROUPS):
        m = jnp.max(gs, axis=0, keepdims=True)
        gi = jnp.min(jnp.where(gs == m, giota, float(N_GROUPS)), axis=0, keepdims=True)
        sel = giota == gi
        keep = jnp.where(sel, 1.0, keep)
        gs = jnp.where(sel, NEG_INF, gs)
    mc = jnp.concatenate(
        [jnp.where(keep[g:g + 1, :] > 0.5, choice[g * gsz:(g + 1) * gsz, :], NEG_INF)
         for g in range(N_GROUPS)], axis=0)
    rows = lax.broadcasted_iota(I32, (n_exp, LANES), 0).astype(F32)
    idxs, ws = [], []
    picked = jnp.zeros((n_exp, LANES), F32)
    for _ in range(TOP_K):
        m = jnp.max(mc, axis=0, keepdims=True)
        idx = jnp.min(jnp.where(mc == m, rows, float(n_exp)), axis=0, keepdims=True)
        sel = rows == idx
        ws.append(jnp.sum(jnp.where(sel, scores, 0.0), axis=0, keepdims=True))
        idxs.append(idx)
        picked = jnp.where(sel, 1.0, picked)
        mc = jnp.where(sel, NEG_INF, mc)
    w = jnp.concatenate(ws, axis=0)
    w = w / jnp.sum(w, axis=0, keepdims=True) * ROUTED_SCALE
    return jnp.concatenate(idxs, axis=0).astype(I32), w, picked


def _outproj_kernel(u_ref, hm_ref, x_ref, gt_ref, sc_ref, sh_ref, g_ref, w_ref, wrh_ref, wrl_ref,
                    rb_ref, x1_ref, hnp_ref, idx_ref, tw_ref, cnt_ref):
    tm, d = x_ref.shape
    dc = u_ref.shape[1]
    half = d // 2
    n_exp = wrh_ref.shape[1]

    @pl.when(pl.program_id(0) == 0)
    def _():
        cnt_ref[...] = jnp.zeros(cnt_ref.shape, F32)

    mix = (jnp.dot(u_ref[...], w_ref[0:dc, :], preferred_element_type=F32)
           + jnp.dot(hm_ref[...], w_ref[dc:, :], preferred_element_type=F32))
    x1 = x_ref[...] + gt_ref[...] * mix
    x1_ref[...] = x1
    ms = jnp.mean(x1 * x1, axis=-1, keepdims=True)
    hn = (x1 * lax.rsqrt(ms + EPS) * g_ref[...]) * (1.0 + sc_ref[...]) + sh_ref[...]
    _store_rows(hnp_ref, _pack_halves(hn[:, :half], hn[:, half:]))
    hn_hi = hn.astype(BF16)
    hn_lo = (hn - hn_hi.astype(F32)).astype(BF16)
    dot = lambda a, b: jnp.dot(a, b, preferred_element_type=F32)
    logits = (dot(hn_hi, wrh_ref[...]) + dot(hn_lo, wrh_ref[...])) + dot(hn_hi, wrl_ref[...])
    cnt = cnt_ref[...]
    for c0 in range(0, tm, LANES):
        idx, w, picked = _route_chunk(logits[c0:c0 + LANES, :].T, rb_ref[...], n_exp)
        idx_ref[:, c0:c0 + LANES] = idx
        tw_ref[:, c0:c0 + LANES] = w
        cnt = cnt + picked
    cnt_ref[...] = cnt


def _outproj(u, hm, x2, gt1, sc2, sh2, g_ffn, w_out, w_router, router_bias, seq, tm=512):
    t, d = x2.shape
    dc = u.shape[1]
    n_exp = w_router.shape[1]
    tpb = seq // tm
    nt = t // tm
    wr_hi = w_router.astype(BF16)
    wr_lo = (w_router - wr_hi.astype(F32)).astype(BF16)
    per_b = lambda: pl.BlockSpec((None, 1, d), lambda i: (i // tpb, 0, 0))
    const = lambda shape: pl.BlockSpec(shape, lambda i: (0, 0), pipeline_mode=pl.Buffered(1))
    return pl.pallas_call(
        _outproj_kernel,
        grid=(nt,),
        in_specs=[pl.BlockSpec((tm, dc), lambda i: (i, 0)),
                  pl.BlockSpec((tm, d - dc), lambda i: (i, 0)),
                  pl.BlockSpec((tm, d), lambda i: (i, 0)),
                  per_b(), per_b(), per_b(),
                  pl.BlockSpec((1, d), lambda i: (0, 0)),
                  const((d, d)), const((d, n_exp)), const((d, n_exp)),
                  pl.BlockSpec((n_exp, 1), lambda i: (0, 0))],
        out_specs=[pl.BlockSpec((tm, d), lambda i: (i, 0)),
                   pl.BlockSpec((tm * (d // 2 // LANES), LANES), lambda i: (i, 0)),
                   pl.BlockSpec((TOP_K, tm), lambda i: (0, i)),
                   pl.BlockSpec((TOP_K, tm), lambda i: (0, i)),
                   pl.BlockSpec((n_exp, LANES), lambda i: (0, 0))],
        out_shape=[jax.ShapeDtypeStruct((t, d), F32),
                   jax.ShapeDtypeStruct((t * (d // 2 // LANES), LANES), U32),
                   jax.ShapeDtypeStruct((TOP_K, t), I32),
                   jax.ShapeDtypeStruct((TOP_K, t), F32),
                   jax.ShapeDtypeStruct((n_exp, LANES), F32)],
        compiler_params=_cparams(("arbitrary",)),
        name="outproj_route",
    )(u, hm, x2, gt1, sc2, sh2, g_ffn, w_out, wr_hi, wr_lo, router_bias.reshape(n_exp, 1))


def _plan_kernel(idx_ref, start_ref, dest_ref, run_scr):
    n_exp = start_ref.shape[0]
    tp = idx_ref.shape[1]

    @pl.when(pl.program_id(0) == 0)
    def _():
        run_scr[...] = start_ref[...]

    rows = lax.broadcasted_iota(I32, (n_exp, tp), 0)
    earlier = (lax.broadcasted_iota(I32, (tp, tp), 0)
               < lax.broadcasted_iota(I32, (tp, tp), 1)).astype(BF16)
    base = run_scr[...]
    for k in range(TOP_K):
        oh = rows == idx_ref[k:k + 1, :]
        ohf = oh.astype(F32)
        pref = jnp.dot(ohf.astype(BF16), earlier, preferred_element_type=F32)
        dest = jnp.sum(jnp.where(oh, pref + base, 0.0), axis=0, keepdims=True)
        dest_ref[k:k + 1, :] = dest.astype(I32)
        base = base + jnp.sum(ohf, axis=1, keepdims=True)
    run_scr[...] = base


def _plan(top_idx_t, starts_col, tp=512):
    k, t = top_idx_t.shape
    n_exp = starts_col.shape[0]
    return pl.pallas_call(
        _plan_kernel,
        grid=(t // tp,),
        in_specs=[pl.BlockSpec((k, tp), lambda i: (0, i)),
                  pl.BlockSpec((n_exp, 1), lambda i: (0, 0))],
        out_specs=pl.BlockSpec((k, tp), lambda i: (0, i)),
        out_shape=jax.ShapeDtypeStruct((k, t), I32),
        scratch_shapes=[pltpu.VMEM((n_exp, 1), F32)],
        compiler_params=_cparams(("arbitrary",)),
        name="plan",
    )(top_idx_t, starts_col)


def _dispatch_kernel(dest_hbm, hnp_ref, xs_hbm, idx_s, zero_scr, sem_i, sem_r, *, spr):
    i = pl.program_id(0)
    nt = pl.num_programs(0)
    td = hnp_ref.shape[0] // spr
    n_pick = idx_s.shape[1]

    def idx_copy(j):
        return pltpu.make_async_copy(dest_hbm.at[:, pl.ds(j * td, td)], idx_s.at[j % 2], sem_i.at[j % 2])

    @pl.when(i == 0)
    def _():
        idx_copy(0).start()

    @pl.when(i + 1 < nt)
    def _():
        idx_copy(i + 1).start()

    idx_copy(i).wait()
    slot = i % 2

    def body(t, carry):
        src = _row_slab(hnp_ref, t, spr)
        for k in range(n_pick):
            dst = idx_s[slot, k, t]
            pltpu.make_async_copy(src, _row_slab(xs_hbm, dst, spr), sem_r.at[0]).start(priority=k % 2)
        return carry

    lax.fori_loop(0, td, body, 0)

    for _ in range(n_pick):
        pltpu.make_async_copy(hnp_ref, _row_slab(xs_hbm, 0, spr, td), sem_r.at[0]).wait()

    @pl.when(i == nt - 1)
    def _():
        zero_scr[...] = jnp.zeros(zero_scr.shape, zero_scr.dtype)
        nz = zero_scr.shape[0]
        tail = pltpu.make_async_copy(zero_scr, xs_hbm.at[pl.ds(xs_hbm.shape[0] - nz, nz), :], sem_r.at[0])
        tail.start()
        tail.wait()


def _dispatch(dest, hnp, n_rows, tail_rows, spr, td=512):
    t = hnp.shape[0] // spr
    k = dest.shape[0]
    return pl.pallas_call(
        functools.partial(_dispatch_kernel, spr=spr),
        grid=(t // td,),
        in_specs=[pl.BlockSpec(memory_space=pl.ANY),
                  pl.BlockSpec((td * spr, LANES), lambda i: (i, 0))],
        out_specs=pl.BlockSpec(memory_space=pl.ANY),
        out_shape=jax.ShapeDtypeStruct((n_rows * spr, LANES), U32),
        scratch_shapes=[pltpu.SMEM((2, k, td), I32),
                        pltpu.VMEM((tail_rows * spr, LANES), U32),
                        pltpu.SemaphoreType.DMA((2,)),
                        pltpu.SemaphoreType.DMA((1,))],
        compiler_params=_cparams(("arbitrary",)),
        name="dispatch",
    )(dest, hnp)


_X_SLOTS = 4
_W_SLOTS = 3


def _block_offsets(sizes, starts, mb, n_blocks):
    nblk = (sizes + mb - 1) // mb
    blk_end = jnp.cumsum(nblk)
    bstart = blk_end - nblk
    b = jnp.arange(n_blocks, dtype=I32)
    owner = (blk_end[None, :] > b[:, None]) & (bstart[None, :] <= b[:, None])
    base = jnp.sum(jnp.where(owner, (starts - bstart * mb)[None, :], 0), axis=1)
    off = jnp.where(b < blk_end[-1], base + b * mb, 0)
    return off.astype(I32), nblk.astype(I32), bstart.astype(I32), blk_end[-1:].astype(I32)


def _moe_kernel(off_ref, nblk_ref, bstart_ref, nu_ref, xs_hbm, wgu_hbm, wd_hbm, ys_hbm,
                xbuf, ybuf, wgu_f, wd_f, wgu_b, wd_b, sem_x, sem_y, sem_w):
    e = pl.program_id(0)
    n_exp = pl.num_programs(0)
    n_used = nu_ref[0]
    half = wgu_b.shape[0] // 2
    spr = half // LANES
    mb = ybuf.shape[1] // spr
    de = wd_b.shape[0]
    ahead = _X_SLOTS - 2

    def x_copy(b):
        slot = b % _X_SLOTS
        return pltpu.make_async_copy(_row_slab(xs_hbm, off_ref[b], spr, mb), xbuf.at[slot], sem_x.at[slot])

    def y_copy(off, slot):
        return pltpu.make_async_copy(ybuf.at[slot], _row_slab(ys_hbm, off, spr, mb), sem_y.at[slot])

    def w_copies(j):
        slot = j % _W_SLOTS
        return (pltpu.make_async_copy(wgu_hbm.at[j], wgu_f.at[slot], sem_w.at[0, slot]),
                pltpu.make_async_copy(wd_hbm.at[j], wd_f.at[slot], sem_w.at[1, slot]))

    @pl.when(e == 0)
    def _():
        for j in range(_W_SLOTS - 1):
            for cp in w_copies(j):
                pl.when(j < n_exp)(functools.partial(cp.start, priority=1))
        for j in range(ahead + 1):
            pl.when(j < n_used)(x_copy(j).start)
        ybuf[1] = jnp.zeros(ybuf.shape[1:], ybuf.dtype)
        y_copy(ys_hbm.shape[0] // spr - mb, 1).start()

    @pl.when(e + _W_SLOTS - 1 < n_exp)
    def _():
        for cp in w_copies(e + _W_SLOTS - 1):
            cp.start(priority=1)

    nb = nblk_ref[e]
    b0 = bstart_ref[e]
    wslot = e % _W_SLOTS
    copy_gu, copy_d = w_copies(e)

    copy_gu.wait()

    @pl.when(nb > 0)
    def _():
        wgu_b[...] = wgu_f[wslot].astype(BF16)

    copy_d.wait()

    @pl.when(nb > 0)
    def _():
        wd_b[...] = wd_f[wslot].astype(BF16)

        def prefetch(b):
            @pl.when(b + ahead + 1 < n_used)
            def _():
                x_copy(b + ahead + 1).start()

        def block(b):
            x_copy(b).wait()
            lo, hi = _unpack_halves(_load_rows(xbuf.at[b % _X_SLOTS], spr))
            gu = (jnp.dot(lo.astype(BF16), wgu_b[0:half, :], preferred_element_type=F32)
                  + jnp.dot(hi.astype(BF16), wgu_b[half:, :], preferred_element_type=F32))
            act = _silu(gu[:, :de]) * gu[:, de:]
            y = jnp.dot(act.astype(BF16), wd_b[...], preferred_element_type=F32)
            yslot = b % 2
            _store_rows(ybuf.at[yslot], _pack_halves(y[:, :half], y[:, half:]))

            y_copy(0, 1 - yslot).wait()
            y_copy(off_ref[b], yslot).start()

        def body(i, carry):
            prefetch(b0 + i)
            block(b0 + i)
            return carry

        lax.fori_loop(0, nb, body, 0)

    @pl.when(e == n_exp - 1)
    def _():
        last = (n_used - 1) % 2
        y_copy(0, last).wait()
        tail = y_copy(ys_hbm.shape[0] // spr - mb, last)
        tail.start()
        tail.wait()


def _moe(xs, block_off, nblk, bstart, n_used, w_gate_up, w_down, mb):
    n_exp, d, de2 = w_gate_up.shape
    s = d // 2 // LANES
    de = w_down.shape[1]
    grid_spec = pltpu.PrefetchScalarGridSpec(
        num_scalar_prefetch=4,
        grid=(n_exp,),
        in_specs=[pl.BlockSpec(memory_space=pl.ANY),
                  pl.BlockSpec(memory_space=pl.ANY),
                  pl.BlockSpec(memory_space=pl.ANY)],
        out_specs=pl.BlockSpec(memory_space=pl.ANY),
        scratch_shapes=[pltpu.VMEM((_X_SLOTS, mb * s, LANES), U32),
                        pltpu.VMEM((2, mb * s, LANES), U32),
                        pltpu.VMEM((_W_SLOTS, d, de2), F32),
                        pltpu.VMEM((_W_SLOTS, de, d), F32),
                        pltpu.VMEM((d, de2), BF16),
                        pltpu.VMEM((de, d), BF16),
                        pltpu.SemaphoreType.DMA((_X_SLOTS,)),
                        pltpu.SemaphoreType.DMA((2,)),
                        pltpu.SemaphoreType.DMA((2, _W_SLOTS))])
    return pl.pallas_call(
        _moe_kernel,
        grid_spec=grid_spec,
        out_shape=jax.ShapeDtypeStruct(xs.shape, U32),
        compiler_params=_cparams(("arbitrary",)),
        name="moe",
    )(block_off, nblk, bstart, n_used, xs, w_gate_up, w_down)


def _combine_kernel(dest_hbm, ys_hbm, tw_ref, x1_ref, hnp_ref, gt_ref, gf_ref, wgu_ref, wd_ref,
                    o_ref, idx_s, gbuf, y_scr, w_scr, sem_i, sem_g):
    i = pl.program_id(0)
    nt = pl.num_programs(0)
    tc, d = x1_ref.shape
    half = d // 2
    spr = half // LANES
    de = wd_ref.shape[0]
    n_pick = tw_ref.shape[0]
    grp = SUBLANES

    def idx_copy(j):
        return pltpu.make_async_copy(dest_hbm.at[:, pl.ds(j * tc, tc)], idx_s.at[j % 3], sem_i.at[j % 3])

    def issue_rows(islot, gslot, t0):
        for j in range(grp):
            t = t0 + j
            for k in range(n_pick):
                src = idx_s[islot, k, t]
                pltpu.make_async_copy(_row_slab(ys_hbm, src, spr), _row_slab(gbuf.at[gslot, k], t, spr),
                                      sem_g.at[gslot]).start(priority=k % 2)

    def gather_wait(gslot):
        for k in range(n_pick):
            pltpu.make_async_copy(_row_slab(ys_hbm, 0, spr, tc), gbuf.at[gslot, k], sem_g.at[gslot]).wait()

    @pl.when(i == 0)
    def _():
        idx_copy(0).start()
        idx_copy(0).wait()

        def first(g, carry):
            issue_rows(0, 0, pl.multiple_of(g * grp, grp))
            return carry

        lax.fori_loop(0, tc // grp, first, 0)

        @pl.when(nt > 1)
        def _():
            idx_copy(1).start()

    @pl.when(i + 2 < nt)
    def _():
        idx_copy(i + 2).start()

    @pl.when(i + 1 < nt)
    def _():
        idx_copy(i + 1).wait()

    lo, hi = _unpack_halves(_load_rows(hnp_ref, spr))
    gu = (jnp.dot(lo.astype(BF16), wgu_ref[0:half, :], preferred_element_type=F32)
          + jnp.dot(hi.astype(BF16), wgu_ref[half:, :], preferred_element_type=F32))
    act = _silu(gu[:, :de]) * gu[:, de:]
    y_scr[...] = jnp.dot(act.astype(BF16), wd_ref[...], preferred_element_type=F32)

    tw = jnp.concatenate([tw_ref[...], jnp.zeros((LANES - n_pick, tc), F32)], axis=0)
    w_scr[...] = jnp.concatenate([tw[:, c0:c0 + LANES].T for c0 in range(0, tc, LANES)], axis=0)

    gslot = i % 2
    gather_wait(gslot)
    has_next = i + 1 < nt
    nxt_islot = jnp.where(has_next, (i + 1) % 3, i % 3)
    nxt_gslot = (i + 1) % 2
    gt = gt_ref[...]
    gf = gf_ref[...]

    def body(g, carry):
        t0 = pl.multiple_of(g * grp, grp)
        rows = pl.ds(t0, grp)
        acc_lo = y_scr[rows, 0:half]
        acc_hi = y_scr[rows, half:]
        wrow = w_scr[rows, :]
        for k in range(n_pick):
            src = gbuf.at[gslot, k]
            words = jnp.concatenate(
                [src[pl.ds(t0 * spr + s, grp, stride=spr), :] for s in range(spr)], axis=1)
            rlo, rhi = _unpack_halves(words)
            wk = wrow[:, k:k + 1]
            acc_lo = acc_lo + wk * rlo
            acc_hi = acc_hi + wk * rhi
        x2 = x1_ref[rows, :] + gt * jnp.concatenate([acc_lo, acc_hi], axis=1)
        ms = jnp.mean(x2 * x2, axis=-1, keepdims=True)
        o_ref[rows, :] = x2 * lax.rsqrt(ms + EPS) * gf
        issue_rows(nxt_islot, nxt_gslot, t0)
        return carry

    lax.fori_loop(0, tc // grp, body, 0)

    @pl.when(i == nt - 1)
    def _():
        gather_wait(nxt_gslot)


def _combine(dest, ys, top_w_t, x1, hnp, gt2, g_final, ws_gu, ws_d, seq, tc=128):
    t, d = x1.shape
    half = d // 2
    k = dest.shape[0]
    tpb = seq // tc
    de2 = ws_gu.shape[1]
    return pl.pallas_call(
        _combine_kernel,
        grid=(t // tc,),
        in_specs=[pl.BlockSpec(memory_space=pl.ANY),
                  pl.BlockSpec(memory_space=pl.ANY),
                  pl.BlockSpec((k, tc), lambda i: (0, i)),
                  pl.BlockSpec((tc, d), lambda i: (i, 0)),
                  pl.BlockSpec((tc * (half // LANES), LANES), lambda i: (i, 0)),
                  pl.BlockSpec((None, 1, d), lambda i: (i // tpb, 0, 0)),
                  pl.BlockSpec((1, d), lambda i: (0, 0)),
                  pl.BlockSpec((d, de2), lambda i: (0, 0)),
                  pl.BlockSpec((de2 // 2, d), lambda i: (0, 0))],
        out_specs=pl.BlockSpec((tc, d), lambda i: (i, 0)),
        out_shape=jax.ShapeDtypeStruct((t, d), F32),
        scratch_shapes=[pltpu.SMEM((3, k, tc), I32),
                        pltpu.VMEM((2, k, tc * (half // LANES), LANES), U32),
                        pltpu.VMEM((tc, d), F32),
                        pltpu.VMEM((tc, LANES), F32),
                        pltpu.SemaphoreType.DMA((3,)),
                        pltpu.SemaphoreType.DMA((2,))],
        compiler_params=_cparams(("arbitrary",)),
        name="combine",
    )(dest, ys, top_w_t, x1, hnp, gt2, g_final.reshape(1, d), ws_gu, ws_d)


def kernel(x, c, w_ada, b_ada, g_mix, w_in, conv_w, conv_b, conv_ln_g, conv_ln_b, qk_conv_w, qk_conv_b, b_igate, b_fgate, mlstm_norm_g, w_out, g_ffn, w_router, router_bias, w_gate_up, w_down, ws_gate_up, ws_down, g_final):
    bsz, seq, d = x.shape
    depth = w_ada.shape[0]
    assert depth == 1, "single-layer block"
    dc = conv_w.shape[2]
    dm = d - dc
    nh = b_igate.shape[1]
    dh = dm // nh
    n_exp = w_router.shape[2]
    t = bsz * seq
    n_main = 2 * dc + 4 * dm
    n_assign = TOP_K * t
    mb = MOE_BLOCK
    assert dc == dm and nh == N_HEADS and 2 * nh <= LANES and n_assign % mb == 0

    mod = _ada(c, w_ada[0], b_ada[0])
    sh1, sc1, gt1, sh2, sc2, gt2 = [m.reshape(bsz, 1, d) for m in jnp.split(mod, 6, axis=-1)]

    x2 = x.reshape(t, d)
    w_main = w_in[0, :, :n_main].astype(BF16)
    w_gate = jnp.zeros((d, LANES), BF16).at[:, :2 * nh].set(w_in[0, :, n_main:].astype(BF16))
    proj, gates = _inproj(x2, sc1, sh1, g_mix[0].reshape(1, d), w_main, w_gate, seq)

    u = _convmod(proj, conv_w[0], conv_b[0], conv_ln_g[0], conv_ln_b[0], bsz, seq)

    gbias = jnp.zeros((1, LANES), F32).at[0, :nh].set(b_igate[0]).at[0, nh:2 * nh].set(b_fgate[0])
    hm = _mlstm(proj, gates, gbias, qk_conv_w[0], qk_conv_b[0], mlstm_norm_g[0],
                bsz, seq, dm, dh, col0=2 * dc // dm)

    x1, hnp, top_idx_t, top_w_t, cnt = _outproj(
        u, hm, x2, gt1, sc2, sh2, g_ffn[0].reshape(1, d), w_out[0].astype(BF16),
        w_router[0], router_bias[0], seq)

    sizes = jnp.sum(cnt, axis=1).astype(I32)
    starts = jnp.cumsum(sizes) - sizes
    dest = _plan(top_idx_t, starts.astype(F32).reshape(n_exp, 1))
    xs = _dispatch(dest, hnp, n_assign + mb, mb, d // 2 // LANES)
    block_off, nblk, bstart, n_used = _block_offsets(sizes, starts, mb, n_assign // mb + n_exp)
    ys = _moe(xs, block_off, nblk, bstart, n_used, w_gate_up[0], w_down[0], mb)

    out = _combine(dest, ys, top_w_t, x1, hnp, gt2, g_final,
                   ws_gate_up[0].astype(BF16), ws_down[0].astype(BF16), seq)
    return out.reshape(bsz, seq, d)
```

```python
import functools

import jax
import jax.numpy as jnp
from jax import lax
from jax.experimental import pallas as pl
from jax.experimental.pallas import tpu as pltpu

F32 = jnp.float32
BF16 = jnp.bfloat16
U32 = jnp.uint32
I32 = jnp.int32

EPS = 1e-6
STAB_INIT = -1e30
TOP_K = 8
N_GROUPS = 8
TOPK_GROUPS = 4
ROUTED_SCALE = 2.5
N_HEADS = 4

LANES = 128
SUBLANES = 8
VMEM_LIMIT = 56 * 1024 * 1024

MLSTM_CHUNK = 256
MOE_BLOCK = 128
NEG_INF = float("-inf")
HIGHEST = lax.Precision.HIGHEST


def _cparams(sem):
    return pltpu.CompilerParams(dimension_semantics=sem, vmem_limit_bytes=VMEM_LIMIT)


def _sigmoid(x):
    return 1.0 / (1.0 + jnp.exp(-x))


def _silu(x):
    return x * _sigmoid(x)


def _pack_halves(lo, hi):
    lo_b = lax.bitcast_convert_type(lo.astype(BF16).astype(F32), U32)
    hi_b = lax.bitcast_convert_type(hi.astype(BF16).astype(F32), U32)
    return (hi_b & jnp.uint32(0xFFFF0000)) | (lo_b >> 16)


def _unpack_halves(w):
    lo = lax.bitcast_convert_type(w << 16, F32)
    hi = lax.bitcast_convert_type(w & jnp.uint32(0xFFFF0000), F32)
    return lo, hi


def _row_slab(ref, r, s_per_row, n=1):
    return ref.at[pl.ds(pl.multiple_of(r * s_per_row, s_per_row), n * s_per_row), :]


def _store_rows(ref, words):
    m = words.shape[0]
    s_per_row = words.shape[1] // LANES
    for s in range(s_per_row):
        ref[pl.ds(s, m, stride=s_per_row), :] = words[:, s * LANES:(s + 1) * LANES]


def _load_rows(ref, s_per_row):
    m = ref.shape[0] // s_per_row
    return jnp.concatenate([ref[pl.ds(s, m, stride=s_per_row), :] for s in range(s_per_row)], axis=1)


def _ada_kernel(c_ref, w_ref, b_ref, o_ref):
    cs = _silu(c_ref[...])
    cs_hi = cs.astype(BF16)
    cs_lo = (cs - cs_hi.astype(F32)).astype(BF16)
    w = w_ref[...].astype(BF16)
    o_ref[...] = (jnp.dot(cs_hi, w, preferred_element_type=F32)
                  + jnp.dot(cs_lo, w, preferred_element_type=F32)) + b_ref[...]


def _ada(c, w_ada, b_ada):
    bsz, d = c.shape
    n = w_ada.shape[1]
    tn = 1536
    rows = 2 * SUBLANES
    cp = jnp.zeros((rows, d), F32).at[:bsz].set(c)
    out = pl.pallas_call(
        _ada_kernel,
        grid=(n // tn,),
        in_specs=[pl.BlockSpec((rows, d), lambda j: (0, 0)),
                  pl.BlockSpec((d, tn), lambda j: (0, j)),
                  pl.BlockSpec((1, tn), lambda j: (0, j))],
        out_specs=pl.BlockSpec((rows, tn), lambda j: (0, j)),
        out_shape=jax.ShapeDtypeStruct((rows, n), F32),
        compiler_params=_cparams(("arbitrary",)),
        name="ada",
    )(cp, w_ada, b_ada.reshape(1, n))
    return out[:bsz]


_NORM_ROWS = 256


def _inproj_kernel(x_ref, sc_ref, sh_ref, g_ref, w_ref, wg_ref, o_ref, og_ref, h_scr):
    @pl.when(pl.program_id(1) == 0)
    def _():
        def rows(c, carry):
            r0 = pl.multiple_of(c * _NORM_ROWS, _NORM_ROWS)
            x = x_ref[pl.ds(r0, _NORM_ROWS), :]
            ms = jnp.mean(x * x, axis=-1, keepdims=True)
            h = (x * lax.rsqrt(ms + EPS) * g_ref[...]) * (1.0 + sc_ref[...]) + sh_ref[...]
            hb = h.astype(BF16)
            h_scr[pl.ds(r0, _NORM_ROWS), :] = hb
            og_ref[pl.ds(r0, _NORM_ROWS), :] = jnp.dot(hb, wg_ref[...], preferred_element_type=F32)
            return carry

        lax.fori_loop(0, x_ref.shape[0] // _NORM_ROWS, rows, 0)

    o_ref[...] = jnp.dot(h_scr[...], w_ref[...], preferred_element_type=F32).astype(o_ref.dtype)


def _inproj(x2, sc, sh, g, w_main, w_gate, seq, tm=1024, tn=1536):
    t, d = x2.shape
    n = w_main.shape[1]
    tpb = seq // tm
    return pl.pallas_call(
        _inproj_kernel,
        grid=(t // tm, n // tn),
        in_specs=[pl.BlockSpec((tm, d), lambda i, j: (i, 0)),
                  pl.BlockSpec((None, 1, d), lambda i, j: (i // tpb, 0, 0)),
                  pl.BlockSpec((None, 1, d), lambda i, j: (i // tpb, 0, 0)),
                  pl.BlockSpec((1, d), lambda i, j: (0, 0)),
                  pl.BlockSpec((d, tn), lambda i, j: (0, j)),
                  pl.BlockSpec((d, LANES), lambda i, j: (0, 0))],
        out_specs=[pl.BlockSpec((tm, tn), lambda i, j: (i, j)),
                   pl.BlockSpec((tm, LANES), lambda i, j: (i, 0))],
        out_shape=[jax.ShapeDtypeStruct((t, n), BF16),
                   jax.ShapeDtypeStruct((t, LANES), F32)],
        scratch_shapes=[pltpu.VMEM((tm, d), BF16)],
        compiler_params=_cparams(("arbitrary", "arbitrary")),
        name="inproj",
    )(x2, sc, sh, g, w_main, w_gate)


_CONV_RC = 128
_CONV_HIST = 32


def _convmod_kernel(a_ref, gate_ref, w_ref, b_ref, lg_ref, lb_ref, o_ref, ext_scr, y_scr, *, width):
    tt, ch = a_ref.shape
    hist = _CONV_HIST
    rc = _CONV_RC

    @pl.when(pl.program_id(1) == 0)
    def _():
        ext_scr[:, 0:hist, :] = jnp.zeros((ch // LANES, hist, LANES), F32)

    @pl.when(pl.program_id(1) != 0)
    def _():
        ext_scr[:, 0:hist, :] = ext_scr[:, tt:tt + hist, :]

    for c in range(ch // LANES):
        cols = slice(c * LANES, (c + 1) * LANES)
        ext_scr[c, hist:hist + tt, :] = a_ref[:, cols].astype(F32) * _sigmoid(gate_ref[:, cols].astype(F32))

    off0 = hist - (width - 1)
    n_rb = tt // rc
    n_acc = rc // SUBLANES

    def block(it, carry):
        c = it // n_rb
        r0 = pl.multiple_of((it % n_rb) * rc, rc)
        acc = [jnp.broadcast_to(b_ref[c], (SUBLANES, LANES))] * n_acc
        for j in range(width):
            wj = jnp.broadcast_to(w_ref[c, j:j + 1, :], (SUBLANES, LANES))
            for i in range(n_acc):
                acc[i] = acc[i] + wj * ext_scr[c, pl.ds(r0 + off0 + j + i * SUBLANES, SUBLANES), :]
        for i in range(n_acc):
            y_scr[c, pl.ds(r0 + i * SUBLANES, SUBLANES), :] = acc[i]
        return carry

    lax.fori_loop(0, (ch // LANES) * n_rb, block, 0)

    y = jnp.concatenate([y_scr[c] for c in range(ch // LANES)], axis=1)
    mu = jnp.mean(y, axis=-1, keepdims=True)
    yc = y - mu
    var = jnp.mean(yc * yc, axis=-1, keepdims=True)
    z = yc * lax.rsqrt(var + EPS) * lg_ref[...] + lb_ref[...]
    o_ref[...] = _silu(z).astype(o_ref.dtype)


def _convmod(proj, conv_w, conv_b, ln_g, ln_b, bsz, seq, tt=512):
    width, ch = conv_w.shape
    assert width - 1 <= _CONV_HIST and tt % _CONV_RC == 0
    tpb = seq // tt
    n_pl = ch // LANES
    vec = lambda: pl.BlockSpec((1, ch), lambda b, i: (0, 0))
    w_pl = conv_w.reshape(width, n_pl, LANES).transpose(1, 0, 2)
    b_pl = conv_b.reshape(n_pl, 1, LANES)
    return pl.pallas_call(
        functools.partial(_convmod_kernel, width=width),
        grid=(bsz, tpb),
        in_specs=[pl.BlockSpec((tt, ch), lambda b, i: (b * tpb + i, 0)),
                  pl.BlockSpec((tt, ch), lambda b, i: (b * tpb + i, 1)),
                  pl.BlockSpec((n_pl, width, LANES), lambda b, i: (0, 0, 0)),
                  pl.BlockSpec((n_pl, 1, LANES), lambda b, i: (0, 0, 0)),
                  vec(), vec()],
        out_specs=pl.BlockSpec((tt, ch), lambda b, i: (b * tpb + i, 0)),
        out_shape=jax.ShapeDtypeStruct((bsz * seq, ch), BF16),
        scratch_shapes=[pltpu.VMEM((n_pl, tt + _CONV_HIST, LANES), F32),
                        pltpu.VMEM((n_pl, tt, LANES), F32)],
        compiler_params=_cparams(("arbitrary", "arbitrary")),
        name="convmod",
    )(proj, proj, w_pl, b_pl, ln_g.reshape(1, ch), ln_b.reshape(1, ch))


def _log_sigmoid(x):
    return -(jnp.maximum(-x, 0.0) + jnp.log(1.0 + jnp.exp(-jnp.abs(x))))


def _mlstm_kernel(q_ref, k_ref, v_ref, og_ref, gates_ref, gbias_ref, cw_ref, cb_ref, ng_ref,
                  o_ref, ext_scr, c_scr, n_scr, m_scr, *, dh, qk_width):
    ln = q_ref.shape[0]
    dm = q_ref.shape[1]
    nh = dm // dh
    hist = SUBLANES

    n_pl = dm // LANES
    @pl.when(pl.program_id(1) == 0)
    def _():
        ext_scr[:, 0:hist, :] = jnp.zeros((2 * n_pl, hist, LANES), F32)
        c_scr[...] = jnp.zeros(c_scr.shape, F32)
        n_scr[...] = jnp.zeros(n_scr.shape, F32)
        m_scr[...] = jnp.full(m_scr.shape, STAB_INIT, F32)

    @pl.when(pl.program_id(1) != 0)
    def _():
        ext_scr[:, 0:hist, :] = ext_scr[:, ln:ln + hist, :]

    for c in range(n_pl):
        cols = slice(c * LANES, (c + 1) * LANES)
        ext_scr[c, hist:hist + ln, :] = q_ref[:, cols].astype(F32)
        ext_scr[n_pl + c, hist:hist + ln, :] = k_ref[:, cols].astype(F32)

    gts = gates_ref[...] + gbias_ref[...]
    lf = _log_sigmoid(gts)
    row = lax.broadcasted_iota(I32, (ln, ln), 0)
    col = lax.broadcasted_iota(I32, (ln, ln), 1)
    causal = row >= col
    tri = causal.astype(F32)
    bcum = jnp.dot(tri, lf, preferred_element_type=F32, precision=HIGHEST)
    bcum_t = bcum.T
    gts_t = gts.T

    def short_conv(c0):
        parts = []
        for c in range(c0 // LANES, (c0 + dh) // LANES):
            cols = slice(c * LANES, (c + 1) * LANES)
            acc = jnp.zeros((ln, LANES), F32) + cb_ref[:, cols]
            for j in range(qk_width):
                s = hist - (qk_width - 1) + j
                acc = acc + cw_ref[j:j + 1, cols] * ext_scr[c, s:s + ln, :]
            parts.append(acc)
        return _silu(jnp.concatenate(parts, axis=1))

    for h in range(nh):
        c0 = h * dh
        q = short_conv(c0)
        k = short_conv(dm + c0) * (dh ** -0.5)
        qb = q.astype(BF16)
        kb = k.astype(BF16)
        v = v_ref[:, c0:c0 + dh]
        vf = v.astype(F32)

        b_col = bcum[:, nh + h:nh + h + 1]
        b_row = bcum_t[nh + h:nh + h + 1, :]
        ig_col = gts[:, h:h + 1]
        ig_row = gts_t[h:h + 1, :]
        m_prev = m_scr[h, 0:1, 0:1]

        dmat = jnp.where(causal, b_col - b_row + ig_row, NEG_INF)
        m_inter = b_col + m_prev
        m_t = jnp.maximum(m_inter, jnp.max(dmat, axis=1, keepdims=True))
        pmat = jnp.exp(dmat - m_t)
        qk = lax.dot_general(qb, kb, (((1,), (1,)), ((), ())), preferred_element_type=F32)
        s = qk * pmat
        a = jnp.exp(m_inter - m_t)
        inter = jnp.dot(qb, c_scr[h].astype(BF16), preferred_element_type=F32)
        num = jnp.dot(s.astype(BF16), v, preferred_element_type=F32) + a * inter
        den = jnp.sum(s, axis=1, keepdims=True) + a * jnp.sum(q * n_scr[h], axis=1, keepdims=True)
        hv = num / jnp.maximum(jnp.abs(den), jnp.exp(-m_t))

        mu = jnp.mean(hv, axis=-1, keepdims=True)
        hc = hv - mu
        var = jnp.mean(hc * hc, axis=-1, keepdims=True)
        hn = hc * lax.rsqrt(var + EPS) * ng_ref[:, c0:c0 + dh]
        og = og_ref[:, c0:c0 + dh].astype(F32)
        o_ref[:, c0:c0 + dh] = (_sigmoid(og) * hn).astype(o_ref.dtype)

        b_last = b_col[ln - 1:ln, :]
        g_col = b_last - b_col + ig_col
        m_new = jnp.maximum(b_last + m_prev, jnp.max(g_col, axis=0, keepdims=True))
        wk = jnp.exp(g_col - m_new)
        decay = jnp.exp(b_last + m_prev - m_new)
        upd = lax.dot_general(kb, (vf * wk).astype(BF16), (((0,), (0,)), ((), ())),
                              preferred_element_type=F32)
        c_scr[h] = decay * c_scr[h] + upd
        n_scr[h] = decay * n_scr[h] + jnp.sum(k * wk, axis=0, keepdims=True)
        m_scr[h] = jnp.broadcast_to(m_new, m_scr.shape[1:])


def _mlstm(proj, gates, gbias, qk_conv_w, qk_conv_b, norm_g, bsz, seq, dm, dh, col0):
    ln = MLSTM_CHUNK
    nc = seq // ln
    nh = dm // dh
    width = qk_conv_w.shape[0]
    blk = lambda off: pl.BlockSpec((ln, dm), lambda b, c: (b * nc + c, col0 + off))
    return pl.pallas_call(
        functools.partial(_mlstm_kernel, dh=dh, qk_width=width),
        grid=(bsz, nc),
        in_specs=[blk(0), blk(1), blk(2), blk(3),
                  pl.BlockSpec((ln, LANES), lambda b, c: (b * nc + c, 0)),
                  pl.BlockSpec((1, LANES), lambda b, c: (0, 0)),
                  pl.BlockSpec((width, 2 * dm), lambda b, c: (0, 0)),
                  pl.BlockSpec((1, 2 * dm), lambda b, c: (0, 0)),
                  pl.BlockSpec((1, dm), lambda b, c: (0, 0))],
        out_specs=pl.BlockSpec((ln, dm), lambda b, c: (b * nc + c, 0)),
        out_shape=jax.ShapeDtypeStruct((bsz * seq, dm), BF16),
        scratch_shapes=[pltpu.VMEM((2 * dm // LANES, SUBLANES + ln, LANES), F32),
                        pltpu.VMEM((nh, dh, dh), F32),
                        pltpu.VMEM((nh, 1, dh), F32),
                        pltpu.VMEM((nh, SUBLANES, LANES), F32)],
        compiler_params=_cparams(("arbitrary", "arbitrary")),
        name="mlstm",
    )(proj, proj, proj, proj, gates, gbias, qk_conv_w, qk_conv_b.reshape(1, 2 * dm),
      norm_g.reshape(1, dm))


def _route_chunk(logits_t, bias_col, n_exp):
    gsz = n_exp // N_GROUPS
    scores = _sigmoid(logits_t)
    choice = scores + bias_col
    rows_g = lax.broadcasted_iota(I32, (gsz, LANES), 0).astype(F32)
    gscore = []
    for g in range(N_GROUPS):
        v = choice[g * gsz:(g + 1) * gsz, :]
        m1 = jnp.max(v, axis=0, keepdims=True)
        i1 = jnp.min(jnp.where(v == m1, rows_g, float(gsz)), axis=0, keepdims=True)
        m2 = jnp.max(jnp.where(rows_g == i1, NEG_INF, v), axis=0, keepdims=True)
        gscore.append(m1 + m2)
    gs = jnp.concatenate(gscore, axis=0)
    giota = lax.broadcasted_iota(I32, (N_GROUPS, LANES), 0).astype(F32)
    keep = jnp.zeros((N_GROUPS, LANES), F32)
    for _ in range(TOPK_GROUPS):
        m = jnp.max(gs, axis=0, keepdims=True)
        gi = jnp.min(jnp.where(gs == m, giota, float(N_GROUPS)), axis=0, keepdims=True)
        sel = giota == gi
        keep = jnp.where(sel, 1.0, keep)
        gs = jnp.where(sel, NEG_INF, gs)
    mc = jnp.concatenate(
        [jnp.where(keep[g:g + 1, :] > 0.5, choice[g * gsz:(g + 1) * gsz, :], NEG_INF)
         for g in range(N_GROUPS)], axis=0)
    rows = lax.broadcasted_iota(I32, (n_exp, LANES), 0).astype(F32)
    idxs, ws = [], []
    picked = jnp.zeros((n_exp, LANES), F32)
    for _ in range(TOP_K):
        m = jnp.max(mc, axis=0, keepdims=True)
        idx = jnp.min(jnp.where(mc == m, rows, float(n_exp)), axis=0, keepdims=True)
        sel = rows == idx
        ws.append(jnp.sum(jnp.where(sel, scores, 0.0), axis=0, keepdims=True))
        idxs.append(idx)
        picked = jnp.where(sel, 1.0, picked)
        mc = jnp.where(sel, NEG_INF, mc)
    w = jnp.concatenate(ws, axis=0)
    w = w / jnp.sum(w, axis=0, keepdims=True) * ROUTED_SCALE
    return jnp.concatenate(idxs, axis=0).astype(I32), w, picked


def _outproj_kernel(u_ref, hm_ref, x_ref, gt_ref, sc_ref, sh_ref, g_ref, w_ref, wrh_ref, wrl_ref,
                    rb_ref, x1_ref, hnp_ref, idx_ref, tw_ref, cnt_ref):
    tm, d = x_ref.shape
    dc = u_ref.shape[1]
    half = d // 2
    n_exp = wrh_ref.shape[1]

    @pl.when(pl.program_id(0) == 0)
    def _():
        cnt_ref[...] = jnp.zeros(cnt_ref.shape, F32)

    mix = (jnp.dot(u_ref[...], w_ref[0:dc, :], preferred_element_type=F32)
           + jnp.dot(hm_ref[...], w_ref[dc:, :], preferred_element_type=F32))
    x1 = x_ref[...] + gt_ref[...] * mix
    x1_ref[...] = x1
    ms = jnp.mean(x1 * x1, axis=-1, keepdims=True)
    hn = (x1 * lax.rsqrt(ms + EPS) * g_ref[...]) * (1.0 + sc_ref[...]) + sh_ref[...]
    _store_rows(hnp_ref, _pack_halves(hn[:, :half], hn[:, half:]))
    hn_hi = hn.astype(BF16)
    hn_lo = (hn - hn_hi.astype(F32)).astype(BF16)
    dot = lambda a, b: jnp.dot(a, b, preferred_element_type=F32)
    logits = (dot(hn_hi, wrh_ref[...]) + dot(hn_lo, wrh_ref[...])) + dot(hn_hi, wrl_ref[...])
    cnt = cnt_ref[...]
    for c0 in range(0, tm, LANES):
        idx, w, picked = _route_chunk(logits[c0:c0 + LANES, :].T, rb_ref[...], n_exp)
        idx_ref[:, c0:c0 + LANES] = idx
        tw_ref[:, c0:c0 + LANES] = w
        cnt = cnt + picked
    cnt_ref[...] = cnt


def _outproj(u, hm, x2, gt1, sc2, sh2, g_ffn, w_out, w_router, router_bias, seq, tm=512):
    t, d = x2.shape
    dc = u.shape[1]
    n_exp = w_router.shape[1]
    tpb = seq // tm
    nt = t // tm
    wr_hi = w_router.astype(BF16)
    wr_lo = (w_router - wr_hi.astype(F32)).astype(BF16)
    per_b = lambda: pl.BlockSpec((None, 1, d), lambda i: (i // tpb, 0, 0))
    const = lambda shape: pl.BlockSpec(shape, lambda i: (0, 0), pipeline_mode=pl.Buffered(1))
    return pl.pallas_call(
        _outproj_kernel,
        grid=(nt,),
        in_specs=[pl.BlockSpec((tm, dc), lambda i: (i, 0)),
                  pl.BlockSpec((tm, d - dc), lambda i: (i, 0)),
                  pl.BlockSpec((tm, d), lambda i: (i, 0)),
                  per_b(), per_b(), per_b(),
                  pl.BlockSpec((1, d), lambda i: (0, 0)),
                  const((d, d)), const((d, n_exp)), const((d, n_exp)),
                  pl.BlockSpec((n_exp, 1), lambda i: (0, 0))],
        out_specs=[pl.BlockSpec((tm, d), lambda i: (i, 0)),
                   pl.BlockSpec((tm * (d // 2 // LANES), LANES), lambda i: (i, 0)),
                   pl.BlockSpec((TOP_K, tm), lambda i: (0, i)),
                   pl.BlockSpec((TOP_K, tm), lambda i: (0, i)),
                   pl.BlockSpec((n_exp, LANES), lambda i: (0, 0))],
        out_shape=[jax.ShapeDtypeStruct((t, d), F32),
                   jax.ShapeDtypeStruct((t * (d // 2 // LANES), LANES), U32),
                   jax.ShapeDtypeStruct((TOP_K, t), I32),
                   jax.ShapeDtypeStruct((TOP_K, t), F32),
                   jax.ShapeDtypeStruct((n_exp, LANES), F32)],
        compiler_params=_cparams(("arbitrary",)),
        name="outproj_route",
    )(u, hm, x2, gt1, sc2, sh2, g_ffn, w_out, wr_hi, wr_lo, router_bias.reshape(n_exp, 1))


def _plan_kernel(idx_ref, start_ref, dest_ref, run_scr):
    n_exp = start_ref.shape[0]
    tp = idx_ref.shape[1]

    @pl.when(pl.program_id(0) == 0)
    def _():
        run_scr[...] = start_ref[...]

    rows = lax.broadcasted_iota(I32, (n_exp, tp), 0)
    earlier = (lax.broadcasted_iota(I32, (tp, tp), 0)
               < lax.broadcasted_iota(I32, (tp, tp), 1)).astype(BF16)
    base = run_scr[...]
    for k in range(TOP_K):
        oh = rows == idx_ref[k:k + 1, :]
        ohf = oh.astype(F32)
        pref = jnp.dot(ohf.astype(BF16), earlier, preferred_element_type=F32)
        dest = jnp.sum(jnp.where(oh, pref + base, 0.0), axis=0, keepdims=True)
        dest_ref[k:k + 1, :] = dest.astype(I32)
        base = base + jnp.sum(ohf, axis=1, keepdims=True)
    run_scr[...] = base


def _plan(top_idx_t, starts_col, tp=512):
    k, t = top_idx_t.shape
    n_exp = starts_col.shape[0]
    return pl.pallas_call(
        _plan_kernel,
        grid=(t // tp,),
        in_specs=[pl.BlockSpec((k, tp), lambda i: (0, i)),
                  pl.BlockSpec((n_exp, 1), lambda i: (0, 0))],
        out_specs=pl.BlockSpec((k, tp), lambda i: (0, i)),
        out_shape=jax.ShapeDtypeStruct((k, t), I32),
        scratch_shapes=[pltpu.VMEM((n_exp, 1), F32)],
        compiler_params=_cparams(("arbitrary",)),
        name="plan",
    )(top_idx_t, starts_col)


def _dispatch_kernel(dest_hbm, hnp_ref, xs_hbm, idx_s, zero_scr, sem_i, sem_r, *, spr):
    i = pl.program_id(0)
    nt = pl.num_programs(0)
    td = hnp_ref.shape[0] // spr
    n_pick = idx_s.shape[1]

    def idx_copy(j):
        return pltpu.make_async_copy(dest_hbm.at[:, pl.ds(j * td, td)], idx_s.at[j % 2], sem_i.at[j % 2])

    @pl.when(i == 0)
    def _():
        idx_copy(0).start()

    @pl.when(i + 1 < nt)
    def _():
        idx_copy(i + 1).start()

    idx_copy(i).wait()
    slot = i % 2

    def body(t, carry):
        src = _row_slab(hnp_ref, t, spr)
        for k in range(n_pick):
            dst = idx_s[slot, k, t]
            pltpu.make_async_copy(src, _row_slab(xs_hbm, dst, spr), sem_r.at[0]).start(priority=k % 2)
        return carry

    lax.fori_loop(0, td, body, 0)

    for _ in range(n_pick):
        pltpu.make_async_copy(hnp_ref, _row_slab(xs_hbm, 0, spr, td), sem_r.at[0]).wait()

    @pl.when(i == nt - 1)
    def _():
        zero_scr[...] = jnp.zeros(zero_scr.shape, zero_scr.dtype)
        nz = zero_scr.shape[0]
        tail = pltpu.make_async_copy(zero_scr, xs_hbm.at[pl.ds(xs_hbm.shape[0] - nz, nz), :], sem_r.at[0])
        tail.start()
        tail.wait()


def _dispatch(dest, hnp, n_rows, tail_rows, spr, td=1024):
    t = hnp.shape[0] // spr
    k = dest.shape[0]
    return pl.pallas_call(
        functools.partial(_dispatch_kernel, spr=spr),
        grid=(t // td,),
        in_specs=[pl.BlockSpec(memory_space=pl.ANY),
                  pl.BlockSpec((td * spr, LANES), lambda i: (i, 0))],
        out_specs=pl.BlockSpec(memory_space=pl.ANY),
        out_shape=jax.ShapeDtypeStruct((n_rows * spr, LANES), U32),
        scratch_shapes=[pltpu.SMEM((2, k, td), I32),
                        pltpu.VMEM((tail_rows * spr, LANES), U32),
                        pltpu.SemaphoreType.DMA((2,)),
                        pltpu.SemaphoreType.DMA((1,))],
        compiler_params=_cparams(("arbitrary",)),
        name="dispatch",
    )(dest, hnp)


_X_SLOTS = 4
_W_SLOTS = 3


def _block_offsets(sizes, starts, mb, n_blocks):
    nblk = (sizes + mb - 1) // mb
    blk_end = jnp.cumsum(nblk)
    bstart = blk_end - nblk
    b = jnp.arange(n_blocks, dtype=I32)
    owner = (blk_end[None, :] > b[:, None]) & (bstart[None, :] <= b[:, None])
    base = jnp.sum(jnp.where(owner, (starts - bstart * mb)[None, :], 0), axis=1)
    off = jnp.where(b < blk_end[-1], base + b * mb, 0)
    return off.astype(I32), nblk.astype(I32), bstart.astype(I32), blk_end[-1:].astype(I32)


def _moe_kernel(off_ref, nblk_ref, bstart_ref, nu_ref, xs_hbm, wgu_hbm, wd_hbm, ys_hbm,
                xbuf, ybuf, wgu_f, wd_f, wgu_b, wd_b, sem_x, sem_y, sem_w):
    e = pl.program_id(0)
    n_exp = pl.num_programs(0)
    n_used = nu_ref[0]
    half = wgu_b.shape[0] // 2
    spr = half // LANES
    mb = ybuf.shape[1] // spr
    de = wd_b.shape[0]
    ahead = _X_SLOTS - 2

    def x_copy(b):
        slot = b % _X_SLOTS
        return pltpu.make_async_copy(_row_slab(xs_hbm, off_ref[b], spr, mb), xbuf.at[slot], sem_x.at[slot])

    def y_copy(off, slot):
        return pltpu.make_async_copy(ybuf.at[slot], _row_slab(ys_hbm, off, spr, mb), sem_y.at[slot])

    def w_copies(j):
        slot = j % _W_SLOTS
        return (pltpu.make_async_copy(wgu_hbm.at[j], wgu_f.at[slot], sem_w.at[0, slot]),
                pltpu.make_async_copy(wd_hbm.at[j], wd_f.at[slot], sem_w.at[1, slot]))

    @pl.when(e == 0)
    def _():
        for j in range(_W_SLOTS - 1):
            for cp in w_copies(j):
                pl.when(j < n_exp)(functools.partial(cp.start, priority=1))
        for j in range(ahead + 1):
            pl.when(j < n_used)(x_copy(j).start)
        ybuf[1] = jnp.zeros(ybuf.shape[1:], ybuf.dtype)
        y_copy(ys_hbm.shape[0] // spr - mb, 1).start()

    @pl.when(e + _W_SLOTS - 1 < n_exp)
    def _():
        for cp in w_copies(e + _W_SLOTS - 1):
            cp.start(priority=1)

    nb = nblk_ref[e]
    b0 = bstart_ref[e]
    wslot = e % _W_SLOTS
    copy_gu, copy_d = w_copies(e)

    copy_gu.wait()

    @pl.when(nb > 0)
    def _():
        wgu_b[...] = wgu_f[wslot].astype(BF16)

    copy_d.wait()

    @pl.when(nb > 0)
    def _():
        wd_b[...] = wd_f[wslot].astype(BF16)

        def prefetch(b):
            @pl.when(b + ahead + 1 < n_used)
            def _():
                x_copy(b + ahead + 1).start()

        def block(b):
            x_copy(b).wait()
            lo, hi = _unpack_halves(_load_rows(xbuf.at[b % _X_SLOTS], spr))
            gu = (jnp.dot(lo.astype(BF16), wgu_b[0:half, :], preferred_element_type=F32)
                  + jnp.dot(hi.astype(BF16), wgu_b[half:, :], preferred_element_type=F32))
            act = _silu(gu[:, :de]) * gu[:, de:]
            y = jnp.dot(act.astype(BF16), wd_b[...], preferred_element_type=F32)
            yslot = b % 2
            _store_rows(ybuf.at[yslot], _pack_halves(y[:, :half], y[:, half:]))

            y_copy(0, 1 - yslot).wait()
            y_copy(off_ref[b], yslot).start()

        def body(i, carry):
            prefetch(b0 + i)
            block(b0 + i)
            return carry

        lax.fori_loop(0, nb, body, 0)

    @pl.when(e == n_exp - 1)
    def _():
        last = (n_used - 1) % 2
        y_copy(0, last).wait()
        tail = y_copy(ys_hbm.shape[0] // spr - mb, last)
        tail.start()
        tail.wait()


def _moe(xs, block_off, nblk, bstart, n_used, w_gate_up, w_down, mb):
    n_exp, d, de2 = w_gate_up.shape
    s = d // 2 // LANES
    de = w_down.shape[1]
    grid_spec = pltpu.PrefetchScalarGridSpec(
        num_scalar_prefetch=4,
        grid=(n_exp,),
        in_specs=[pl.BlockSpec(memory_space=pl.ANY),
                  pl.BlockSpec(memory_space=pl.ANY),
                  pl.BlockSpec(memory_space=pl.ANY)],
        out_specs=pl.BlockSpec(memory_space=pl.ANY),
        scratch_shapes=[pltpu.VMEM((_X_SLOTS, mb * s, LANES), U32),
                        pltpu.VMEM((2, mb * s, LANES), U32),
                        pltpu.VMEM((_W_SLOTS, d, de2), F32),
                        pltpu.VMEM((_W_SLOTS, de, d), F32),
                        pltpu.VMEM((d, de2), BF16),
                        pltpu.VMEM((de, d), BF16),
                        pltpu.SemaphoreType.DMA((_X_SLOTS,)),
                        pltpu.SemaphoreType.DMA((2,)),
                        pltpu.SemaphoreType.DMA((2, _W_SLOTS))])
    return pl.pallas_call(
        _moe_kernel,
        grid_spec=grid_spec,
        out_shape=jax.ShapeDtypeStruct(xs.shape, U32),
        compiler_params=_cparams(("arbitrary",)),
        name="moe",
    )(block_off, nblk, bstart, n_used, xs, w_gate_up, w_down)


def _combine_kernel(dest_hbm, ys_hbm, tw_ref, x1_ref, hnp_ref, gt_ref, gf_ref, wgu_ref, wd_ref,
                    o_ref, idx_s, gbuf, y_scr, w_scr, sem_i, sem_g):
    i = pl.program_id(0)
    nt = pl.num_programs(0)
    tc, d = x1_ref.shape
    half = d // 2
    spr = half // LANES
    de = wd_ref.shape[0]
    n_pick = tw_ref.shape[0]
    grp = SUBLANES

    def idx_copy(j):
        return pltpu.make_async_copy(dest_hbm.at[:, pl.ds(j * tc, tc)], idx_s.at[j % 3], sem_i.at[j % 3])

    def issue_rows(islot, gslot, t0):
        for j in range(grp):
            t = t0 + j
            for k in range(n_pick):
                src = idx_s[islot, k, t]
                pltpu.make_async_copy(_row_slab(ys_hbm, src, spr), _row_slab(gbuf.at[gslot, k], t, spr),
                                      sem_g.at[gslot]).start(priority=k % 2)

    def gather_wait(gslot):
        for k in range(n_pick):
            pltpu.make_async_copy(_row_slab(ys_hbm, 0, spr, tc), gbuf.at[gslot, k], sem_g.at[gslot]).wait()

    @pl.when(i == 0)
    def _():
        idx_copy(0).start()
        idx_copy(0).wait()

        def first(g, carry):
            issue_rows(0, 0, pl.multiple_of(g * grp, grp))
            return carry

        lax.fori_loop(0, tc // grp, first, 0)

        @pl.when(nt > 1)
        def _():
            idx_copy(1).start()

    @pl.when(i + 2 < nt)
    def _():
        idx_copy(i + 2).start()

    @pl.when(i + 1 < nt)
    def _():
        idx_copy(i + 1).wait()

    lo, hi = _unpack_halves(_load_rows(hnp_ref, spr))
    gu = (jnp.dot(lo.astype(BF16), wgu_ref[0:half, :], preferred_element_type=F32)
          + jnp.dot(hi.astype(BF16), wgu_ref[half:, :], preferred_element_type=F32))
    act = _silu(gu[:, :de]) * gu[:, de:]
    y_scr[...] = jnp.dot(act.astype(BF16), wd_ref[...], preferred_element_type=F32)

    tw = jnp.concatenate([tw_ref[...], jnp.zeros((LANES - n_pick, tc), F32)], axis=0)
    w_scr[...] = jnp.concatenate([tw[:, c0:c0 + LANES].T for c0 in range(0, tc, LANES)], axis=0)

    gslot = i % 2
    gather_wait(gslot)
    has_next = i + 1 < nt
    nxt_islot = jnp.where(has_next, (i + 1) % 3, i % 3)
    nxt_gslot = (i + 1) % 2
    gt = gt_ref[...]
    gf = gf_ref[...]

    def body(g, carry):
        t0 = pl.multiple_of(g * grp, grp)
        rows = pl.ds(t0, grp)
        acc_lo = y_scr[rows, 0:half]
        acc_hi = y_scr[rows, half:]
        wrow = w_scr[rows, :]
        for k in range(n_pick):
            src = gbuf.at[gslot, k]
            words = jnp.concatenate(
                [src[pl.ds(t0 * spr + s, grp, stride=spr), :] for s in range(spr)], axis=1)
            rlo, rhi = _unpack_halves(words)
            wk = wrow[:, k:k + 1]
            acc_lo = acc_lo + wk * rlo
            acc_hi = acc_hi + wk * rhi
        x2 = x1_ref[rows, :] + gt * jnp.concatenate([acc_lo, acc_hi], axis=1)
        ms = jnp.mean(x2 * x2, axis=-1, keepdims=True)
        o_ref[rows, :] = x2 * lax.rsqrt(ms + EPS) * gf
        issue_rows(nxt_islot, nxt_gslot, t0)
        return carry

    lax.fori_loop(0, tc // grp, body, 0)

    @pl.when(i == nt - 1)
    def _():
        gather_wait(nxt_gslot)


def _combine(dest, ys, top_w_t, x1, hnp, gt2, g_final, ws_gu, ws_d, seq, tc=256):
    t, d = x1.shape
    half = d // 2
    k = dest.shape[0]
    tpb = seq // tc
    de2 = ws_gu.shape[1]
    return pl.pallas_call(
        _combine_kernel,
        grid=(t // tc,),
        in_specs=[pl.BlockSpec(memory_space=pl.ANY),
                  pl.BlockSpec(memory_space=pl.ANY),
                  pl.BlockSpec((k, tc), lambda i: (0, i)),
                  pl.BlockSpec((tc, d), lambda i: (i, 0)),
                  pl.BlockSpec((tc * (half // LANES), LANES), lambda i: (i, 0)),
                  pl.BlockSpec((None, 1, d), lambda i: (i // tpb, 0, 0)),
                  pl.BlockSpec((1, d), lambda i: (0, 0)),
                  pl.BlockSpec((d, de2), lambda i: (0, 0)),
                  pl.BlockSpec((de2 // 2, d), lambda i: (0, 0))],
        out_specs=pl.BlockSpec((tc, d), lambda i: (i, 0)),
        out_shape=jax.ShapeDtypeStruct((t, d), F32),
        scratch_shapes=[pltpu.SMEM((3, k, tc), I32),
                        pltpu.VMEM((2, k, tc * (half // LANES), LANES), U32),
                        pltpu.VMEM((tc, d), F32),
                        pltpu.VMEM((tc, LANES), F32),
                        pltpu.SemaphoreType.DMA((3,)),
                        pltpu.SemaphoreType.DMA((2,))],
        compiler_params=_cparams(("arbitrary",)),
        name="combine",
    )(dest, ys, top_w_t, x1, hnp, gt2, g_final.reshape(1, d), ws_gu, ws_d)


def kernel(x, c, w_ada, b_ada, g_mix, w_in, conv_w, conv_b, conv_ln_g, conv_ln_b, qk_conv_w, qk_conv_b, b_igate, b_fgate, mlstm_norm_g, w_out, g_ffn, w_router, router_bias, w_gate_up, w_down, ws_gate_up, ws_down, g_final):
    bsz, seq, d = x.shape
    depth = w_ada.shape[0]
    assert depth == 1, "single-layer block"
    dc = conv_w.shape[2]
    dm = d - dc
    nh = b_igate.shape[1]
    dh = dm // nh
    n_exp = w_router.shape[2]
    t = bsz * seq
    n_main = 2 * dc + 4 * dm
    n_assign = TOP_K * t
    mb = MOE_BLOCK
    assert dc == dm and nh == N_HEADS and 2 * nh <= LANES and n_assign % mb == 0

    mod = _ada(c, w_ada[0], b_ada[0])
    sh1, sc1, gt1, sh2, sc2, gt2 = [m.reshape(bsz, 1, d) for m in jnp.split(mod, 6, axis=-1)]

    x2 = x.reshape(t, d)
    w_main = w_in[0, :, :n_main].astype(BF16)
    w_gate = jnp.zeros((d, LANES), BF16).at[:, :2 * nh].set(w_in[0, :, n_main:].astype(BF16))
    proj, gates = _inproj(x2, sc1, sh1, g_mix[0].reshape(1, d), w_main, w_gate, seq)

    u = _convmod(proj, conv_w[0], conv_b[0], conv_ln_g[0], conv_ln_b[0], bsz, seq)

    gbias = jnp.zeros((1, LANES), F32).at[0, :nh].set(b_igate[0]).at[0, nh:2 * nh].set(b_fgate[0])
    hm = _mlstm(proj, gates, gbias, qk_conv_w[0], qk_conv_b[0], mlstm_norm_g[0],
                bsz, seq, dm, dh, col0=2 * dc // dm)

    x1, hnp, top_idx_t, top_w_t, cnt = _outproj(
        u, hm, x2, gt1, sc2, sh2, g_ffn[0].reshape(1, d), w_out[0].astype(BF16),
        w_router[0], router_bias[0], seq)

    sizes = jnp.sum(cnt, axis=1).astype(I32)
    starts = jnp.cumsum(sizes) - sizes
    dest = _plan(top_idx_t, starts.astype(F32).reshape(n_exp, 1))
    xs = _dispatch(dest, hnp, n_assign + mb, mb, d // 2 // LANES)
    block_off, nblk, bstart, n_used = _block_offsets(sizes, starts, mb, n_assign // mb + n_exp)
    ys = _moe(xs, block_off, nblk, bstart, n_used, w_gate_up[0], w_down[0], mb)

    out = _combine(dest, ys, top_w_t, x1, hnp, gt2, g_final,
                   ws_gate_up[0].astype(BF16), ws_down[0].astype(BF16), seq)
    return out.reshape(bsz, seq, d)
```

```python
import functools

import jax
import jax.numpy as jnp
from jax import lax
from jax.experimental import pallas as pl
from jax.experimental.pallas import tpu as pltpu

F32 = jnp.float32
BF16 = jnp.bfloat16
U32 = jnp.uint32
I32 = jnp.int32

EPS = 1e-6
STAB_INIT = -1e30
TOP_K = 8
N_GROUPS = 8
TOPK_GROUPS = 4
ROUTED_SCALE = 2.5
N_HEADS = 4

LANES = 128
SUBLANES = 8
VMEM_LIMIT = 56 * 1024 * 1024

MLSTM_CHUNK = 256
MOE_BLOCK = 256
NEG_INF = float("-inf")
HIGHEST = lax.Precision.HIGHEST


def _cparams(sem):
    return pltpu.CompilerParams(dimension_semantics=sem, vmem_limit_bytes=VMEM_LIMIT)


def _sigmoid(x):
    return 1.0 / (1.0 + jnp.exp(-x))


def _silu(x):
    return x * _sigmoid(x)


def _pack_halves(lo, hi):
    lo_b = lax.bitcast_convert_type(lo.astype(BF16).astype(F32), U32)
    hi_b = lax.bitcast_convert_type(hi.astype(BF16).astype(F32), U32)
    return (hi_b & jnp.uint32(0xFFFF0000)) | (lo_b >> 16)


def _unpack_halves(w):
    lo = lax.bitcast_convert_type(w << 16, F32)
    hi = lax.bitcast_convert_type(w & jnp.uint32(0xFFFF0000), F32)
    return lo, hi


def _row_slab(ref, r, s_per_row, n=1):
    return ref.at[pl.ds(pl.multiple_of(r * s_per_row, s_per_row), n * s_per_row), :]


def _store_rows(ref, words):
    m = words.shape[0]
    s_per_row = words.shape[1] // LANES
    for s in range(s_per_row):
        ref[pl.ds(s, m, stride=s_per_row), :] = words[:, s * LANES:(s + 1) * LANES]


def _load_rows(ref, s_per_row):
    m = ref.shape[0] // s_per_row
    return jnp.concatenate([ref[pl.ds(s, m, stride=s_per_row), :] for s in range(s_per_row)], axis=1)


def _ada_kernel(c_ref, w_ref, b_ref, o_ref):
    cs = _silu(c_ref[...])
    cs_hi = cs.astype(BF16)
    cs_lo = (cs - cs_hi.astype(F32)).astype(BF16)
    w = w_ref[...].astype(BF16)
    o_ref[...] = (jnp.dot(cs_hi, w, preferred_element_type=F32)
                  + jnp.dot(cs_lo, w, preferred_element_type=F32)) + b_ref[...]


def _ada(c, w_ada, b_ada):
    bsz, d = c.shape
    n = w_ada.shape[1]
    tn = 1536
    rows = 2 * SUBLANES
    cp = jnp.zeros((rows, d), F32).at[:bsz].set(c)
    out = pl.pallas_call(
        _ada_kernel,
        grid=(n // tn,),
        in_specs=[pl.BlockSpec((rows, d), lambda j: (0, 0)),
                  pl.BlockSpec((d, tn), lambda j: (0, j)),
                  pl.BlockSpec((1, tn), lambda j: (0, j))],
        out_specs=pl.BlockSpec((rows, tn), lambda j: (0, j)),
        out_shape=jax.ShapeDtypeStruct((rows, n), F32),
        compiler_params=_cparams(("arbitrary",)),
        name="ada",
    )(cp, w_ada, b_ada.reshape(1, n))
    return out[:bsz]


_NORM_ROWS = 256


def _inproj_kernel(x_ref, sc_ref, sh_ref, g_ref, w_ref, wg_ref, o_ref, og_ref, h_scr):
    @pl.when(pl.program_id(1) == 0)
    def _():
        def rows(c, carry):
            r0 = pl.multiple_of(c * _NORM_ROWS, _NORM_ROWS)
            x = x_ref[pl.ds(r0, _NORM_ROWS), :]
            ms = jnp.mean(x * x, axis=-1, keepdims=True)
            h = (x * lax.rsqrt(ms + EPS) * g_ref[...]) * (1.0 + sc_ref[...]) + sh_ref[...]
            hb = h.astype(BF16)
            h_scr[pl.ds(r0, _NORM_ROWS), :] = hb
            og_ref[pl.ds(r0, _NORM_ROWS), :] = jnp.dot(hb, wg_ref[...], preferred_element_type=F32)
            return carry

        lax.fori_loop(0, x_ref.shape[0] // _NORM_ROWS, rows, 0)

    o_ref[...] = jnp.dot(h_scr[...], w_ref[...], preferred_element_type=F32).astype(o_ref.dtype)


def _inproj(x2, sc, sh, g, w_main, w_gate, seq, tm=1024, tn=1536):
    t, d = x2.shape
    n = w_main.shape[1]
    tpb = seq // tm
    return pl.pallas_call(
        _inproj_kernel,
        grid=(t // tm, n // tn),
        in_specs=[pl.BlockSpec((tm, d), lambda i, j: (i, 0)),
                  pl.BlockSpec((None, 1, d), lambda i, j: (i // tpb, 0, 0)),
                  pl.BlockSpec((None, 1, d), lambda i, j: (i // tpb, 0, 0)),
                  pl.BlockSpec((1, d), lambda i, j: (0, 0)),
                  pl.BlockSpec((d, tn), lambda i, j: (0, j)),
                  pl.BlockSpec((d, LANES), lambda i, j: (0, 0))],
        out_specs=[pl.BlockSpec((tm, tn), lambda i, j: (i, j)),
                   pl.BlockSpec((tm, LANES), lambda i, j: (i, 0))],
        out_shape=[jax.ShapeDtypeStruct((t, n), BF16),
                   jax.ShapeDtypeStruct((t, LANES), F32)],
        scratch_shapes=[pltpu.VMEM((tm, d), BF16)],
        compiler_params=_cparams(("arbitrary", "arbitrary")),
        name="inproj",
    )(x2, sc, sh, g, w_main, w_gate)


_CONV_RC = 128
_CONV_HIST = 32


def _convmod_kernel(a_ref, gate_ref, w_ref, b_ref, lg_ref, lb_ref, o_ref, ext_scr, y_scr, *, width):
    tt, ch = a_ref.shape
    hist = _CONV_HIST
    rc = _CONV_RC

    @pl.when(pl.program_id(1) == 0)
    def _():
        ext_scr[:, 0:hist, :] = jnp.zeros((ch // LANES, hist, LANES), F32)

    @pl.when(pl.program_id(1) != 0)
    def _():
        ext_scr[:, 0:hist, :] = ext_scr[:, tt:tt + hist, :]

    for c in range(ch // LANES):
        cols = slice(c * LANES, (c + 1) * LANES)
        ext_scr[c, hist:hist + tt, :] = a_ref[:, cols].astype(F32) * _sigmoid(gate_ref[:, cols].astype(F32))

    off0 = hist - (width - 1)
    n_rb = tt // rc
    n_acc = rc // SUBLANES

    def block(it, carry):
        c = it // n_rb
        r0 = pl.multiple_of((it % n_rb) * rc, rc)
        acc = [jnp.broadcast_to(b_ref[c], (SUBLANES, LANES))] * n_acc
        for j in range(width):
            wj = jnp.broadcast_to(w_ref[c, j:j + 1, :], (SUBLANES, LANES))
            for i in range(n_acc):
                acc[i] = acc[i] + wj * ext_scr[c, pl.ds(r0 + off0 + j + i * SUBLANES, SUBLANES), :]
        for i in range(n_acc):
            y_scr[c, pl.ds(r0 + i * SUBLANES, SUBLANES), :] = acc[i]
        return carry

    lax.fori_loop(0, (ch // LANES) * n_rb, block, 0)

    y = jnp.concatenate([y_scr[c] for c in range(ch // LANES)], axis=1)
    mu = jnp.mean(y, axis=-1, keepdims=True)
    yc = y - mu
    var = jnp.mean(yc * yc, axis=-1, keepdims=True)
    z = yc * lax.rsqrt(var + EPS) * lg_ref[...] + lb_ref[...]
    o_ref[...] = _silu(z).astype(o_ref.dtype)


def _convmod(proj, conv_w, conv_b, ln_g, ln_b, bsz, seq, tt=512):
    width, ch = conv_w.shape
    assert width - 1 <= _CONV_HIST and tt % _CONV_RC == 0
    tpb = seq // tt
    n_pl = ch // LANES
    vec = lambda: pl.BlockSpec((1, ch), lambda b, i: (0, 0))
    w_pl = conv_w.reshape(width, n_pl, LANES).transpose(1, 0, 2)
    b_pl = conv_b.reshape(n_pl, 1, LANES)
    return pl.pallas_call(
        functools.partial(_convmod_kernel, width=width),
        grid=(bsz, tpb),
        in_specs=[pl.BlockSpec((tt, ch), lambda b, i: (b * tpb + i, 0)),
                  pl.BlockSpec((tt, ch), lambda b, i: (b * tpb + i, 1)),
                  pl.BlockSpec((n_pl, width, LANES), lambda b, i: (0, 0, 0)),
                  pl.BlockSpec((n_pl, 1, LANES), lambda b, i: (0, 0, 0)),
                  vec(), vec()],
        out_specs=pl.BlockSpec((tt, ch), lambda b, i: (b * tpb + i, 0)),
        out_shape=jax.ShapeDtypeStruct((bsz * seq, ch), BF16),
        scratch_shapes=[pltpu.VMEM((n_pl, tt + _CONV_HIST, LANES), F32),
                        pltpu.VMEM((n_pl, tt, LANES), F32)],
        compiler_params=_cparams(("arbitrary", "arbitrary")),
        name="convmod",
    )(proj, proj, w_pl, b_pl, ln_g.reshape(1, ch), ln_b.reshape(1, ch))


def _log_sigmoid(x):
    return -(jnp.maximum(-x, 0.0) + jnp.log(1.0 + jnp.exp(-jnp.abs(x))))


def _mlstm_kernel(q_ref, k_ref, v_ref, og_ref, gates_ref, gbias_ref, cw_ref, cb_ref, ng_ref,
                  o_ref, ext_scr, c_scr, n_scr, m_scr, *, dh, qk_width):
    ln = q_ref.shape[0]
    dm = q_ref.shape[1]
    nh = dm // dh
    hist = SUBLANES

    n_pl = dm // LANES
    @pl.when(pl.program_id(1) == 0)
    def _():
        ext_scr[:, 0:hist, :] = jnp.zeros((2 * n_pl, hist, LANES), F32)
        c_scr[...] = jnp.zeros(c_scr.shape, F32)
        n_scr[...] = jnp.zeros(n_scr.shape, F32)
        m_scr[...] = jnp.full(m_scr.shape, STAB_INIT, F32)

    @pl.when(pl.program_id(1) != 0)
    def _():
        ext_scr[:, 0:hist, :] = ext_scr[:, ln:ln + hist, :]

    for c in range(n_pl):
        cols = slice(c * LANES, (c + 1) * LANES)
        ext_scr[c, hist:hist + ln, :] = q_ref[:, cols].astype(F32)
        ext_scr[n_pl + c, hist:hist + ln, :] = k_ref[:, cols].astype(F32)

    gts = gates_ref[...] + gbias_ref[...]
    lf = _log_sigmoid(gts)
    row = lax.broadcasted_iota(I32, (ln, ln), 0)
    col = lax.broadcasted_iota(I32, (ln, ln), 1)
    causal = row >= col
    tri = causal.astype(F32)
    bcum = jnp.dot(tri, lf, preferred_element_type=F32, precision=HIGHEST)
    bcum_t = bcum.T
    gts_t = gts.T

    def short_conv(c0):
        parts = []
        for c in range(c0 // LANES, (c0 + dh) // LANES):
            cols = slice(c * LANES, (c + 1) * LANES)
            acc = jnp.zeros((ln, LANES), F32) + cb_ref[:, cols]
            for j in range(qk_width):
                s = hist - (qk_width - 1) + j
                acc = acc + cw_ref[j:j + 1, cols] * ext_scr[c, s:s + ln, :]
            parts.append(acc)
        return _silu(jnp.concatenate(parts, axis=1))

    for h in range(nh):
        c0 = h * dh
        q = short_conv(c0)
        k = short_conv(dm + c0) * (dh ** -0.5)
        qb = q.astype(BF16)
        kb = k.astype(BF16)
        v = v_ref[:, c0:c0 + dh]
        vf = v.astype(F32)

        b_col = bcum[:, nh + h:nh + h + 1]
        b_row = bcum_t[nh + h:nh + h + 1, :]
        ig_col = gts[:, h:h + 1]
        ig_row = gts_t[h:h + 1, :]
        m_prev = m_scr[h, 0:1, 0:1]

        dmat = jnp.where(causal, b_col - b_row + ig_row, NEG_INF)
        m_inter = b_col + m_prev
        m_t = jnp.maximum(m_inter, jnp.max(dmat, axis=1, keepdims=True))
        pmat = jnp.exp(dmat - m_t)
        qk = lax.dot_general(qb, kb, (((1,), (1,)), ((), ())), preferred_element_type=F32)
        s = qk * pmat
        a = jnp.exp(m_inter - m_t)
        inter = jnp.dot(qb, c_scr[h].astype(BF16), preferred_element_type=F32)
        num = jnp.dot(s.astype(BF16), v, preferred_element_type=F32) + a * inter
        den = jnp.sum(s, axis=1, keepdims=True) + a * jnp.sum(q * n_scr[h], axis=1, keepdims=True)
        hv = num / jnp.maximum(jnp.abs(den), jnp.exp(-m_t))

        mu = jnp.mean(hv, axis=-1, keepdims=True)
        hc = hv - mu
        var = jnp.mean(hc * hc, axis=-1, keepdims=True)
        hn = hc * lax.rsqrt(var + EPS) * ng_ref[:, c0:c0 + dh]
        og = og_ref[:, c0:c0 + dh].astype(F32)
        o_ref[:, c0:c0 + dh] = (_sigmoid(og) * hn).astype(o_ref.dtype)

        b_last = b_col[ln - 1:ln, :]
        g_col = b_last - b_col + ig_col
        m_new = jnp.maximum(b_last + m_prev, jnp.max(g_col, axis=0, keepdims=True))
        wk = jnp.exp(g_col - m_new)
        decay = jnp.exp(b_last + m_prev - m_new)
        upd = lax.dot_general(kb, (vf * wk).astype(BF16), (((0,), (0,)), ((), ())),
                              preferred_element_type=F32)
        c_scr[h] = decay * c_scr[h] + upd
        n_scr[h] = decay * n_scr[h] + jnp.sum(k * wk, axis=0, keepdims=True)
        m_scr[h] = jnp.broadcast_to(m_new, m_scr.shape[1:])


def _mlstm(proj, gates, gbias, qk_conv_w, qk_conv_b, norm_g, bsz, seq, dm, dh, col0):
    ln = MLSTM_CHUNK
    nc = seq // ln
    nh = dm // dh
    width = qk_conv_w.shape[0]
    blk = lambda off: pl.BlockSpec((ln, dm), lambda b, c: (b * nc + c, col0 + off))
    return pl.pallas_call(
        functools.partial(_mlstm_kernel, dh=dh, qk_width=width),
        grid=(bsz, nc),
        in_specs=[blk(0), blk(1), blk(2), blk(3),
                  pl.BlockSpec((ln, LANES), lambda b, c: (b * nc + c, 0)),
                  pl.BlockSpec((1, LANES), lambda b, c: (0, 0)),
                  pl.BlockSpec((width, 2 * dm), lambda b, c: (0, 0)),
                  pl.BlockSpec((1, 2 * dm), lambda b, c: (0, 0)),
                  pl.BlockSpec((1, dm), lambda b, c: (0, 0))],
        out_specs=pl.BlockSpec((ln, dm), lambda b, c: (b * nc + c, 0)),
        out_shape=jax.ShapeDtypeStruct((bsz * seq, dm), BF16),
        scratch_shapes=[pltpu.VMEM((2 * dm // LANES, SUBLANES + ln, LANES), F32),
                        pltpu.VMEM((nh, dh, dh), F32),
                        pltpu.VMEM((nh, 1, dh), F32),
                        pltpu.VMEM((nh, SUBLANES, LANES), F32)],
        compiler_params=_cparams(("arbitrary", "arbitrary")),
        name="mlstm",
    )(proj, proj, proj, proj, gates, gbias, qk_conv_w, qk_conv_b.reshape(1, 2 * dm),
      norm_g.reshape(1, dm))


def _route_chunk(logits_t, bias_col, n_exp):
    gsz = n_exp // N_GROUPS
    scores = _sigmoid(logits_t)
    choice = scores + bias_col
    rows_g = lax.broadcasted_iota(I32, (gsz, LANES), 0).astype(F32)
    gscore = []
    for g in range(N_GROUPS):
        v = choice[g * gsz:(g + 1) * gsz, :]
        m1 = jnp.max(v, axis=0, keepdims=True)
        i1 = jnp.min(jnp.where(v == m1, rows_g, float(gsz)), axis=0, keepdims=True)
        m2 = jnp.max(jnp.where(rows_g == i1, NEG_INF, v), axis=0, keepdims=True)
        gscore.append(m1 + m2)
    gs = jnp.concatenate(gscore, axis=0)
    giota = lax.broadcasted_iota(I32, (N_GROUPS, LANES), 0).astype(F32)
    keep = jnp.zeros((N_GROUPS, LANES), F32)
    for _ in range(TOPK_GROUPS):
        m = jnp.max(gs, axis=0, keepdims=True)
        gi = jnp.min(jnp.where(gs == m, giota, float(N_GROUPS)), axis=0, keepdims=True)
        sel = giota == gi
        keep = jnp.where(sel, 1.0, keep)
        gs = jnp.where(sel, NEG_INF, gs)
    mc = jnp.concatenate(
        [jnp.where(keep[g:g + 1, :] > 0.5, choice[g * gsz:(g + 1) * gsz, :], NEG_INF)
         for g in range(N_GROUPS)], axis=0)
    rows = lax.broadcasted_iota(I32, (n_exp, LANES), 0).astype(F32)
    idxs, ws = [], []
    picked = jnp.zeros((n_exp, LANES), F32)
    for _ in range(TOP_K):
        m = jnp.max(mc, axis=0, keepdims=True)
        idx = jnp.min(jnp.where(mc == m, rows, float(n_exp)), axis=0, keepdims=True)
        sel = rows == idx
        ws.append(jnp.sum(jnp.where(sel, scores, 0.0), axis=0, keepdims=True))
        idxs.append(idx)
        picked = jnp.where(sel, 1.0, picked)
        mc = jnp.where(sel, NEG_INF, mc)
    w = jnp.concatenate(ws, axis=0)
    w = w / jnp.sum(w, axis=0, keepdims=True) * ROUTED_SCALE
    return jnp.concatenate(idxs, axis=0).astype(I32), w, picked


def _outproj_kernel(u_ref, hm_ref, x_ref, gt_ref, sc_ref, sh_ref, g_ref, w_ref, wrh_ref, wrl_ref,
                    rb_ref, x1_ref, hnp_ref, idx_ref, tw_ref, cnt_ref):
    tm, d = x_ref.shape
    dc = u_ref.shape[1]
    half = d // 2
    n_exp = wrh_ref.shape[1]

    @pl.when(pl.program_id(0) == 0)
    def _():
        cnt_ref[...] = jnp.zeros(cnt_ref.shape, F32)

    mix = (jnp.dot(u_ref[...], w_ref[0:dc, :], preferred_element_type=F32)
           + jnp.dot(hm_ref[...], w_ref[dc:, :], preferred_element_type=F32))
    x1 = x_ref[...] + gt_ref[...] * mix
    x1_ref[...] = x1
    ms = jnp.mean(x1 * x1, axis=-1, keepdims=True)
    hn = (x1 * lax.rsqrt(ms + EPS) * g_ref[...]) * (1.0 + sc_ref[...]) + sh_ref[...]
    _store_rows(hnp_ref, _pack_halves(hn[:, :half], hn[:, half:]))
    hn_hi = hn.astype(BF16)
    hn_lo = (hn - hn_hi.astype(F32)).astype(BF16)
    dot = lambda a, b: jnp.dot(a, b, preferred_element_type=F32)
    logits = (dot(hn_hi, wrh_ref[...]) + dot(hn_lo, wrh_ref[...])) + dot(hn_hi, wrl_ref[...])
    cnt = cnt_ref[...]
    for c0 in range(0, tm, LANES):
        idx, w, picked = _route_chunk(logits[c0:c0 + LANES, :].T, rb_ref[...], n_exp)
        idx_ref[:, c0:c0 + LANES] = idx
        tw_ref[:, c0:c0 + LANES] = w
        cnt = cnt + picked
    cnt_ref[...] = cnt


def _outproj(u, hm, x2, gt1, sc2, sh2, g_ffn, w_out, w_router, router_bias, seq, tm=512):
    t, d = x2.shape
    dc = u.shape[1]
    n_exp = w_router.shape[1]
    tpb = seq // tm
    nt = t // tm
    wr_hi = w_router.astype(BF16)
    wr_lo = (w_router - wr_hi.astype(F32)).astype(BF16)
    per_b = lambda: pl.BlockSpec((None, 1, d), lambda i: (i // tpb, 0, 0))
    const = lambda shape: pl.BlockSpec(shape, lambda i: (0, 0), pipeline_mode=pl.Buffered(1))
    return pl.pallas_call(
        _outproj_kernel,
        grid=(nt,),
        in_specs=[pl.BlockSpec((tm, dc), lambda i: (i, 0)),
                  pl.BlockSpec((tm, d - dc), lambda i: (i, 0)),
                  pl.BlockSpec((tm, d), lambda i: (i, 0)),
                  per_b(), per_b(), per_b(),
                  pl.BlockSpec((1, d), lambda i: (0, 0)),
                  const((d, d)), const((d, n_exp)), const((d, n_exp)),
                  pl.BlockSpec((n_exp, 1), lambda i: (0, 0))],
        out_specs=[pl.BlockSpec((tm, d), lambda i: (i, 0)),
                   pl.BlockSpec((tm * (d // 2 // LANES), LANES), lambda i: (i, 0)),
                   pl.BlockSpec((TOP_K, tm), lambda i: (0, i)),
                   pl.BlockSpec((TOP_K, tm), lambda i: (0, i)),
                   pl.BlockSpec((n_exp, LANES), lambda i: (0, 0))],
        out_shape=[jax.ShapeDtypeStruct((t, d), F32),
                   jax.ShapeDtypeStruct((t * (d // 2 // LANES), LANES), U32),
                   jax.ShapeDtypeStruct((TOP_K, t), I32),
                   jax.ShapeDtypeStruct((TOP_K, t), F32),
                   jax.ShapeDtypeStruct((n_exp, LANES), F32)],
        compiler_params=_cparams(("arbitrary",)),
        name="outproj_route",
    )(u, hm, x2, gt1, sc2, sh2, g_ffn, w_out, wr_hi, wr_lo, router_bias.reshape(n_exp, 1))


def _plan_kernel(idx_ref, start_ref, dest_ref, run_scr):
    n_exp = start_ref.shape[0]
    tp = idx_ref.shape[1]

    @pl.when(pl.program_id(0) == 0)
    def _():
        run_scr[...] = start_ref[...]

    rows = lax.broadcasted_iota(I32, (n_exp, tp), 0)
    earlier = (lax.broadcasted_iota(I32, (tp, tp), 0)
               < lax.broadcasted_iota(I32, (tp, tp), 1)).astype(BF16)
    base = run_scr[...]
    for k in range(TOP_K):
        oh = rows == idx_ref[k:k + 1, :]
        ohf = oh.astype(F32)
        pref = jnp.dot(ohf.astype(BF16), earlier, preferred_element_type=F32)
        dest = jnp.sum(jnp.where(oh, pref + base, 0.0), axis=0, keepdims=True)
        dest_ref[k:k + 1, :] = dest.astype(I32)
        base = base + jnp.sum(ohf, axis=1, keepdims=True)
    run_scr[...] = base


def _plan(top_idx_t, starts_col, tp=512):
    k, t = top_idx_t.shape
    n_exp = starts_col.shape[0]
    return pl.pallas_call(
        _plan_kernel,
        grid=(t // tp,),
        in_specs=[pl.BlockSpec((k, tp), lambda i: (0, i)),
                  pl.BlockSpec((n_exp, 1), lambda i: (0, 0))],
        out_specs=pl.BlockSpec((k, tp), lambda i: (0, i)),
        out_shape=jax.ShapeDtypeStruct((k, t), I32),
        scratch_shapes=[pltpu.VMEM((n_exp, 1), F32)],
        compiler_params=_cparams(("arbitrary",)),
        name="plan",
    )(top_idx_t, starts_col)


def _dispatch_kernel(dest_hbm, hnp_ref, xs_hbm, idx_s, zero_scr, sem_i, sem_r, *, spr):
    i = pl.program_id(0)
    nt = pl.num_programs(0)
    td = hnp_ref.shape[0] // spr
    n_pick = idx_s.shape[1]

    def idx_copy(j):
        return pltpu.make_async_copy(dest_hbm.at[:, pl.ds(j * td, td)], idx_s.at[j % 2], sem_i.at[j % 2])

    @pl.when(i == 0)
    def _():
        idx_copy(0).start()

    @pl.when(i + 1 < nt)
    def _():
        idx_copy(i + 1).start()

    idx_copy(i).wait()
    slot = i % 2

    def body(t, carry):
        src = _row_slab(hnp_ref, t, spr)
        for k in range(n_pick):
            dst = idx_s[slot, k, t]
            pltpu.make_async_copy(src, _row_slab(xs_hbm, dst, spr), sem_r.at[0]).start(priority=k % 2)
        return carry

    lax.fori_loop(0, td, body, 0)

    for _ in range(n_pick):
        pltpu.make_async_copy(hnp_ref, _row_slab(xs_hbm, 0, spr, td), sem_r.at[0]).wait()

    @pl.when(i == nt - 1)
    def _():
        zero_scr[...] = jnp.zeros(zero_scr.shape, zero_scr.dtype)
        nz = zero_scr.shape[0]
        tail = pltpu.make_async_copy(zero_scr, xs_hbm.at[pl.ds(xs_hbm.shape[0] - nz, nz), :], sem_r.at[0])
        tail.start()
        tail.wait()


def _dispatch(dest, hnp, n_rows, tail_rows, spr, td=1024):
    t = hnp.shape[0] // spr
    k = dest.shape[0]
    return pl.pallas_call(
        functools.partial(_dispatch_kernel, spr=spr),
        grid=(t // td,),
        in_specs=[pl.BlockSpec(memory_space=pl.ANY),
                  pl.BlockSpec((td * spr, LANES), lambda i: (i, 0))],
        out_specs=pl.BlockSpec(memory_space=pl.ANY),
        out_shape=jax.ShapeDtypeStruct((n_rows * spr, LANES), U32),
        scratch_shapes=[pltpu.SMEM((2, k, td), I32),
                        pltpu.VMEM((tail_rows * spr, LANES), U32),
                        pltpu.SemaphoreType.DMA((2,)),
                        pltpu.SemaphoreType.DMA((1,))],
        compiler_params=_cparams(("arbitrary",)),
        name="dispatch",
    )(dest, hnp)


_X_SLOTS = 4
_W_SLOTS = 3


def _block_offsets(sizes, starts, mb, n_blocks):
    nblk = (sizes + mb - 1) // mb
    blk_end = jnp.cumsum(nblk)
    bstart = blk_end - nblk
    b = jnp.arange(n_blocks, dtype=I32)
    owner = (blk_end[None, :] > b[:, None]) & (bstart[None, :] <= b[:, None])
    base = jnp.sum(jnp.where(owner, (starts - bstart * mb)[None, :], 0), axis=1)
    off = jnp.where(b < blk_end[-1], base + b * mb, 0)
    return off.astype(I32), nblk.astype(I32), bstart.astype(I32), blk_end[-1:].astype(I32)


def _moe_kernel(off_ref, nblk_ref, bstart_ref, nu_ref, xs_hbm, wgu_hbm, wd_hbm, ys_hbm,
                xbuf, ybuf, wgu_f, wd_f, wgu_b, wd_b, sem_x, sem_y, sem_w):
    e = pl.program_id(0)
    n_exp = pl.num_programs(0)
    n_used = nu_ref[0]
    half = wgu_b.shape[0] // 2
    spr = half // LANES
    mb = ybuf.shape[1] // spr
    de = wd_b.shape[0]
    ahead = _X_SLOTS - 2

    def x_copy(b):
        slot = b % _X_SLOTS
        return pltpu.make_async_copy(_row_slab(xs_hbm, off_ref[b], spr, mb), xbuf.at[slot], sem_x.at[slot])

    def y_copy(off, slot):
        return pltpu.make_async_copy(ybuf.at[slot], _row_slab(ys_hbm, off, spr, mb), sem_y.at[slot])

    def w_copies(j):
        slot = j % _W_SLOTS
        return (pltpu.make_async_copy(wgu_hbm.at[j], wgu_f.at[slot], sem_w.at[0, slot]),
                pltpu.make_async_copy(wd_hbm.at[j], wd_f.at[slot], sem_w.at[1, slot]))

    @pl.when(e == 0)
    def _():
        for j in range(_W_SLOTS - 1):
            for cp in w_copies(j):
                pl.when(j < n_exp)(functools.partial(cp.start, priority=1))
        for j in range(ahead + 1):
            pl.when(j < n_used)(x_copy(j).start)
        ybuf[1] = jnp.zeros(ybuf.shape[1:], ybuf.dtype)
        y_copy(ys_hbm.shape[0] // spr - mb, 1).start()

    @pl.when(e + _W_SLOTS - 1 < n_exp)
    def _():
        for cp in w_copies(e + _W_SLOTS - 1):
            cp.start(priority=1)

    nb = nblk_ref[e]
    b0 = bstart_ref[e]
    wslot = e % _W_SLOTS
    copy_gu, copy_d = w_copies(e)

    copy_gu.wait()

    @pl.when(nb > 0)
    def _():
        wgu_b[...] = wgu_f[wslot].astype(BF16)

    copy_d.wait()

    @pl.when(nb > 0)
    def _():
        wd_b[...] = wd_f[wslot].astype(BF16)

        def prefetch(b):
            @pl.when(b + ahead + 1 < n_used)
            def _():
                x_copy(b + ahead + 1).start()

        def block(b):
            x_copy(b).wait()
            lo, hi = _unpack_halves(_load_rows(xbuf.at[b % _X_SLOTS], spr))
            gu = (jnp.dot(lo.astype(BF16), wgu_b[0:half, :], preferred_element_type=F32)
                  + jnp.dot(hi.astype(BF16), wgu_b[half:, :], preferred_element_type=F32))
            act = _silu(gu[:, :de]) * gu[:, de:]
            y = jnp.dot(act.astype(BF16), wd_b[...], preferred_element_type=F32)
            yslot = b % 2
            _store_rows(ybuf.at[yslot], _pack_halves(y[:, :half], y[:, half:]))

            y_copy(0, 1 - yslot).wait()
            y_copy(off_ref[b], yslot).start()

        def body(i, carry):
            prefetch(b0 + i)
            block(b0 + i)
            return carry

        lax.fori_loop(0, nb, body, 0)

    @pl.when(e == n_exp - 1)
    def _():
        last = (n_used - 1) % 2
        y_copy(0, last).wait()
        tail = y_copy(ys_hbm.shape[0] // spr - mb, last)
        tail.start()
        tail.wait()


def _moe(xs, block_off, nblk, bstart, n_used, w_gate_up, w_down, mb):
    n_exp, d, de2 = w_gate_up.shape
    s = d // 2 // LANES
    de = w_down.shape[1]
    grid_spec = pltpu.PrefetchScalarGridSpec(
        num_scalar_prefetch=4,
        grid=(n_exp,),
        in_specs=[pl.BlockSpec(memory_space=pl.ANY),
                  pl.BlockSpec(memory_space=pl.ANY),
                  pl.BlockSpec(memory_space=pl.ANY)],
        out_specs=pl.BlockSpec(memory_space=pl.ANY),
        scratch_shapes=[pltpu.VMEM((_X_SLOTS, mb * s, LANES), U32),
                        pltpu.VMEM((2, mb * s, LANES), U32),
                        pltpu.VMEM((_W_SLOTS, d, de2), F32),
                        pltpu.VMEM((_W_SLOTS, de, d), F32),
                        pltpu.VMEM((d, de2), BF16),
                        pltpu.VMEM((de, d), BF16),
                        pltpu.SemaphoreType.DMA((_X_SLOTS,)),
                        pltpu.SemaphoreType.DMA((2,)),
                        pltpu.SemaphoreType.DMA((2, _W_SLOTS))])
    return pl.pallas_call(
        _moe_kernel,
        grid_spec=grid_spec,
        out_shape=jax.ShapeDtypeStruct(xs.shape, U32),
        compiler_params=_cparams(("arbitrary",)),
        name="moe",
    )(block_off, nblk, bstart, n_used, xs, w_gate_up, w_down)


def _combine_kernel(dest_hbm, ys_hbm, tw_ref, x1_ref, hnp_ref, gt_ref, gf_ref, wgu_ref, wd_ref,
                    o_ref, idx_s, gbuf, y_scr, w_scr, sem_i, sem_g):
    i = pl.program_id(0)
    nt = pl.num_programs(0)
    tc, d = x1_ref.shape
    half = d // 2
    spr = half // LANES
    de = wd_ref.shape[0]
    n_pick = tw_ref.shape[0]
    grp = SUBLANES

    def idx_copy(j):
        return pltpu.make_async_copy(dest_hbm.at[:, pl.ds(j * tc, tc)], idx_s.at[j % 3], sem_i.at[j % 3])

    def issue_rows(islot, gslot, t0):
        for j in range(grp):
            t = t0 + j
            for k in range(n_pick):
                src = idx_s[islot, k, t]
                pltpu.make_async_copy(_row_slab(ys_hbm, src, spr), _row_slab(gbuf.at[gslot, k], t, spr),
                                      sem_g.at[gslot]).start(priority=k % 2)

    def gather_wait(gslot):
        for k in range(n_pick):
            pltpu.make_async_copy(_row_slab(ys_hbm, 0, spr, tc), gbuf.at[gslot, k], sem_g.at[gslot]).wait()

    @pl.when(i == 0)
    def _():
        idx_copy(0).start()
        idx_copy(0).wait()

        def first(g, carry):
            issue_rows(0, 0, pl.multiple_of(g * grp, grp))
            return carry

        lax.fori_loop(0, tc // grp, first, 0)

        @pl.when(nt > 1)
        def _():
            idx_copy(1).start()

    @pl.when(i + 2 < nt)
    def _():
        idx_copy(i + 2).start()

    @pl.when(i + 1 < nt)
    def _():
        idx_copy(i + 1).wait()

    lo, hi = _unpack_halves(_load_rows(hnp_ref, spr))
    gu = (jnp.dot(lo.astype(BF16), wgu_ref[0:half, :], preferred_element_type=F32)
          + jnp.dot(hi.astype(BF16), wgu_ref[half:, :], preferred_element_type=F32))
    act = _silu(gu[:, :de]) * gu[:, de:]
    y_scr[...] = jnp.dot(act.astype(BF16), wd_ref[...], preferred_element_type=F32)

    tw = jnp.concatenate([tw_ref[...], jnp.zeros((LANES - n_pick, tc), F32)], axis=0)
    w_scr[...] = jnp.concatenate([tw[:, c0:c0 + LANES].T for c0 in range(0, tc, LANES)], axis=0)

    gslot = i % 2
    gather_wait(gslot)
    has_next = i + 1 < nt
    nxt_islot = jnp.where(has_next, (i + 1) % 3, i % 3)
    nxt_gslot = (i + 1) % 2
    gt = gt_ref[...]
    gf = gf_ref[...]

    def body(g, carry):
        t0 = pl.multiple_of(g * grp, grp)
        rows = pl.ds(t0, grp)
        acc_lo = y_scr[rows, 0:half]
        acc_hi = y_scr[rows, half:]
        wrow = w_scr[rows, :]
        for k in range(n_pick):
            src = gbuf.at[gslot, k]
            words = jnp.concatenate(
                [src[pl.ds(t0 * spr + s, grp, stride=spr), :] for s in range(spr)], axis=1)
            rlo, rhi = _unpack_halves(words)
            wk = wrow[:, k:k + 1]
            acc_lo = acc_lo + wk * rlo
            acc_hi = acc_hi + wk * rhi
        x2 = x1_ref[rows, :] + gt * jnp.concatenate([acc_lo, acc_hi], axis=1)
        ms = jnp.mean(x2 * x2, axis=-1, keepdims=True)
        o_ref[rows, :] = x2 * lax.rsqrt(ms + EPS) * gf
        issue_rows(nxt_islot, nxt_gslot, t0)
        return carry

    lax.fori_loop(0, tc // grp, body, 0)

    @pl.when(i == nt - 1)
    def _():
        gather_wait(nxt_gslot)


def _combine(dest, ys, top_w_t, x1, hnp, gt2, g_final, ws_gu, ws_d, seq, tc=256):
    t, d = x1.shape
    half = d // 2
    k = dest.shape[0]
    tpb = seq // tc
    de2 = ws_gu.shape[1]
    return pl.pallas_call(
        _combine_kernel,
        grid=(t // tc,),
        in_specs=[pl.BlockSpec(memory_space=pl.ANY),
                  pl.BlockSpec(memory_space=pl.ANY),
                  pl.BlockSpec((k, tc), lambda i: (0, i)),
                  pl.BlockSpec((tc, d), lambda i: (i, 0)),
                  pl.BlockSpec((tc * (half // LANES), LANES), lambda i: (i, 0)),
                  pl.BlockSpec((None, 1, d), lambda i: (i // tpb, 0, 0)),
                  pl.BlockSpec((1, d), lambda i: (0, 0)),
                  pl.BlockSpec((d, de2), lambda i: (0, 0)),
                  pl.BlockSpec((de2 // 2, d), lambda i: (0, 0))],
        out_specs=pl.BlockSpec((tc, d), lambda i: (i, 0)),
        out_shape=jax.ShapeDtypeStruct((t, d), F32),
        scratch_shapes=[pltpu.SMEM((3, k, tc), I32),
                        pltpu.VMEM((2, k, tc * (half // LANES), LANES), U32),
                        pltpu.VMEM((tc, d), F32),
                        pltpu.VMEM((tc, LANES), F32),
                        pltpu.SemaphoreType.DMA((3,)),
                        pltpu.SemaphoreType.DMA((2,))],
        compiler_params=_cparams(("arbitrary",)),
        name="combine",
    )(dest, ys, top_w_t, x1, hnp, gt2, g_final.reshape(1, d), ws_gu, ws_d)


def kernel(x, c, w_ada, b_ada, g_mix, w_in, conv_w, conv_b, conv_ln_g, conv_ln_b, qk_conv_w, qk_conv_b, b_igate, b_fgate, mlstm_norm_g, w_out, g_ffn, w_router, router_bias, w_gate_up, w_down, ws_gate_up, ws_down, g_final):
    bsz, seq, d = x.shape
    depth = w_ada.shape[0]
    assert depth == 1, "single-layer block"
    dc = conv_w.shape[2]
    dm = d - dc
    nh = b_igate.shape[1]
    dh = dm // nh
    n_exp = w_router.shape[2]
    t = bsz * seq
    n_main = 2 * dc + 4 * dm
    n_assign = TOP_K * t
    mb = MOE_BLOCK
    assert dc == dm and nh == N_HEADS and 2 * nh <= LANES and n_assign % mb == 0

    mod = _ada(c, w_ada[0], b_ada[0])
    sh1, sc1, gt1, sh2, sc2, gt2 = [m.reshape(bsz, 1, d) for m in jnp.split(mod, 6, axis=-1)]

    x2 = x.reshape(t, d)
    w_main = w_in[0, :, :n_main].astype(BF16)
    w_gate = jnp.zeros((d, LANES), BF16).at[:, :2 * nh].set(w_in[0, :, n_main:].astype(BF16))
    proj, gates = _inproj(x2, sc1, sh1, g_mix[0].reshape(1, d), w_main, w_gate, seq)

    u = _convmod(proj, conv_w[0], conv_b[0], conv_ln_g[0], conv_ln_b[0], bsz, seq)

    gbias = jnp.zeros((1, LANES), F32).at[0, :nh].set(b_igate[0]).at[0, nh:2 * nh].set(b_fgate[0])
    hm = _mlstm(proj, gates, gbias, qk_conv_w[0], qk_conv_b[0], mlstm_norm_g[0],
                bsz, seq, dm, dh, col0=2 * dc // dm)

    x1, hnp, top_idx_t, top_w_t, cnt = _outproj(
        u, hm, x2, gt1, sc2, sh2, g_ffn[0].reshape(1, d), w_out[0].astype(BF16),
        w_router[0], router_bias[0], seq)

    sizes = jnp.sum(cnt, axis=1).astype(I32)
    starts = jnp.cumsum(sizes) - sizes
    dest = _plan(top_idx_t, starts.astype(F32).reshape(n_exp, 1))
    xs = _dispatch(dest, hnp, n_assign + mb, mb, d // 2 // LANES)
    block_off, nblk, bstart, n_used = _block_offsets(sizes, starts, mb, n_assign // mb + n_exp)
    ys = _moe(xs, block_off, nblk, bstart, n_used, w_gate_up[0], w_down[0], mb)

    out = _combine(dest, ys, top_w_t, x1, hnp, gt2, g_final,
                   ws_gate_up[0].astype(BF16), ws_down[0].astype(BF16), seq)
    return out.reshape(bsz, seq, d)
```
